```python
import math
import jax, jax.numpy as jnp
from jax import lax
import numpy as np

D_MODEL = 2048
BATCH = 8
SEQ = 4096
DEPTH = 4
DEC_BATCH = 4
DEC_SEQ = 8192
PAST_LEN = 128

N_MIXERS = 2
N_RWKV_LAYERS = (DEPTH + N_MIXERS - 1) // N_MIXERS
N_ATTN_LAYERS = DEPTH // N_MIXERS
RW_HEAD = 64
RW_HEADS = D_MODEL // RW_HEAD
DECAY_LORA = max(32, int(round(D_MODEL ** 0.5 * 1.8 / 32)) * 32)
AAA_LORA = max(32, int(round(D_MODEL ** 0.5 * 1.8 / 32)) * 32)
MV_LORA = max(32, int(round(D_MODEL ** 0.5 * 1.3 / 32)) * 32)
GATE_LORA = max(32, int(round(D_MODEL ** 0.8 * 0.6 / 32)) * 32)
LNX_EPS = 64e-5
ATT_HEAD = 128
ATT_HEADS = D_MODEL // ATT_HEAD
DIL_PATTERNS = ((128, 1), (512, 4), (2048, 16))
N_GROUPS = len(DIL_PATTERNS)
Q_BLOCK = 64
D_FF = -(-8 * D_MODEL // (3 * 256)) * 256
RMS_EPS = 1e-6

kernel_name = 'hybrid_rwkv7_dilated_alibi_encoder'


def _rmsnorm(x, g):
    xf = x.astype(jnp.float32)
    y = xf * lax.rsqrt(jnp.mean(xf * xf, axis=-1, keepdims=True) + RMS_EPS)
    return (y * g.astype(jnp.float32)).astype(x.dtype)


def _swiglu(h, w_gate, w_up, w_down):
    return (jax.nn.silu(h @ w_gate) * (h @ w_up)) @ w_down


def _heads(z):
    return z.reshape(z.shape[:-1] + (RW_HEADS, RW_HEAD)).astype(jnp.float32)


def _wkv_step(S, inp):
    r, w, k, v, kk, b = inp
    sa = jnp.einsum('bhij,bhj->bhi', S, -kk)
    S = S * w[:, :, None, :] + sa[:, :, :, None] * b[:, :, None, :] + v[:, :, :, None] * k[:, :, None, :]
    return S, jnp.einsum('bhij,bhj->bhi', S, r)


def _rwkv7_mix(h, v_first, v_res, mu, w_rkv, w0, w1, w2, a0, a1, a2, g1, g2, k_k, k_a, r_k, lnx_w, lnx_b, w_o):
    B, T, D = h.shape
    zero = jnp.zeros_like(h[:, :1])
    xx = 0.5 * (jnp.concatenate([zero, h[:, :-1]], axis=1) + jnp.concatenate([h[:, 1:], zero], axis=1)) - h
    xs = h[None] + xx[None] * mu[:, None, None, :]
    r, k, v = jnp.einsum('cbtd,cde->cbte', xs[:3], w_rkv)
    xv, xw, xa, xg = xs[2], xs[3], xs[4], xs[5]
    w_log = -jax.nn.softplus(-(w0[:, None, None, :] + jnp.einsum('zbtr,zrd->zbtd', jnp.tanh(jnp.einsum('btd,zdr->zbtr', xw, w1)), w2))) - 0.5
    a = jax.nn.sigmoid(a0[:, None, None, :] + jnp.einsum('zbtr,zrd->zbtd', jnp.einsum('btd,zdr->zbtr', xa, a1), a2))
    if v_res is None:
        v_first = v
    else:
        v0, v1, v2 = v_res
        v = v + (v_first - v) * jax.nn.sigmoid(v0 + (xv @ v1) @ v2)
    g = jax.nn.sigmoid(xg @ g1) @ g2
    kk = _heads(k * k_k)
    kk = kk / jnp.maximum(jnp.sqrt(jnp.sum(kk * kk, axis=-1, keepdims=True)), 1e-12)
    kd = _heads(k[None] * (1 + (a - 1) * k_a))
    decay = jnp.exp(-jnp.exp(_heads(w_log)))
    ad = _heads(a)
    rh, vh = _heads(r), _heads(v)
    r_t, v_t, kk_t = jnp.moveaxis(rh, 1, 0), jnp.moveaxis(vh, 1, 0), jnp.moveaxis(kk, 1, 0)
    S0 = jnp.zeros((B, RW_HEADS, RW_HEAD, RW_HEAD), jnp.float32)
    ys = []
    for z in range(2):
        _, yz = lax.scan(_wkv_step, S0, (r_t, jnp.moveaxis(decay[z], 1, 0), jnp.moveaxis(kd[z], 1, 0), v_t, kk_t, jnp.moveaxis(kk * ad[z], 1, 0)), reverse=(z == 1))
        ys.append(yz)
    y = jnp.moveaxis(ys[0] + ys[1], 0, 1)
    mean = jnp.mean(y, axis=-1, keepdims=True)
    var = jnp.mean(jnp.square(y - mean), axis=-1, keepdims=True)
    yn = ((y - mean) * lax.rsqrt(var + LNX_EPS)).reshape(B, T, D) * lnx_w + lnx_b
    bonus = jnp.sum(rh * (kd[0] + kd[1]) * r_k, axis=-1, keepdims=True) * vh
    out = ((yn + bonus.reshape(B, T, D)) * g).astype(h.dtype) @ w_o
    return out, v_first


def _alibi_slopes(n):
    return jnp.exp2(-8.0 * jnp.arange(1, n + 1, dtype=jnp.float32) / n)


def _dilated_band(q, k, v, window, dil, slopes):
    B, T, H, E = q.shape
    L = T // dil
    R = (window // 2) // dil
    blk = math.gcd(L, Q_BLOCK)
    nblk = L // blk
    nb = -(-R // blk)
    K = (2 * nb + 1) * blk

    def to_sub(z):
        return z.reshape(B, L, dil, H, E).transpose(0, 2, 3, 1, 4)

    def neigh(z):
        zp = jnp.pad(to_sub(z), ((0, 0), (0, 0), (0, 0), (nb * blk, nb * blk), (0, 0)))
        zp = zp.reshape(B, dil, H, nblk + 2 * nb, blk, E)
        return jnp.concatenate([zp[:, :, :, j:j + nblk] for j in range(2 * nb + 1)], axis=4)

    qs = to_sub(q).reshape(B, dil, H, nblk, blk, E)
    kn, vn = neigh(k), neigh(v)
    qi = jnp.arange(blk)[:, None]
    rel = jnp.arange(K)[None, :] - nb * blk - qi
    kpos = jnp.arange(nblk)[:, None, None] * blk + qi[None] + rel[None]
    valid = (jnp.abs(rel) <= R)[None] & (kpos >= 0) & (kpos < L)
    bias = -slopes[:, None, None, None] * (dil * jnp.abs(rel)).astype(jnp.float32)
    s = jnp.einsum('bchnie,bchnke->bchnik', qs, kn).astype(jnp.float32) * (ATT_HEAD ** -0.5) + bias
    s = jnp.where(valid, s, -jnp.inf)
    m = jnp.max(s, axis=-1, keepdims=True)
    p = jnp.exp(s - m)
    l = jnp.sum(p, axis=-1)
    o = jnp.einsum('bchnik,bchnke->bchnie', p.astype(vn.dtype), vn).astype(jnp.float32) / l[..., None]
    lse = m[..., 0] + jnp.log(l)
    o = o.reshape(B, dil, H, L, E).transpose(0, 3, 1, 2, 4).reshape(B, T, H, E)
    lse = lse.reshape(B, dil, H, L).transpose(0, 3, 1, 2).reshape(B, T, H)
    return o, lse


def _dilated_attention(h, w_qkv, w_o):
    B, T, _ = h.shape
    qkv = jnp.einsum('btd,dgshe->btgshe', h, w_qkv)
    slopes = _alibi_slopes(ATT_HEADS)
    outs, lses = [], []
    for gi, (window, dil) in enumerate(DIL_PATTERNS):
        o, lse = _dilated_band(qkv[:, :, gi, 0], qkv[:, :, gi, 1], qkv[:, :, gi, 2], window, dil, slopes)
        outs.append(o)
        lses.append(lse)
    wgt = jax.nn.softmax(jnp.stack(lses), axis=0)
    o = jnp.sum(wgt[..., None] * jnp.stack(outs), axis=0)
    return o.reshape(B, T, ATT_HEADS * ATT_HEAD).astype(h.dtype) @ w_o


def _trunk(x, ln1, ln2, ln_f, rw_mu, rw_w_rkv, rw_w0, rw_w1, rw_w2, rw_a0, rw_a1, rw_a2, rw_v0, rw_v1, rw_v2, rw_g1, rw_g2, rw_k_k, rw_k_a, rw_r_k, rw_lnx_w, rw_lnx_b, rw_w_o, at_w_qkv, at_w_o, ffn_w_gate, ffn_w_up, ffn_w_down):
    v_first = None
    for i in range(DEPTH):
        h = _rmsnorm(x, ln1[i])
        j = i // N_MIXERS
        if i % N_MIXERS == 0:
            v_res = None if j == 0 else (rw_v0[j - 1], rw_v1[j - 1], rw_v2[j - 1])
            mix, v_first = _rwkv7_mix(h, v_first, v_res, rw_mu[j], rw_w_rkv[j], rw_w0[j], rw_w1[j], rw_w2[j], rw_a0[j], rw_a1[j], rw_a2[j], rw_g1[j], rw_g2[j], rw_k_k[j], rw_k_a[j], rw_r_k[j], rw_lnx_w[j], rw_lnx_b[j], rw_w_o[j])
        else:
            mix = _dilated_attention(h, at_w_qkv[j], at_w_o[j])
        x = x + mix
        x = x + _swiglu(_rmsnorm(x, ln2[i]), ffn_w_gate[i], ffn_w_up[i], ffn_w_down[i])
    return _rmsnorm(x, ln_f)


def setup_inputs(seed: int = 0) -> dict:
    key = jax.random.key(seed)
    ks = jax.random.split(key, 32)
    f32 = jnp.float32
    D, NR, NA = D_MODEL, N_RWKV_LAYERS, N_ATTN_LAYERS
    nrm = lambda i, shape, scale: jax.random.normal(ks[i], shape, f32) * scale
    return {
        'x_prompt': nrm(0, (BATCH, SEQ, D), 1.0),
        'x_sample': nrm(1, (DEC_BATCH, DEC_SEQ, D), 1.0),
        'ln1': 1.0 + nrm(2, (DEPTH, D), 0.02),
        'ln2': 1.0 + nrm(3, (DEPTH, D), 0.02),
        'ln_f': 1.0 + nrm(4, (D,), 0.02),
        'rw_mu': jax.random.uniform(ks[5], (NR, 6, D), f32),
        'rw_w_rkv': nrm(6, (NR, 3, D, D), D ** -0.5),
        'rw_w0': jax.random.uniform(ks[7], (NR, 2, D), f32, -6.0, -1.0),
        'rw_w1': nrm(8, (NR, 2, D, DECAY_LORA), D ** -0.5),
        'rw_w2': nrm(9, (NR, 2, DECAY_LORA, D), 0.1 * DECAY_LORA ** -0.5),
        'rw_a0': nrm(10, (NR, 2, D), 0.5),
        'rw_a1': nrm(11, (NR, 2, D, AAA_LORA), D ** -0.5),
        'rw_a2': nrm(12, (NR, 2, AAA_LORA, D), 0.5 * AAA_LORA ** -0.5),
        'rw_v0': nrm(13, (NR - 1, D), 0.5),
        'rw_v1': nrm(14, (NR - 1, D, MV_LORA), D ** -0.5),
        'rw_v2': nrm(15, (NR - 1, MV_LORA, D), 0.5 * MV_LORA ** -0.5),
        'rw_g1': nrm(16, (NR, D, GATE_LORA), D ** -0.5),
        'rw_g2': nrm(17, (NR, GATE_LORA, D), GATE_LORA ** -0.5),
        'rw_k_k': 0.85 + nrm(18, (NR, D), 0.05),
        'rw_k_a': 1.0 + nrm(19, (NR, D), 0.05),
        'rw_r_k': nrm(20, (NR, RW_HEADS, RW_HEAD), 0.1),
        'rw_lnx_w': 1.0 + nrm(21, (NR, D), 0.02),
        'rw_lnx_b': nrm(22, (NR, D), 0.01),
        'rw_w_o': nrm(23, (NR, D, D), D ** -0.5),
        'at_w_qkv': nrm(24, (NA, D, N_GROUPS, 3, ATT_HEADS, ATT_HEAD), D ** -0.5),
        'at_w_o': nrm(25, (NA, ATT_HEADS * ATT_HEAD, D), (ATT_HEADS * ATT_HEAD) ** -0.5),
        'ffn_w_gate': nrm(26, (DEPTH, D, D_FF), D ** -0.5),
        'ffn_w_up': nrm(27, (DEPTH, D, D_FF), D ** -0.5),
        'ffn_w_down': nrm(28, (DEPTH, D_FF, D), D_FF ** -0.5),
    }


def reference(x_prompt, x_sample, ln1, ln2, ln_f, rw_mu, rw_w_rkv, rw_w0, rw_w1, rw_w2, rw_a0, rw_a1, rw_a2, rw_v0, rw_v1, rw_v2, rw_g1, rw_g2, rw_k_k, rw_k_a, rw_r_k, rw_lnx_w, rw_lnx_b, rw_w_o, at_w_qkv, at_w_o, ffn_w_gate, ffn_w_up, ffn_w_down):
    weights = (ln1, ln2, ln_f, rw_mu, rw_w_rkv, rw_w0, rw_w1, rw_w2, rw_a0, rw_a1, rw_a2, rw_v0, rw_v1, rw_v2, rw_g1, rw_g2, rw_k_k, rw_k_a, rw_r_k, rw_lnx_w, rw_lnx_b, rw_w_o, at_w_qkv, at_w_o, ffn_w_gate, ffn_w_up, ffn_w_down)
    y_prompt = _trunk(x_prompt, *weights)
    y_sample = _trunk(x_sample, *weights)
    return (y_prompt, y_sample)
```

```python
import functools
import math

import jax
import jax.numpy as jnp
from jax import lax
from jax.experimental import pallas as pl
from jax.experimental.pallas import tpu as pltpu

F32 = jnp.float32
BF16 = jnp.bfloat16

RW_HEAD = 64
PAIR = 2 * RW_HEAD
ATT_HEAD = 128
DIL_PATTERNS = ((128, 1), (512, 4), (2048, 16))
BAND = 64
N_GROUPS = len(DIL_PATTERNS)
LNX_EPS = 64e-5
RMS_EPS = 1e-6
CHUNK = 64
NEG_BIG = -1e30
WKV_UNROLL = 4
V7X_VMEM_LIMIT_BYTES = 56 * 1024 * 1024


def _pick(n, candidates):
    for c in candidates:
        if n % c == 0:
            return c
    raise ValueError(f"no tile in {candidates} divides {n}")


def _cparams(sem):
    return pltpu.CompilerParams(dimension_semantics=sem, vmem_limit_bytes=V7X_VMEM_LIMIT_BYTES)


def _dot(a, b):
    return jnp.dot(a, b, preferred_element_type=F32)


def _dot_nt(a, b):
    return lax.dot_general(a, b, (((1,), (1,)), ((), ())), preferred_element_type=F32)


def _dot_tn(a, b):
    return lax.dot_general(a, b, (((0,), (0,)), ((), ())), preferred_element_type=F32)


def _split(x):
    hi = x.astype(BF16)
    lo = (x - hi.astype(F32)).astype(BF16)
    return hi, lo


def _sigmoid(x):
    return 1.0 / (1.0 + jnp.exp(-x))


def _rms(x, g):
    return x * lax.rsqrt(jnp.mean(x * x, axis=-1, keepdims=True) + RMS_EPS) * g


def _seq_pos(idx, n1, t1, t2):
    first = idx < n1
    pos = jnp.where(first, lax.rem(idx, t1), lax.rem(idx - n1, t2))
    return pos, jnp.where(first, t1, t2)


def _mm_kernel(a_ref, w_ref, o_ref):
    o_ref[...] = _dot(a_ref[...], w_ref[...]).astype(o_ref.dtype)


def _matmul_batched(a, w, groups, out_dtype):
    _, m, k = a.shape
    n = w.shape[2]
    tm = _pick(m, (1024, 512, 256, 128))
    tn = _pick(n, (1024, 512, 256, 128))
    return pl.pallas_call(
        _mm_kernel,
        out_shape=jax.ShapeDtypeStruct((groups, m, n), out_dtype),
        grid=(groups, m // tm, n // tn),
        in_specs=[pl.BlockSpec((None, tm, k), lambda g, i, j: (g, i, 0)),
                  pl.BlockSpec((None, k, tn), lambda g, i, j: (g, 0, j))],
        out_specs=pl.BlockSpec((None, tm, tn), lambda g, i, j: (g, i, j)),
        compiler_params=_cparams(("parallel", "parallel", "arbitrary")),
        name="mm_batched",
    )(a, w)


def _mm_norm_kernel(x_ref, g_ref, w_ref, o_ref, h_ref):
    @pl.when((pl.program_id(1) == 0) & (pl.program_id(2) == 0))
    def _():
        h_ref[...] = _rms(x_ref[...], g_ref[...]).astype(BF16)

    o_ref[...] = _dot(h_ref[...], w_ref[...]).astype(o_ref.dtype)


def _matmul_norm_grouped(x, gain, w, ngroups, out_dtype):
    m, k = x.shape
    ng = w.shape[1] // ngroups
    tm = _pick(m, (1024, 512, 256, 128))
    tn = _pick(ng, (1024, 512, 256, 128))
    nj = ng // tn
    return pl.pallas_call(
        _mm_norm_kernel,
        out_shape=jax.ShapeDtypeStruct((ngroups, m, ng), out_dtype),
        grid=(m // tm, ngroups, nj),
        in_specs=[pl.BlockSpec((tm, k), lambda i, g, j: (i, 0)),
                  pl.BlockSpec((1, k), lambda i, g, j: (0, 0)),
                  pl.BlockSpec((k, tn), lambda i, g, j: (0, g * nj + j))],
        out_specs=pl.BlockSpec((None, tm, tn), lambda i, g, j: (g, i, j)),
        scratch_shapes=[pltpu.VMEM((tm, k), BF16)],
        compiler_params=_cparams(("parallel", "arbitrary", "arbitrary")),
        name="mm_norm",
    )(x, gain, w)


def _mm_res_kernel(a_ref, w_ref, r_ref, o_ref):
    o_ref[...] = r_ref[...] + _dot(a_ref[...], w_ref[...])


def _matmul_residual(a, w, res):
    m, k = a.shape
    n = w.shape[1]
    tm = _pick(m, (1024, 512, 256, 128))
    tn = _pick(n, (1024, 512, 256, 128))
    return pl.pallas_call(
        _mm_res_kernel,
        out_shape=jax.ShapeDtypeStruct((m, n), F32),
        grid=(m // tm, n // tn),
        in_specs=[pl.BlockSpec((tm, k), lambda i, j: (i, 0)),
                  pl.BlockSpec((k, tn), lambda i, j: (0, j)),
                  pl.BlockSpec((tm, tn), lambda i, j: (i, j))],
        out_specs=pl.BlockSpec((tm, tn), lambda i, j: (i, j)),
        input_output_aliases={2: 0},
        compiler_params=_cparams(("parallel", "arbitrary")),
        name="mm_residual",
    )(a, w, res)


def _ffn_kernel(x_ref, g_ref, wg_ref, wu_ref, wd_ref, gf_ref, o_ref, h_ref, acc_ref, *, final_norm):
    j = pl.program_id(1)

    @pl.when(j == 0)
    def _():
        x = x_ref[...]
        h_ref[...] = _rms(x, g_ref[...]).astype(BF16)
        acc_ref[...] = x

    h = h_ref[...]
    gate = _dot(h, wg_ref[...])
    up = _dot(h, wu_ref[...])
    act = (gate * _sigmoid(gate) * up).astype(BF16)
    acc_ref[...] += _dot(act, wd_ref[...])

    @pl.when(j == pl.num_programs(1) - 1)
    def _():
        y = acc_ref[...]
        if final_norm:
            y = _rms(y, gf_ref[...])
        o_ref[...] = y


def _ffn(x, gain, wg, wu, wd, final_gain, final_norm):
    m, d = x.shape
    ff = wg.shape[1]
    tm = _pick(m, (512, 256, 128))
    tf = _pick(ff, (512, 256, 128))
    return pl.pallas_call(
        functools.partial(_ffn_kernel, final_norm=final_norm),
        out_shape=jax.ShapeDtypeStruct((m, d), F32),
        grid=(m // tm, ff // tf),
        in_specs=[pl.BlockSpec((tm, d), lambda i, j: (i, 0)),
                  pl.BlockSpec((1, d), lambda i, j: (0, 0)),
                  pl.BlockSpec((d, tf), lambda i, j: (0, j)),
                  pl.BlockSpec((d, tf), lambda i, j: (0, j)),
                  pl.BlockSpec((tf, d), lambda i, j: (j, 0)),
                  pl.BlockSpec((1, d), lambda i, j: (0, 0))],
        out_specs=pl.BlockSpec((tm, d), lambda i, j: (i, 0)),
        scratch_shapes=[pltpu.VMEM((tm, d), BF16), pltpu.VMEM((tm, d), F32)],
        input_output_aliases={0: 0},
        compiler_params=_cparams(("parallel", "arbitrary")),
        name="ffn",
    )(x, gain, wg, wu, wd, final_gain)


def _shift_kernel(x_ref, xp_ref, xn_ref, g_ref, mu_ref, o_ref, *, tt, n1, t1, t2):
    pos0, seq_len = _seq_pos(pl.program_id(0) * tt, n1, t1, t2)
    g = g_ref[...]
    h = _rms(x_ref[...], g)
    h_before = jnp.where(pos0 == 0, 0.0, _rms(xp_ref[7:8, :], g))
    h_after = jnp.where(pos0 + tt == seq_len, 0.0, _rms(xn_ref[0:1, :], g))
    row = lax.broadcasted_iota(jnp.int32, (tt, 1), 0)
    h_prev = jnp.where(row == 0, h_before, pltpu.roll(h, 1, 0))
    h_next = jnp.where(row == tt - 1, h_after, pltpu.roll(h, tt - 1, 0))
    xx = 0.5 * (h_prev + h_next) - h
    for c in range(6):
        o_ref[c] = (h + xx * mu_ref[c:c + 1, :]).astype(BF16)


def _token_shift(x, gain, mu, geom):
    n, d = x.shape
    n1, t1, t2 = geom
    tt = _pick(math.gcd(t1, t2), (256, 128, 64, 32, 16, 8))
    nb8 = n // 8
    return pl.pallas_call(
        functools.partial(_shift_kernel, tt=tt, n1=n1, t1=t1, t2=t2),
        out_shape=jax.ShapeDtypeStruct((6, n, d), BF16),
        grid=(n // tt,),
        in_specs=[pl.BlockSpec((tt, d), lambda i: (i, 0)),
                  pl.BlockSpec((8, d), lambda i: (jnp.maximum(i * (tt // 8) - 1, 0), 0)),
                  pl.BlockSpec((8, d), lambda i: (jnp.minimum((i + 1) * (tt // 8), nb8 - 1), 0)),
                  pl.BlockSpec((1, d), lambda i: (0, 0)),
                  pl.BlockSpec((6, d), lambda i: (0, 0))],
        out_specs=pl.BlockSpec((6, tt, d), lambda i: (0, i, 0)),
        compiler_params=_cparams(("parallel",)),
        name="token_shift",
    )(x, x, x, gain, mu)


def _mid_kernel(*refs, has_vres):
    if has_vres:
        (xv_ref, xw_ref, xa_ref, xg_ref, k_ref, v_ref, vf_ref,
         w0_ref, w1_ref, w2_ref, a0_ref, a1_ref, a2_ref, g1_ref, g2_ref, ka_ref,
         v0_ref, v1_ref, v2_ref,
         logw_ref, kd_ref, a_ref, vo_ref, g_ref) = refs
    else:
        (xv_ref, xw_ref, xa_ref, xg_ref, k_ref, v_ref,
         w0_ref, w1_ref, w2_ref, a0_ref, a1_ref, a2_ref, g1_ref, g2_ref, ka_ref,
         logw_ref, kd_ref, a_ref, vo_ref, g_ref) = refs
    xw = xw_ref[...]
    xa = xa_ref[...]
    k = k_ref[...]
    k_a = ka_ref[...]
    for z in range(2):
        lora = _dot(jnp.tanh(_dot(xw, w1_ref[z])).astype(BF16), w2_ref[z])
        y = -(w0_ref[z:z + 1, :] + lora)
        softplus = jnp.maximum(y, 0.0) + jnp.log(1.0 + jnp.exp(-jnp.abs(y)))
        logw_ref[z] = -jnp.exp(-softplus - 0.5)
        aa = _sigmoid(a0_ref[z:z + 1, :] + _dot(_dot(xa, a1_ref[z]).astype(BF16), a2_ref[z]))
        a_ref[z] = aa.astype(a_ref.dtype)
        kd_ref[z] = (k * (1.0 + (aa - 1.0) * k_a)).astype(kd_ref.dtype)
    v = v_ref[...]
    if has_vres:
        gate = _sigmoid(v0_ref[...] + _dot(_dot(xv_ref[...], v1_ref[...]).astype(BF16), v2_ref[...]))
        v = v + (vf_ref[...] - v) * gate
    vo_ref[...] = v.astype(vo_ref.dtype)
    g_ref[...] = _dot(_sigmoid(_dot(xg_ref[...], g1_ref[...])).astype(BF16), g2_ref[...]).astype(g_ref.dtype)


def _rwkv_mid(xs, rkv, v_first, p, vres):
    _, n, d = xs.shape
    tm = _pick(n, (128, 64, 32, 16, 8))
    has_vres = vres is not None

    def slab(c):
        return pl.BlockSpec((None, tm, d), lambda i, c=c: (c, i, 0))

    def whole(a):
        nd = a.ndim
        return pl.BlockSpec(a.shape, lambda i, nd=nd: (0,) * nd)

    tok = pl.BlockSpec((tm, d), lambda i: (i, 0))
    tok2 = pl.BlockSpec((2, tm, d), lambda i: (0, i, 0))
    weights = [p["w0"], p["w1"], p["w2"], p["a0"], p["a1"], p["a2"], p["g1"], p["g2"], p["k_a"]]
    args = [xs, xs, xs, xs, rkv, rkv]
    specs = [slab(2), slab(3), slab(4), slab(5), slab(1), slab(2)]
    if has_vres:
        args.append(v_first)
        specs.append(slab(2))
        weights += list(vres)
    args += weights
    specs += [whole(a) for a in weights]
    return pl.pallas_call(
        functools.partial(_mid_kernel, has_vres=has_vres),
        out_shape=(jax.ShapeDtypeStruct((2, n, d), F32),
                   jax.ShapeDtypeStruct((2, n, d), BF16),
                   jax.ShapeDtypeStruct((2, n, d), BF16),
                   jax.ShapeDtypeStruct((n, d), BF16),
                   jax.ShapeDtypeStruct((n, d), BF16)),
        grid=(n // tm,),
        in_specs=specs,
        out_specs=(tok2, tok2, tok2, tok, tok),
        compiler_params=_cparams(("parallel",)),
        name="rwkv_mid",
    )(*args)


def _wkv1_kernel(r_ref, lw_ref, kd_ref, v_ref, k_ref, a_ref, kk_ref,
                 rp_ref, y0_ref, g_ref, h_ref, *, nchunk, npair, unroll, reverse):
    c2 = 2 * CHUNK
    lane = lax.broadcasted_iota(jnp.int32, (CHUNK, PAIR), 1)
    head0 = lane < RW_HEAD
    row_c = lax.broadcasted_iota(jnp.int32, (CHUNK, PAIR), 0)
    diag_c = row_c == jnp.where(head0, lane, lane - RW_HEAD)
    r2 = lax.broadcasted_iota(jnp.int32, (c2, c2), 0)
    q2 = lax.broadcasted_iota(jnp.int32, (c2, c2), 1)
    same = (r2 < CHUNK) == (q2 < CHUNK)
    if reverse:
        strict = same & (r2 < q2)
        incl = same & (r2 <= q2)
    else:
        strict = same & (r2 > q2)
        incl = same & (r2 >= q2)
    eye2 = (r2 == q2).astype(F32)
    ones_bd = same.astype(BF16)
    t1 = lax.broadcasted_iota(jnp.int32, (CHUNK, CHUNK), 0)
    s1 = lax.broadcasted_iota(jnp.int32, (CHUNK, CHUNK), 1)
    tri = ((t1 <= s1) if reverse else (t1 >= s1)).astype(BF16)
    last = 0 if reverse else CHUNK - 1

    def bd(x):
        return jnp.concatenate([jnp.where(head0, x, 0.0), jnp.where(head0, 0.0, x)], axis=0)

    def fold(x):
        return x[:CHUNK] + x[CHUNK:]

    def prepare(rows, ln, lw, cum):
        kk = k_ref[rows, ln] * kk_ref[:, ln]
        ss = _dot((kk * kk).astype(BF16), ones_bd)
        kk = kk / jnp.maximum(jnp.sqrt(ss), 1e-12)
        b = kk * a_ref[rows, ln].astype(F32)
        kd = kd_ref[rows, ln].astype(F32)
        tot = cum[last:last + 1, :]
        e_neg = jnp.exp(-cum)
        e_tail = jnp.exp(tot - cum)
        w = dict(tot=tot)
        w["at"] = bd(-kk * jnp.exp(cum - lw)).astype(BF16)
        w["rt"] = bd(r_ref[rows, ln] * jnp.exp(cum))
        w["bt"] = (b * e_neg).astype(BF16)
        w["kt"] = (kd * e_neg).astype(BF16)
        w["bh"] = bd(b * e_tail).astype(BF16)
        w["kh"] = bd(kd * e_tail).astype(BF16)
        w["vb"] = bd(v_ref[rows, ln].astype(F32)).astype(BF16)
        return w

    def chunk_body(ci, carry):
        items = []
        for u in range(unroll):
            rows = pl.ds(pl.multiple_of((ci * unroll + u) * CHUNK, CHUNK), CHUNK)
            lw_all = lw_ref[rows, :]
            hi, lo = _split(lw_all)
            cum_all = _dot(tri, hi) + _dot(tri, lo)
            for p in range(npair):
                ln = slice(p * PAIR, (p + 1) * PAIR)
                w = prepare(rows, ln, lw_all[:, ln], cum_all[:, ln])
                w["rows"], w["ln"] = rows, ln
                items.append(w)
        for w in items:
            m1 = _dot_nt(jnp.concatenate([w["at"], w["rt"].astype(BF16)], axis=0),
                         jnp.concatenate([w["bt"], w["bt"], w["kt"], w["kt"]], axis=0))
            a_ab = jnp.where(strict, m1[:c2, :c2], 0.0)
            w["a_kv"] = jnp.concatenate([jnp.where(strict, m1[:c2, c2:], 0.0),
                                         jnp.where(incl, m1[c2:, c2:], 0.0)], axis=0).astype(BF16)
            w["a_rb"] = jnp.where(incl, m1[c2:, :c2], 0.0).astype(BF16)
            w["pw"] = a_ab.astype(BF16)
            w["inv"] = eye2 + a_ab
        for f in range(1, int(math.log2(CHUNK))):
            for w in items:
                w["pw"] = _dot(w["pw"], w["pw"]).astype(BF16)
            for w in items:
                w["inv"] = w["inv"] + _dot(w["inv"].astype(BF16), w["pw"])
        for w in items:
            w["av"] = _dot(w["a_kv"], w["vb"])
        for w in items:
            wu = _dot(w["inv"].astype(BF16), jnp.concatenate([w["at"], w["av"][:c2].astype(BF16)], axis=1))
            w["wu"] = wu.astype(BF16)
        for w in items:
            rows, ln = w["rows"], w["ln"]
            rw = _dot(w["a_rb"], w["wu"])
            rp_ref[rows, ln] = fold(w["rt"] + rw[:, :PAIR]).astype(rp_ref.dtype)
            y0_ref[rows, ln] = fold(rw[:, PAIR:] + w["av"][c2:])
            gpart = _dot_tn(w["bh"], w["wu"][:, :PAIR])
            g_ref[rows, ln] = (fold(gpart) + jnp.where(diag_c, jnp.exp(w["tot"]), 0.0)).astype(g_ref.dtype)
            hpart = _dot_tn(jnp.concatenate([w["bh"], w["kh"]], axis=0),
                            jnp.concatenate([w["wu"][:, PAIR:], w["vb"]], axis=0))
            h_ref[rows, ln] = fold(hpart)
        return carry

    lax.fori_loop(0, nchunk // unroll, chunk_body, 0)


def _wkv_stage1(r3, r_slab, logw, kd, v, k3, k_slab, a, kk_scale, z, reverse):
    n, d = v.shape
    npair = _pick(d // PAIR, (4, 2, 1))
    lanes = npair * PAIR
    tb = _pick(n, (512, 256, 128, 64))

    def slab(c):
        return pl.BlockSpec((None, tb, lanes), lambda i, j, c=c: (c, i, j))

    tok = pl.BlockSpec((tb, lanes), lambda i, j: (i, j))
    return pl.pallas_call(
        functools.partial(_wkv1_kernel, nchunk=tb // CHUNK, npair=npair, unroll=WKV_UNROLL, reverse=reverse),
        out_shape=(jax.ShapeDtypeStruct((n, d), BF16),
                   jax.ShapeDtypeStruct((n, d), F32),
                   jax.ShapeDtypeStruct((n, d), BF16),
                   jax.ShapeDtypeStruct((n, d), F32)),
        grid=(n // tb, d // lanes),
        in_specs=[slab(r_slab), slab(z), slab(z), tok, slab(k_slab), slab(z),
                  pl.BlockSpec((1, lanes), lambda i, j: (0, j))],
        out_specs=(tok, tok, tok, tok),
        compiler_params=_cparams(("parallel", "parallel")),
        name="wkv_stage1_rev" if reverse else "wkv_stage1_fwd",
    )(r3, logw, kd, v, k3, a, kk_scale)


def _wkv2_kernel(rp_ref, y0_ref, g_ref, h_ref, y_ref, st_ref, *, npairs, nchunks, n1, t1, t2, reverse):
    i = pl.program_id(0)
    chunk = (nchunks - 1 - i) if reverse else i
    pos0, seq_len = _seq_pos(chunk * CHUNK, n1, t1, t2)
    fresh = (pos0 + CHUNK == seq_len) if reverse else (pos0 == 0)

    @pl.when(fresh)
    def _():
        st_ref[...] = jnp.zeros_like(st_ref)

    lane = lax.broadcasted_iota(jnp.int32, (CHUNK, PAIR), 1)
    head0 = lane < RW_HEAD

    def bd(x):
        zero = jnp.zeros_like(x)
        return jnp.concatenate([jnp.where(head0, x, zero), jnp.where(head0, zero, x)], axis=0)

    for p in range(npairs):
        ln = slice(p * PAIR, (p + 1) * PAIR)
        st = st_ref[p]
        s_hi, s_lo = _split(st)
        rp = rp_ref[:, ln]
        y_ref[:, ln] = y0_ref[:, ln] + _dot(rp, s_hi) + _dot(rp, s_lo)
        g = bd(g_ref[:, ln])
        st_ref[p] = _dot(g, s_hi) + _dot(g, s_lo) + bd(h_ref[:, ln])


def _wkv_stage2(rp, y0, g, h, geom, reverse):
    n, d = rp.shape
    n1, t1, t2 = geom
    nchunks = n // CHUNK
    npairs = d // PAIR

    def imap(i):
        return ((nchunks - 1 - i) if reverse else i, 0)

    tok = pl.BlockSpec((CHUNK, d), imap)
    return pl.pallas_call(
        functools.partial(_wkv2_kernel, npairs=npairs, nchunks=nchunks, n1=n1, t1=t1, t2=t2, reverse=reverse),
        out_shape=jax.ShapeDtypeStruct((n, d), F32),
        grid=(nchunks,),
        in_specs=[tok, tok, tok, tok],
        out_specs=tok,
        scratch_shapes=[pltpu.VMEM((npairs, PAIR, PAIR), F32)],
        compiler_params=_cparams(("arbitrary",)),
        name="wkv_stage2_rev" if reverse else "wkv_stage2_fwd",
    )(rp, y0, g, h)


def _post_kernel(yf_ref, yb_ref, r_ref, kd_ref, v_ref, g_ref, rk_ref, lw_ref, lb_ref, o_ref, *, ngroups):
    r2 = lax.broadcasted_iota(jnp.int32, (PAIR, PAIR), 0)
    q2 = lax.broadcasted_iota(jnp.int32, (PAIR, PAIR), 1)
    same = (r2 < RW_HEAD) == (q2 < RW_HEAD)
    ones_bd = same.astype(BF16)
    mean_bd = (same.astype(F32) * (1.0 / RW_HEAD)).astype(BF16)
    for p in range(ngroups):
        ln = slice(p * PAIR, (p + 1) * PAIR)
        y = yf_ref[:, ln] + yb_ref[:, ln]
        hi, lo = _split(y)
        mean = _dot(hi, mean_bd) + _dot(lo, mean_bd)
        dev = y - mean
        hi, lo = _split(dev * dev)
        var = _dot(hi, mean_bd) + _dot(lo, mean_bd)
        yn = dev * lax.rsqrt(var + LNX_EPS) * lw_ref[:, ln] + lb_ref[:, ln]
        kd = kd_ref[0, :, ln].astype(F32) + kd_ref[1, :, ln].astype(F32)
        hi, lo = _split(r_ref[:, ln] * kd * rk_ref[:, ln])
        bonus = (_dot(hi, ones_bd) + _dot(lo, ones_bd)) * v_ref[:, ln].astype(F32)
        o_ref[:, ln] = ((yn + bonus) * g_ref[:, ln].astype(F32)).astype(o_ref.dtype)


def _rwkv_post(yf, yb, rkv, kd, v, g, r_k, lnx_w, lnx_b):
    n, d = yf.shape
    tm = _pick(n, (256, 128, 64, 32, 16, 8))
    tok = pl.BlockSpec((tm, d), lambda i: (i, 0))
    row = pl.BlockSpec((1, d), lambda i: (0, 0))
    return pl.pallas_call(
        functools.partial(_post_kernel, ngroups=d // PAIR),
        out_shape=jax.ShapeDtypeStruct((n, d), BF16),
        grid=(n // tm,),
        in_specs=[tok, tok, pl.BlockSpec((None, tm, d), lambda i: (0, i, 0)),
                  pl.BlockSpec((2, tm, d), lambda i: (0, i, 0)), tok, tok, row, row, row],
        out_specs=tok,
        compiler_params=_cparams(("parallel",)),
        name="rwkv_post",
    )(yf, yb, rkv, kd, v, g, r_k, lnx_w, lnx_b)


def _band_kernel(q_ref, kp_ref, kc_ref, kn_ref, vp_ref, vc_ref, vn_ref, o_ref, lse_ref,
                 *, tq, dil, nheads, n1l, l1, l2):
    pos0, seq_len = _seq_pos(pl.program_id(0) * tq, n1l, l1, l2)
    prev_ok = pos0 > 0
    next_ok = pos0 + tq < seq_len
    nk = tq + 2 * BAND
    row = lax.broadcasted_iota(jnp.int32, (tq, nk), 0)
    col = lax.broadcasted_iota(jnp.int32, (tq, nk), 1)
    dist = jnp.abs(col - BAND - row)
    valid = (dist <= BAND) & ((col >= BAND) | prev_ok) & ((col < tq + BAND) | next_ok)
    alibi = dist.astype(F32) * float(dil)
    head_lane = lax.broadcasted_iota(jnp.int32, (1, ATT_HEAD), 1)
    lse_all = jnp.zeros((tq, ATT_HEAD), F32)
    scale = ATT_HEAD ** -0.5
    for h in range(nheads):
        ln = slice(h * ATT_HEAD, (h + 1) * ATT_HEAD)
        keys = jnp.concatenate([kp_ref[:, ln], kc_ref[:, ln], kn_ref[:, ln]], axis=0)
        vals = jnp.concatenate([vp_ref[:, ln], vc_ref[:, ln], vn_ref[:, ln]], axis=0)
        slope = 2.0 ** (-8.0 * (h + 1) / nheads)
        s = _dot_nt(q_ref[:, ln], keys) * scale - slope * alibi
        s = jnp.where(valid, s, NEG_BIG)
        m = jnp.max(s, axis=-1, keepdims=True)
        pr = jnp.exp(s - m)
        den = jnp.sum(pr, axis=-1, keepdims=True)
        o_ref[:, ln] = (_dot(pr.astype(BF16), vals) / den).astype(o_ref.dtype)
        lse_all = lse_all + jnp.where(head_lane == h, m + jnp.log(den), 0.0)
    lse_ref[...] = lse_all


def _band_attention(qkv3, group, dil, geom, d):
    n = qkv3.shape[1]
    n1, t1, t2 = geom
    nl, n1l, l1, l2 = n // dil, n1 // dil, t1 // dil, t2 // dil
    tq = _pick(math.gcd(l1, l2), (128, 64))
    hb = tq // BAND
    nhalo = nl // BAND
    if dil == 1:
        qkv_v, gsel = qkv3, group
    else:
        qkv_v, gsel = qkv3[group].reshape(1, nl, dil * 3 * d), 0

    def cur(s):
        return pl.BlockSpec((None, tq, d), lambda i, c, s=s: (gsel, i, c * 3 + s))

    def before(s):
        return pl.BlockSpec((None, BAND, d), lambda i, c, s=s: (gsel, jnp.maximum(i * hb - 1, 0), c * 3 + s))

    def after(s):
        return pl.BlockSpec((None, BAND, d), lambda i, c, s=s: (gsel, jnp.minimum((i + 1) * hb, nhalo - 1), c * 3 + s))

    o, lse = pl.pallas_call(
        functools.partial(_band_kernel, tq=tq, dil=dil, nheads=d // ATT_HEAD, n1l=n1l, l1=l1, l2=l2),
        out_shape=(jax.ShapeDtypeStruct((nl, dil * d), BF16),
                   jax.ShapeDtypeStruct((nl, dil * ATT_HEAD), F32)),
        grid=(nl // tq, dil),
        in_specs=[cur(0), before(1), cur(1), after(1), before(2), cur(2), after(2)],
        out_specs=(pl.BlockSpec((tq, d), lambda i, c: (i, c)),
                   pl.BlockSpec((tq, ATT_HEAD), lambda i, c: (i, c))),
        compiler_params=_cparams(("parallel", "parallel")),
        name=f"band_attention_g{group}",
    )(qkv_v, qkv_v, qkv_v, qkv_v, qkv_v, qkv_v, qkv_v)
    return o.reshape(n, d), lse.reshape(n, ATT_HEAD)


def _combine_kernel(o0_ref, o1_ref, o2_ref, l0_ref, l1_ref, l2_ref, e_ref, out_ref):
    lses = [l0_ref[...], l1_ref[...], l2_ref[...]]
    outs = [o0_ref, o1_ref, o2_ref]
    m = jnp.maximum(jnp.maximum(lses[0], lses[1]), lses[2])
    ws = [jnp.exp(l - m) for l in lses]
    tot = ws[0] + ws[1] + ws[2]
    e = e_ref[...]
    acc = None
    for w, o_ref in zip(ws, outs):
        hi, lo = _split(w / tot)
        term = (_dot(hi, e) + _dot(lo, e)) * o_ref[...].astype(F32)
        acc = term if acc is None else acc + term
    out_ref[...] = acc.astype(out_ref.dtype)


def _combine(os_, lses, d):
    n = os_[0].shape[0]
    tm = _pick(n, (256, 128, 64, 32, 16, 8))
    head_of_lane = jnp.arange(d, dtype=jnp.int32) // ATT_HEAD
    expand = (jnp.arange(ATT_HEAD, dtype=jnp.int32)[:, None] == head_of_lane[None, :]).astype(BF16)
    tok = pl.BlockSpec((tm, d), lambda i: (i, 0))
    stat = pl.BlockSpec((tm, ATT_HEAD), lambda i: (i, 0))
    return pl.pallas_call(
        _combine_kernel,
        out_shape=jax.ShapeDtypeStruct((n, d), BF16),
        grid=(n // tm,),
        in_specs=[tok, tok, tok, stat, stat, stat, pl.BlockSpec((ATT_HEAD, d), lambda i: (0, 0))],
        out_specs=tok,
        compiler_params=_cparams(("parallel",)),
        name="attention_combine",
    )(*os_, *lses, expand)


def _rwkv_layer(x, gain, p, v_first, vres, geom):
    xs = _token_shift(x, gain, p["mu"], geom)
    rkv = _matmul_batched(xs, p["w_rkv"], 3, F32)
    if v_first is None:
        v_first = rkv
    logw, kd, a, v, g = _rwkv_mid(xs, rkv, v_first, p, vres)
    ys = []
    for z in range(2):
        rp, y0, gm, hm = _wkv_stage1(rkv, 0, logw, kd, v, rkv, 1, a, p["k_k"], z, reverse=(z == 1))
        ys.append(_wkv_stage2(rp, y0, gm, hm, geom, reverse=(z == 1)))
    o = _rwkv_post(ys[0], ys[1], rkv, kd, v, g, p["r_k"], p["lnx_w"], p["lnx_b"])
    return _matmul_residual(o, p["w_o"], x), v_first


def _attention_layer(x, gain, w_qkv, w_o, geom):
    d = x.shape[1]
    qkv = _matmul_norm_grouped(x, gain, w_qkv, N_GROUPS, BF16)
    outs, lses = [], []
    for gi, (window, dil) in enumerate(DIL_PATTERNS):
        assert (window // 2) // dil == BAND
        o, lse = _band_attention(qkv, gi, dil, geom, d)
        outs.append(o)
        lses.append(lse)
    return _matmul_residual(_combine(outs, lses, d), w_o, x)


def kernel(x_prompt, x_sample, ln1, ln2, ln_f, rw_mu, rw_w_rkv, rw_w0, rw_w1, rw_w2, rw_a0, rw_a1, rw_a2, rw_v0, rw_v1, rw_v2, rw_g1, rw_g2, rw_k_k, rw_k_a, rw_r_k, rw_lnx_w, rw_lnx_b, rw_w_o, at_w_qkv, at_w_o, ffn_w_gate, ffn_w_up, ffn_w_down):
    b1, t1, d = x_prompt.shape
    b2, t2, _ = x_sample.shape
    n1, n2 = b1 * t1, b2 * t2
    geom = (n1, t1, t2)
    depth = ln1.shape[0]
    assert d % PAIR == 0 and d % ATT_HEAD == 0
    max_dil = max(dil for _, dil in DIL_PATTERNS)
    assert t1 % (max_dil * BAND) == 0 and t2 % (max_dil * BAND) == 0

    x = jnp.concatenate([x_prompt.reshape(n1, d), x_sample.reshape(n2, d)], axis=0)
    bf = lambda w: w.astype(BF16)
    row = lambda w: w.reshape(1, d)

    v_first = None
    for i in range(depth):
        j = i // 2
        if i % 2 == 0:
            p = dict(mu=rw_mu[j], w_rkv=bf(rw_w_rkv[j]), w0=rw_w0[j], w1=bf(rw_w1[j]), w2=bf(rw_w2[j]),
                     a0=rw_a0[j], a1=bf(rw_a1[j]), a2=bf(rw_a2[j]), g1=bf(rw_g1[j]), g2=bf(rw_g2[j]),
                     k_k=row(rw_k_k[j]), k_a=row(rw_k_a[j]), r_k=rw_r_k[j].reshape(1, d),
                     lnx_w=row(rw_lnx_w[j]), lnx_b=row(rw_lnx_b[j]), w_o=bf(rw_w_o[j]))
            vres = None if j == 0 else (row(rw_v0[j - 1]), bf(rw_v1[j - 1]), bf(rw_v2[j - 1]))
            x, v_first = _rwkv_layer(x, row(ln1[i]), p, v_first, vres, geom)
        else:
            w_qkv = bf(at_w_qkv[j]).reshape(d, -1)
            x = _attention_layer(x, row(ln1[i]), w_qkv, bf(at_w_o[j]), geom)
        x = _ffn(x, row(ln2[i]), bf(ffn_w_gate[i]), bf(ffn_w_up[i]), bf(ffn_w_down[i]),
                 row(ln_f), final_norm=(i == depth - 1))
    return x[:n1].reshape(b1, t1, d), x[n1:].reshape(b2, t2, d)
```

```python
import functools
import math

import jax
import jax.numpy as jnp
from jax import lax
from jax.experimental import pallas as pl
from jax.experimental.pallas import tpu as pltpu

F32 = jnp.float32
BF16 = jnp.bfloat16

RW_HEAD = 64
PAIR = 2 * RW_HEAD
ATT_HEAD = 128
LANES = 128
DIL_PATTERNS = ((128, 1), (512, 4), (2048, 16))
BAND = 64
N_GROUPS = len(DIL_PATTERNS)
LNX_EPS = 64e-5
RMS_EPS = 1e-6
CHUNK = 64
NEG_BIG = -1e30
HEAD_BATCH = 8
WKV_UNROLL = 4
V7X_VMEM_LIMIT_BYTES = 56 * 1024 * 1024


def _pick(n, candidates):
    for c in candidates:
        if n % c == 0:
            return c
    raise ValueError(f"no tile in {candidates} divides {n}")


def _cparams(sem):
    return pltpu.CompilerParams(dimension_semantics=sem, vmem_limit_bytes=V7X_VMEM_LIMIT_BYTES)


def _dot(a, b):
    return jnp.dot(a, b, preferred_element_type=F32)


def _dot_nt(a, b):
    return lax.dot_general(a, b, (((1,), (1,)), ((), ())), preferred_element_type=F32)


def _dot_tn(a, b):
    return lax.dot_general(a, b, (((0,), (0,)), ((), ())), preferred_element_type=F32)


def _split(x):
    hi = x.astype(BF16)
    lo = (x - hi.astype(F32)).astype(BF16)
    return hi, lo


def _sigmoid(x):
    return 1.0 / (1.0 + jnp.exp(-x))


def _rms(x, g):
    return x * lax.rsqrt(jnp.mean(x * x, axis=-1, keepdims=True) + RMS_EPS) * g


def _seq_pos(idx, n1, t1, t2):
    first = idx < n1
    pos = jnp.where(first, lax.rem(idx, t1), lax.rem(idx - n1, t2))
    return pos, jnp.where(first, t1, t2)


def _mm_kernel(a_ref, w_ref, o_ref):
    o_ref[...] = _dot(a_ref[...], w_ref[...]).astype(o_ref.dtype)


def _matmul_batched(a, w, groups, out_dtype):
    _, m, k = a.shape
    n = w.shape[2]
    tm = _pick(m, (1024, 512, 256, 128))
    tn = _pick(n, (1024, 512, 256, 128))
    return pl.pallas_call(
        _mm_kernel,
        out_shape=jax.ShapeDtypeStruct((groups, m, n), out_dtype),
        grid=(groups, m // tm, n // tn),
        in_specs=[pl.BlockSpec((None, tm, k), lambda g, i, j: (g, i, 0)),
                  pl.BlockSpec((None, k, tn), lambda g, i, j: (g, 0, j))],
        out_specs=pl.BlockSpec((None, tm, tn), lambda g, i, j: (g, i, j)),
        compiler_params=_cparams(("parallel", "parallel", "arbitrary")),
        name="mm_batched",
    )(a, w)


def _mm_norm_kernel(x_ref, g_ref, w_ref, o_ref, h_ref, *scratch, dil):
    @pl.when(pl.program_id(1) == 0)
    def _():
        h_ref[...] = _rms(x_ref[...], g_ref[...]).astype(BF16)

    res = _dot(h_ref[...], w_ref[...])
    if dil == 1:
        o_ref[0] = res.astype(o_ref.dtype)
    else:
        (res_ref,) = scratch
        rows = res.shape[0] // dil
        for k in range(res.shape[1] // LANES):
            cols = slice(k * LANES, (k + 1) * LANES)
            res_ref[k] = res[:, cols]
            for c in range(dil):
                o_ref[c, :, cols] = res_ref[k, pl.ds(c, rows, stride=dil), :].astype(o_ref.dtype)


def _matmul_norm_classes(x, gain, w, group, ngroups, dil, out_dtype):
    m, k = x.shape
    ng = w.shape[1] // ngroups
    tm = _pick(m, (1024, 512, 256))
    tn = _pick(ng, (1024, 512, 256, 128))
    nj = ng // tn
    scratch = [pltpu.VMEM((tm, k), BF16)]
    if dil > 1:
        scratch.append(pltpu.VMEM((tn // LANES, tm, LANES), F32))
    return pl.pallas_call(
        functools.partial(_mm_norm_kernel, dil=dil),
        out_shape=jax.ShapeDtypeStruct((dil, m // dil, ng), out_dtype),
        grid=(m // tm, nj),
        in_specs=[pl.BlockSpec((tm, k), lambda i, j: (i, 0)),
                  pl.BlockSpec((1, k), lambda i, j: (0, 0)),
                  pl.BlockSpec((k, tn), lambda i, j: (0, group * nj + j))],
        out_specs=pl.BlockSpec((dil, tm // dil, tn), lambda i, j: (0, i, j)),
        scratch_shapes=scratch,
        compiler_params=_cparams(("parallel", "arbitrary")),
        name=f"mm_norm_g{group}",
    )(x, gain, w)


def _mm_res_kernel(a_ref, w_ref, r_ref, o_ref):
    o_ref[...] = r_ref[...] + _dot(a_ref[...], w_ref[...])


def _matmul_residual(a, w, res):
    m, k = a.shape
    n = w.shape[1]
    tm = _pick(m, (1024, 512, 256, 128))
    tn = _pick(n, (1024, 512, 256, 128))
    return pl.pallas_call(
        _mm_res_kernel,
        out_shape=jax.ShapeDtypeStruct((m, n), F32),
        grid=(m // tm, n // tn),
        in_specs=[pl.BlockSpec((tm, k), lambda i, j: (i, 0)),
                  pl.BlockSpec((k, tn), lambda i, j: (0, j)),
                  pl.BlockSpec((tm, tn), lambda i, j: (i, j))],
        out_specs=pl.BlockSpec((tm, tn), lambda i, j: (i, j)),
        input_output_aliases={2: 0},
        compiler_params=_cparams(("parallel", "arbitrary")),
        name="mm_residual",
    )(a, w, res)


def _ffn_kernel(x_ref, g_ref, wg_ref, wu_ref, wd_ref, gf_ref, o_ref, h_ref, acc_ref, *, final_norm):
    j = pl.program_id(1)

    @pl.when(j == 0)
    def _():
        x = x_ref[...]
        h_ref[...] = _rms(x, g_ref[...]).astype(BF16)
        acc_ref[...] = x

    h = h_ref[...]
    gate = _dot(h, wg_ref[...])
    up = _dot(h, wu_ref[...])
    act = (gate * _sigmoid(gate) * up).astype(BF16)
    acc_ref[...] += _dot(act, wd_ref[...])

    @pl.when(j == pl.num_programs(1) - 1)
    def _():
        y = acc_ref[...]
        if final_norm:
            y = _rms(y, gf_ref[...])
        o_ref[...] = y


def _ffn(x, gain, wg, wu, wd, final_gain, final_norm):
    m, d = x.shape
    ff = wg.shape[1]
    tm = _pick(m, (512, 256, 128))
    tf = _pick(ff, (512, 256, 128))
    return pl.pallas_call(
        functools.partial(_ffn_kernel, final_norm=final_norm),
        out_shape=jax.ShapeDtypeStruct((m, d), F32),
        grid=(m // tm, ff // tf),
        in_specs=[pl.BlockSpec((tm, d), lambda i, j: (i, 0)),
                  pl.BlockSpec((1, d), lambda i, j: (0, 0)),
                  pl.BlockSpec((d, tf), lambda i, j: (0, j)),
                  pl.BlockSpec((d, tf), lambda i, j: (0, j)),
                  pl.BlockSpec((tf, d), lambda i, j: (j, 0)),
                  pl.BlockSpec((1, d), lambda i, j: (0, 0))],
        out_specs=pl.BlockSpec((tm, d), lambda i, j: (i, 0)),
        scratch_shapes=[pltpu.VMEM((tm, d), BF16), pltpu.VMEM((tm, d), F32)],
        input_output_aliases={0: 0},
        compiler_params=_cparams(("parallel", "arbitrary")),
        name="ffn",
    )(x, gain, wg, wu, wd, final_gain)


def _shift_kernel(x_ref, xp_ref, xn_ref, g_ref, mu_ref, o_ref, *, tt, n1, t1, t2):
    pos0, seq_len = _seq_pos(pl.program_id(0) * tt, n1, t1, t2)
    g = g_ref[...]
    h = _rms(x_ref[...], g)
    h_before = jnp.where(pos0 == 0, 0.0, _rms(xp_ref[7:8, :], g))
    h_after = jnp.where(pos0 + tt == seq_len, 0.0, _rms(xn_ref[0:1, :], g))
    row = lax.broadcasted_iota(jnp.int32, (tt, 1), 0)
    h_prev = jnp.where(row == 0, h_before, pltpu.roll(h, 1, 0))
    h_next = jnp.where(row == tt - 1, h_after, pltpu.roll(h, tt - 1, 0))
    xx = 0.5 * (h_prev + h_next) - h
    for c in range(6):
        o_ref[c] = (h + xx * mu_ref[c:c + 1, :]).astype(BF16)


def _token_shift(x, gain, mu, geom):
    n, d = x.shape
    n1, t1, t2 = geom
    tt = _pick(math.gcd(t1, t2), (256, 128, 64, 32, 16, 8))
    nb8 = n // 8
    return pl.pallas_call(
        functools.partial(_shift_kernel, tt=tt, n1=n1, t1=t1, t2=t2),
        out_shape=jax.ShapeDtypeStruct((6, n, d), BF16),
        grid=(n // tt,),
        in_specs=[pl.BlockSpec((tt, d), lambda i: (i, 0)),
                  pl.BlockSpec((8, d), lambda i: (jnp.maximum(i * (tt // 8) - 1, 0), 0)),
                  pl.BlockSpec((8, d), lambda i: (jnp.minimum((i + 1) * (tt // 8), nb8 - 1), 0)),
                  pl.BlockSpec((1, d), lambda i: (0, 0)),
                  pl.BlockSpec((6, d), lambda i: (0, 0))],
        out_specs=pl.BlockSpec((6, tt, d), lambda i: (0, i, 0)),
        compiler_params=_cparams(("parallel",)),
        name="token_shift",
    )(x, x, x, gain, mu)


def _mid_kernel(*refs, has_vres):
    if has_vres:
        (xv_ref, xw_ref, xa_ref, xg_ref, k_ref, v_ref, vf_ref,
         w0_ref, w1_ref, w2_ref, a0_ref, a1_ref, a2_ref, g1_ref, g2_ref, ka_ref,
         v0_ref, v1_ref, v2_ref,
         logw_ref, kd_ref, a_ref, vo_ref, g_ref) = refs
    else:
        (xv_ref, xw_ref, xa_ref, xg_ref, k_ref, v_ref,
         w0_ref, w1_ref, w2_ref, a0_ref, a1_ref, a2_ref, g1_ref, g2_ref, ka_ref,
         logw_ref, kd_ref, a_ref, vo_ref, g_ref) = refs
    xw = xw_ref[...]
    xa = xa_ref[...]
    k = k_ref[...]
    k_a = ka_ref[...]
    for z in range(2):
        lora = _dot(jnp.tanh(_dot(xw, w1_ref[z])).astype(BF16), w2_ref[z])
        y = -(w0_ref[z:z + 1, :] + lora)
        softplus = jnp.maximum(y, 0.0) + jnp.log(1.0 + jnp.exp(-jnp.abs(y)))
        logw_ref[z] = -jnp.exp(-softplus - 0.5)
        aa = _sigmoid(a0_ref[z:z + 1, :] + _dot(_dot(xa, a1_ref[z]).astype(BF16), a2_ref[z]))
        a_ref[z] = aa.astype(a_ref.dtype)
        kd_ref[z] = (k * (1.0 + (aa - 1.0) * k_a)).astype(kd_ref.dtype)
    v = v_ref[...]
    if has_vres:
        gate = _sigmoid(v0_ref[...] + _dot(_dot(xv_ref[...], v1_ref[...]).astype(BF16), v2_ref[...]))
        v = v + (vf_ref[...] - v) * gate
    vo_ref[...] = v.astype(vo_ref.dtype)
    g_ref[...] = _dot(_sigmoid(_dot(xg_ref[...], g1_ref[...])).astype(BF16), g2_ref[...]).astype(g_ref.dtype)


def _rwkv_mid(xs, rkv, v_first, p, vres):
    _, n, d = xs.shape
    tm = _pick(n, (128, 64, 32, 16, 8))
    has_vres = vres is not None

    def slab(c):
        return pl.BlockSpec((None, tm, d), lambda i, c=c: (c, i, 0))

    def whole(a):
        nd = a.ndim
        return pl.BlockSpec(a.shape, lambda i, nd=nd: (0,) * nd)

    tok = pl.BlockSpec((tm, d), lambda i: (i, 0))
    tok2 = pl.BlockSpec((2, tm, d), lambda i: (0, i, 0))
    weights = [p["w0"], p["w1"], p["w2"], p["a0"], p["a1"], p["a2"], p["g1"], p["g2"], p["k_a"]]
    args = [xs, xs, xs, xs, rkv, rkv]
    specs = [slab(2), slab(3), slab(4), slab(5), slab(1), slab(2)]
    if has_vres:
        args.append(v_first)
        specs.append(slab(2))
        weights += list(vres)
    args += weights
    specs += [whole(a) for a in weights]
    return pl.pallas_call(
        functools.partial(_mid_kernel, has_vres=has_vres),
        out_shape=(jax.ShapeDtypeStruct((2, n, d), F32),
                   jax.ShapeDtypeStruct((2, n, d), BF16),
                   jax.ShapeDtypeStruct((2, n, d), BF16),
                   jax.ShapeDtypeStruct((n, d), BF16),
                   jax.ShapeDtypeStruct((n, d), BF16)),
        grid=(n // tm,),
        in_specs=specs,
        out_specs=(tok2, tok2, tok2, tok, tok),
        compiler_params=_cparams(("parallel",)),
        name="rwkv_mid",
    )(*args)


def _pair_rows(x, head0):
    zero = jnp.zeros_like(x)
    return jnp.concatenate([jnp.where(head0, x, zero), jnp.where(head0, zero, x)], axis=0)


def _wkv1_kernel(r_ref, lw_ref, kd_ref, v_ref, k_ref, a_ref, kk_ref,
                 rp_ref, y0_ref, g_ref, h_ref, *, nchunk, npair, unroll, reverse):
    lane = lax.broadcasted_iota(jnp.int32, (CHUNK, PAIR), 1)
    head0 = lane < RW_HEAD
    col = jnp.where(head0, lane, lane - RW_HEAD)
    row = lax.broadcasted_iota(jnp.int32, (CHUNK, PAIR), 0)
    strict = (row < col) if reverse else (row > col)
    incl = (row <= col) if reverse else (row >= col)
    eye = row == col
    r2 = lax.broadcasted_iota(jnp.int32, (PAIR, PAIR), 0)
    q2 = lax.broadcasted_iota(jnp.int32, (PAIR, PAIR), 1)
    ones_bd = ((r2 < RW_HEAD) == (q2 < RW_HEAD)).astype(BF16)
    t1 = lax.broadcasted_iota(jnp.int32, (CHUNK, CHUNK), 0)
    s1 = lax.broadcasted_iota(jnp.int32, (CHUNK, CHUNK), 1)
    tri = ((t1 <= s1) if reverse else (t1 >= s1)).astype(BF16)
    tri2 = jnp.concatenate([tri, tri], axis=1)
    last = 0 if reverse else CHUNK - 1
    bd = functools.partial(_pair_rows, head0=head0)

    def prepare(rows, ln, lw, cum):
        kk = k_ref[rows, ln] * kk_ref[:, ln]
        ss = _dot((kk * kk).astype(BF16), ones_bd)
        kk = kk / jnp.maximum(jnp.sqrt(ss), 1e-12)
        b = kk * a_ref[rows, ln].astype(F32)
        kd = kd_ref[rows, ln].astype(F32)
        tot = cum[last:last + 1, :]
        e_neg = jnp.exp(-cum)
        e_tail = jnp.exp(tot - cum)
        w = dict(tot=tot)
        w["at"] = (-kk * jnp.exp(cum - lw)).astype(BF16)
        w["rt"] = r_ref[rows, ln] * jnp.exp(cum)
        w["bk_in"] = jnp.concatenate([bd((b * e_neg).astype(BF16)), bd((kd * e_neg).astype(BF16))], axis=0)
        w["bk_out"] = jnp.concatenate([(b * e_tail).astype(BF16), (kd * e_tail).astype(BF16)], axis=0)
        w["v"] = v_ref[rows, ln]
        return w

    def chunk_body(ci, carry):
        items = []
        for u in range(unroll):
            rows = pl.ds(pl.multiple_of((ci * unroll + u) * CHUNK, CHUNK), CHUNK)
            lw_all = lw_ref[rows, :]
            hi, lo = _split(lw_all)
            cum_all = _dot(tri2, jnp.concatenate([hi, lo], axis=0))
            for p in range(npair):
                ln = slice(p * PAIR, (p + 1) * PAIR)
                w = prepare(rows, ln, lw_all[:, ln], cum_all[:, ln])
                w["rows"], w["ln"] = rows, ln
                items.append(w)
        for w in items:
            m1 = _dot_nt(jnp.concatenate([w["at"], w["rt"].astype(BF16)], axis=0), w["bk_in"])
            a_ab = jnp.where(strict, m1[:CHUNK, :PAIR], 0.0)
            w["a_kv"] = jnp.concatenate([jnp.where(strict, m1[:CHUNK, PAIR:], 0.0),
                                         jnp.where(incl, m1[CHUNK:, PAIR:], 0.0)], axis=0).astype(BF16)
            w["a_rb"] = jnp.where(incl, m1[CHUNK:, :PAIR], 0.0).astype(BF16)
            w["pw"] = a_ab.astype(BF16)
            w["inv"] = jnp.where(eye, 1.0, a_ab)
        for w in items:
            w["pw"] = _dot(w["pw"], bd(w["pw"])).astype(BF16)
        for f in range(1, int(math.log2(CHUNK)) - 1):
            for w in items:
                both = _dot(jnp.concatenate([w["pw"], w["inv"].astype(BF16)], axis=0), bd(w["pw"]))
                w["pw"] = both[:CHUNK].astype(BF16)
                w["inv"] = w["inv"] + both[CHUNK:]
        for w in items:
            w["inv"] = w["inv"] + _dot(w["inv"].astype(BF16), bd(w["pw"]))
        for w in items:
            w["av"] = _dot(w["a_kv"], bd(w["v"]))
        for w in items:
            rhs = jnp.concatenate([bd(w["at"]), bd(w["av"][:CHUNK].astype(BF16))], axis=1)
            w["wu"] = _dot(w["inv"].astype(BF16), rhs).astype(BF16)
        for w in items:
            rows, ln = w["rows"], w["ln"]
            wu = w["wu"]
            rw = _dot(w["a_rb"], jnp.concatenate([bd(wu[:, :PAIR]), bd(wu[:, PAIR:])], axis=1))
            rp_ref[rows, ln] = (w["rt"] + rw[:, :PAIR]).astype(rp_ref.dtype)
            y0_ref[rows, ln] = rw[:, PAIR:] + w["av"][CHUNK:]
            gfull = _dot_tn(w["bk_out"][:CHUNK], wu[:, :PAIR])
            gdiag = jnp.where(eye, jnp.exp(w["tot"]), 0.0)
            g_ref[rows, ln] = (jnp.where(head0, gfull[:CHUNK], gfull[CHUNK:]) + gdiag).astype(g_ref.dtype)
            hfull = _dot_tn(w["bk_out"], jnp.concatenate([wu[:, PAIR:], w["v"]], axis=0))
            h_ref[rows, ln] = jnp.where(head0, hfull[:CHUNK], hfull[CHUNK:])
        return carry

    lax.fori_loop(0, nchunk // unroll, chunk_body, 0)


def _wkv_stage1(r3, r_slab, logw, kd, v, k3, k_slab, a, kk_scale, z, reverse):
    n, d = v.shape
    npair = _pick(d // PAIR, (4, 2, 1))
    lanes = npair * PAIR
    tb = _pick(n, (512, 256))

    def slab(c):
        return pl.BlockSpec((None, tb, lanes), lambda i, j, c=c: (c, i, j))

    tok = pl.BlockSpec((tb, lanes), lambda i, j: (i, j))
    return pl.pallas_call(
        functools.partial(_wkv1_kernel, nchunk=tb // CHUNK, npair=npair, unroll=WKV_UNROLL, reverse=reverse),
        out_shape=(jax.ShapeDtypeStruct((n, d), BF16),
                   jax.ShapeDtypeStruct((n, d), F32),
                   jax.ShapeDtypeStruct((n, d), BF16),
                   jax.ShapeDtypeStruct((n, d), F32)),
        grid=(n // tb, d // lanes),
        in_specs=[slab(r_slab), slab(z), slab(z), tok, slab(k_slab), slab(z),
                  pl.BlockSpec((1, lanes), lambda i, j: (0, j))],
        out_specs=(tok, tok, tok, tok),
        compiler_params=_cparams(("parallel", "parallel")),
        name="wkv_stage1_rev" if reverse else "wkv_stage1_fwd",
    )(r3, logw, kd, v, k3, a, kk_scale)


def _wkv2_kernel(rpf_ref, y0f_ref, gf_ref, hf_ref, rpb_ref, y0b_ref, gb_ref, hb_ref,
                 yf_ref, yb_ref, stf_ref, stb_ref, *, npairs, nchunks, n1, t1, t2):
    i = pl.program_id(0)
    pos_f, _ = _seq_pos(i * CHUNK, n1, t1, t2)
    pos_b, len_b = _seq_pos((nchunks - 1 - i) * CHUNK, n1, t1, t2)

    @pl.when(pos_f == 0)
    def _():
        stf_ref[...] = jnp.zeros_like(stf_ref)

    @pl.when(pos_b + CHUNK == len_b)
    def _():
        stb_ref[...] = jnp.zeros_like(stb_ref)

    lane = lax.broadcasted_iota(jnp.int32, (CHUNK, PAIR), 1)
    head0 = lane < RW_HEAD
    bd = functools.partial(_pair_rows, head0=head0)
    dirs = ((rpf_ref, y0f_ref, gf_ref, hf_ref, yf_ref, stf_ref), (rpb_ref, y0b_ref, gb_ref, hb_ref, yb_ref, stb_ref))
    items = [(refs, slice(p * PAIR, (p + 1) * PAIR)) for p in range(npairs) for refs in dirs]
    results = []
    for (rp_ref, _, g_ref, _, _, st_ref), ln in items:
        hi, lo = _split(st_ref[:, ln])
        lhs = jnp.concatenate([rp_ref[:, ln], g_ref[:, ln]], axis=0)
        results.append(_dot(jnp.concatenate([lhs, lhs], axis=1), jnp.concatenate([bd(hi), bd(lo)], axis=0)))
    for ((_, y0_ref, _, h_ref, y_ref, st_ref), ln), res in zip(items, results):
        y_ref[:, ln] = y0_ref[:, ln] + res[:CHUNK]
        st_ref[:, ln] = res[CHUNK:] + h_ref[:, ln]


def _wkv_stage2(fwd, bwd, geom):
    n, d = fwd[0].shape
    n1, t1, t2 = geom
    nchunks = n // CHUNK
    tok_f = pl.BlockSpec((CHUNK, d), lambda i: (i, 0))
    tok_b = pl.BlockSpec((CHUNK, d), lambda i: (nchunks - 1 - i, 0))
    return pl.pallas_call(
        functools.partial(_wkv2_kernel, npairs=d // PAIR, nchunks=nchunks, n1=n1, t1=t1, t2=t2),
        out_shape=(jax.ShapeDtypeStruct((n, d), F32), jax.ShapeDtypeStruct((n, d), F32)),
        grid=(nchunks,),
        in_specs=[tok_f] * 4 + [tok_b] * 4,
        out_specs=(tok_f, tok_b),
        scratch_shapes=[pltpu.VMEM((CHUNK, d), F32), pltpu.VMEM((CHUNK, d), F32)],
        compiler_params=_cparams(("arbitrary",)),
        name="wkv_stage2",
    )(*fwd, *bwd)


def _post_kernel(yf_ref, yb_ref, r_ref, kd_ref, v_ref, g_ref, rk_ref, lw_ref, lb_ref, o_ref, *, ngroups):
    r2 = lax.broadcasted_iota(jnp.int32, (PAIR, PAIR), 0)
    q2 = lax.broadcasted_iota(jnp.int32, (PAIR, PAIR), 1)
    same = (r2 < RW_HEAD) == (q2 < RW_HEAD)
    ones_bd = same.astype(BF16)
    mean_bd = (same.astype(F32) * (1.0 / RW_HEAD)).astype(BF16)
    for p in range(ngroups):
        ln = slice(p * PAIR, (p + 1) * PAIR)
        y = yf_ref[:, ln] + yb_ref[:, ln]
        hi, lo = _split(y)
        mean = _dot(hi, mean_bd) + _dot(lo, mean_bd)
        dev = y - mean
        hi, lo = _split(dev * dev)
        var = _dot(hi, mean_bd) + _dot(lo, mean_bd)
        yn = dev * lax.rsqrt(var + LNX_EPS) * lw_ref[:, ln] + lb_ref[:, ln]
        kd = kd_ref[0, :, ln].astype(F32) + kd_ref[1, :, ln].astype(F32)
        hi, lo = _split(r_ref[:, ln] * kd * rk_ref[:, ln])
        bonus = (_dot(hi, ones_bd) + _dot(lo, ones_bd)) * v_ref[:, ln].astype(F32)
        o_ref[:, ln] = ((yn + bonus) * g_ref[:, ln].astype(F32)).astype(o_ref.dtype)


def _rwkv_post(yf, yb, rkv, kd, v, g, r_k, lnx_w, lnx_b):
    n, d = yf.shape
    tm = _pick(n, (256, 128, 64, 32, 16, 8))
    tok = pl.BlockSpec((tm, d), lambda i: (i, 0))
    row = pl.BlockSpec((1, d), lambda i: (0, 0))
    return pl.pallas_call(
        functools.partial(_post_kernel, ngroups=d // PAIR),
        out_shape=jax.ShapeDtypeStruct((n, d), BF16),
        grid=(n // tm,),
        in_specs=[tok, tok, pl.BlockSpec((None, tm, d), lambda i: (0, i, 0)),
                  pl.BlockSpec((2, tm, d), lambda i: (0, i, 0)), tok, tok, row, row, row],
        out_specs=tok,
        compiler_params=_cparams(("parallel",)),
        name="rwkv_post",
    )(yf, yb, rkv, kd, v, g, r_k, lnx_w, lnx_b)


def _band_kernel(q_ref, kp_ref, kc_ref, kn_ref, vp_ref, vc_ref, vn_ref, o_ref, lse_ref,
                 *, tq, dil, nheads, n1l, l1, l2):
    pos0, seq_len = _seq_pos(pl.program_id(0) * tq, n1l, l1, l2)
    prev_ok = pos0 > 0
    next_ok = pos0 + tq < seq_len
    nk = tq + 2 * BAND
    row = lax.broadcasted_iota(jnp.int32, (tq, nk), 0)
    col = lax.broadcasted_iota(jnp.int32, (tq, nk), 1)
    dist = jnp.abs(col - BAND - row)
    valid = (dist <= BAND) & ((col >= BAND) | prev_ok) & ((col < tq + BAND) | next_ok)
    alibi = dist.astype(F32) * float(dil)
    head_lane = lax.broadcasted_iota(jnp.int32, (1, ATT_HEAD), 1)
    lse_all = jnp.zeros((tq, ATT_HEAD), F32)
    scale = ATT_HEAD ** -0.5
    lanes = [slice(h * ATT_HEAD, (h + 1) * ATT_HEAD) for h in range(nheads)]
    for h0 in range(0, nheads, HEAD_BATCH):
        batch = range(h0, min(h0 + HEAD_BATCH, nheads))
        scores, probs, dens = {}, {}, {}
        for h in batch:
            keys = jnp.concatenate([kp_ref[:, lanes[h]], kc_ref[:, lanes[h]], kn_ref[:, lanes[h]]], axis=0)
            slope = 2.0 ** (-8.0 * (h + 1) / nheads)
            s = _dot_nt(q_ref[:, lanes[h]], keys) * scale - slope * alibi
            scores[h] = jnp.where(valid, s, NEG_BIG)
        for h in batch:
            m = jnp.max(scores[h], axis=-1, keepdims=True)
            probs[h] = jnp.exp(scores[h] - m)
            dens[h] = jnp.sum(probs[h], axis=-1, keepdims=True)
            lse_all = lse_all + jnp.where(head_lane == h, m + jnp.log(dens[h]), 0.0)
        for h in batch:
            vals = jnp.concatenate([vp_ref[:, lanes[h]], vc_ref[:, lanes[h]], vn_ref[:, lanes[h]]], axis=0)
            o_ref[:, lanes[h]] = (_dot(probs[h].astype(BF16), vals) / dens[h]).astype(o_ref.dtype)
    lse_ref[...] = lse_all


def _band_attention(qkv, group, geom, d):
    dil, nl, _ = qkv.shape
    n1, t1, t2 = geom
    n1l, l1, l2 = n1 // dil, t1 // dil, t2 // dil
    tq = _pick(math.gcd(l1, l2), (128, 64))
    hb = tq // BAND
    nhalo = nl // BAND

    def cur(s):
        return pl.BlockSpec((None, tq, d), lambda i, c, s=s: (c, i, s))

    def before(s):
        return pl.BlockSpec((None, BAND, d), lambda i, c, s=s: (c, jnp.maximum(i * hb - 1, 0), s))

    def after(s):
        return pl.BlockSpec((None, BAND, d), lambda i, c, s=s: (c, jnp.minimum((i + 1) * hb, nhalo - 1), s))

    return pl.pallas_call(
        functools.partial(_band_kernel, tq=tq, dil=dil, nheads=d // ATT_HEAD, n1l=n1l, l1=l1, l2=l2),
        out_shape=(jax.ShapeDtypeStruct((dil, nl, d), BF16),
                   jax.ShapeDtypeStruct((dil, nl, ATT_HEAD), F32)),
        grid=(nl // tq, dil),
        in_specs=[cur(0), before(1), cur(1), after(1), before(2), cur(2), after(2)],
        out_specs=(pl.BlockSpec((None, tq, d), lambda i, c: (c, i, 0)),
                   pl.BlockSpec((None, tq, ATT_HEAD), lambda i, c: (c, i, 0))),
        compiler_params=_cparams(("parallel", "parallel")),
        name=f"band_attention_g{group}",
    )(qkv, qkv, qkv, qkv, qkv, qkv, qkv)


def _combine_kernel(o0_ref, o1_ref, o2_ref, l0_ref, l1_ref, l2_ref, e_ref, out_ref, lse_scr, o_scr, *, dils):
    tm = out_ref.shape[0]

    def token_order(ref, scr, dil):
        if dil == 1:
            return ref[0].astype(F32)
        rows = tm // dil
        ngrp = ref.shape[2] // LANES
        for k in range(ngrp):
            for c in range(dil):
                scr[k, pl.ds(c, rows, stride=dil), :] = ref[c, :, k * LANES:(k + 1) * LANES].astype(F32)
        return jnp.concatenate([scr[k] for k in range(ngrp)], axis=1)

    lses = [token_order(l_ref, lse_scr.at[pl.ds(g, 1)], dil)
            for g, (l_ref, dil) in enumerate(zip((l0_ref, l1_ref, l2_ref), dils))]
    m = jnp.maximum(jnp.maximum(lses[0], lses[1]), lses[2])
    ws = [jnp.exp(l - m) for l in lses]
    tot = ws[0] + ws[1] + ws[2]
    e = e_ref[...]
    acc = None
    for w, o_ref, dil in zip(ws, (o0_ref, o1_ref, o2_ref), dils):
        hi, lo = _split(w / tot)
        term = (_dot(hi, e) + _dot(lo, e)) * token_order(o_ref, o_scr, dil)
        acc = term if acc is None else acc + term
    out_ref[...] = acc.astype(out_ref.dtype)


def _combine(os_, lses, d):
    dils = tuple(o.shape[0] for o in os_)
    n = os_[0].shape[0] * os_[0].shape[1]
    tm = _pick(n, (256,))
    head_of_lane = jnp.arange(d, dtype=jnp.int32) // ATT_HEAD
    expand = (jnp.arange(ATT_HEAD, dtype=jnp.int32)[:, None] == head_of_lane[None, :]).astype(BF16)

    def classes(dil, width):
        return pl.BlockSpec((dil, tm // dil, width), lambda i: (0, i, 0))

    return pl.pallas_call(
        functools.partial(_combine_kernel, dils=dils),
        out_shape=jax.ShapeDtypeStruct((n, d), BF16),
        grid=(n // tm,),
        in_specs=[classes(dil, d) for dil in dils] + [classes(dil, ATT_HEAD) for dil in dils]
                 + [pl.BlockSpec((ATT_HEAD, d), lambda i: (0, 0))],
        out_specs=pl.BlockSpec((tm, d), lambda i: (i, 0)),
        scratch_shapes=[pltpu.VMEM((len(dils), tm, LANES), F32), pltpu.VMEM((d // LANES, tm, LANES), F32)],
        compiler_params=_cparams(("parallel",)),
        name="attention_combine",
    )(*os_, *lses, expand)


def _rwkv_layer(x, gain, p, v_first, vres, geom):
    xs = _token_shift(x, gain, p["mu"], geom)
    rkv = _matmul_batched(xs, p["w_rkv"], 3, F32)
    if v_first is None:
        v_first = rkv
    logw, kd, a, v, g = _rwkv_mid(xs, rkv, v_first, p, vres)
    maps = [_wkv_stage1(rkv, 0, logw, kd, v, rkv, 1, a, p["k_k"], z, reverse=(z == 1)) for z in range(2)]
    yf, yb = _wkv_stage2(maps[0], maps[1], geom)
    o = _rwkv_post(yf, yb, rkv, kd, v, g, p["r_k"], p["lnx_w"], p["lnx_b"])
    return _matmul_residual(o, p["w_o"], x), v_first


def _attention_layer(x, gain, w_qkv, w_o, geom):
    d = x.shape[1]
    outs, lses = [], []
    for gi, (window, dil) in enumerate(DIL_PATTERNS):
        assert (window // 2) // dil == BAND
        qkv = _matmul_norm_classes(x, gain, w_qkv, gi, N_GROUPS, dil, BF16)
        o, lse = _band_attention(qkv, gi, geom, d)
        outs.append(o)
        lses.append(lse)
    return _matmul_residual(_combine(outs, lses, d), w_o, x)


def kernel(x_prompt, x_sample, ln1, ln2, ln_f, rw_mu, rw_w_rkv, rw_w0, rw_w1, rw_w2, rw_a0, rw_a1, rw_a2, rw_v0, rw_v1, rw_v2, rw_g1, rw_g2, rw_k_k, rw_k_a, rw_r_k, rw_lnx_w, rw_lnx_b, rw_w_o, at_w_qkv, at_w_o, ffn_w_gate, ffn_w_up, ffn_w_down):
    b1, t1, d = x_prompt.shape
    b2, t2, _ = x_sample.shape
    n1, n2 = b1 * t1, b2 * t2
    geom = (n1, t1, t2)
    depth = ln1.shape[0]
    assert d % PAIR == 0 and d % ATT_HEAD == 0
    max_dil = max(dil for _, dil in DIL_PATTERNS)
    assert t1 % (max_dil * BAND) == 0 and t2 % (max_dil * BAND) == 0

    x = jnp.concatenate([x_prompt.reshape(n1, d), x_sample.reshape(n2, d)], axis=0)
    bf = lambda w: w.astype(BF16)
    row = lambda w: w.reshape(1, d)

    v_first = None
    for i in range(depth):
        j = i // 2
        if i % 2 == 0:
            p = dict(mu=rw_mu[j], w_rkv=bf(rw_w_rkv[j]), w0=rw_w0[j], w1=bf(rw_w1[j]), w2=bf(rw_w2[j]),
                     a0=rw_a0[j], a1=bf(rw_a1[j]), a2=bf(rw_a2[j]), g1=bf(rw_g1[j]), g2=bf(rw_g2[j]),
                     k_k=row(rw_k_k[j]), k_a=row(rw_k_a[j]), r_k=rw_r_k[j].reshape(1, d),
                     lnx_w=row(rw_lnx_w[j]), lnx_b=row(rw_lnx_b[j]), w_o=bf(rw_w_o[j]))
            vres = None if j == 0 else (row(rw_v0[j - 1]), bf(rw_v1[j - 1]), bf(rw_v2[j - 1]))
            x, v_first = _rwkv_layer(x, row(ln1[i]), p, v_first, vres, geom)
        else:
            w_qkv = bf(at_w_qkv[j]).reshape(d, -1)
            x = _attention_layer(x, row(ln1[i]), w_qkv, bf(at_w_o[j]), geom)
        x = _ffn(x, row(ln2[i]), bf(ffn_w_gate[i]), bf(ffn_w_up[i]), bf(ffn_w_down[i]),
                 row(ln_f), final_norm=(i == depth - 1))
    return x[:n1].reshape(b1, t1, d), x[n1:].reshape(b2, t2, d)
```

```python
import functools
import math

import jax
import jax.numpy as jnp
from jax import lax
from jax.experimental import pallas as pl
from jax.experimental.pallas import tpu as pltpu

F32 = jnp.float32
BF16 = jnp.bfloat16

RW_HEAD = 64
PAIR = 2 * RW_HEAD
ATT_HEAD = 128
LANES = 128
ROW_STRIDE = 4
DIL_PATTERNS = ((128, 1), (512, 4), (2048, 16))
BAND = 64
N_GROUPS = len(DIL_PATTERNS)
LNX_EPS = 64e-5
RMS_EPS = 1e-6
CHUNK = 64
NEG_BIG = -1e30
LOG2E = math.log2(math.e)
LN2 = math.log(2.0)
RKV_DTYPE = BF16
WKV_MAP_DTYPE = BF16
WKV_OUT_DTYPE = BF16
HEAD_BATCH = 8
WKV_UNROLL = 4
V7X_VMEM_LIMIT_BYTES = 56 * 1024 * 1024


def _pick(n, candidates):
    for c in candidates:
        if n % c == 0:
            return c
    raise ValueError(f"no tile in {candidates} divides {n}")


def _cparams(sem):
    return pltpu.CompilerParams(dimension_semantics=sem, vmem_limit_bytes=V7X_VMEM_LIMIT_BYTES)


def _dot(a, b):
    return jnp.dot(a, b, preferred_element_type=F32)


def _dot_nt(a, b):
    return lax.dot_general(a, b, (((1,), (1,)), ((), ())), preferred_element_type=F32)


def _dot_tn(a, b):
    return lax.dot_general(a, b, (((0,), (0,)), ((), ())), preferred_element_type=F32)


def _split(x):
    hi = x.astype(BF16)
    lo = (x - hi.astype(F32)).astype(BF16)
    return hi, lo


def _sigmoid(x):
    return 1.0 / (1.0 + jnp.exp(-x))


def _rms(x, g):
    return x * lax.rsqrt(jnp.mean(x * x, axis=-1, keepdims=True) + RMS_EPS) * g


def _seq_pos(idx, n1, t1, t2):
    first = idx < n1
    pos = jnp.where(first, lax.rem(idx, t1), lax.rem(idx - n1, t2))
    return pos, jnp.where(first, t1, t2)


def _mm_kernel(a_ref, w_ref, o_ref):
    o_ref[...] = _dot(a_ref[...], w_ref[...]).astype(o_ref.dtype)


def _matmul_batched(a, w, groups, out_dtype):
    _, m, k = a.shape
    n = w.shape[2]
    tm = _pick(m, (1024, 512, 256, 128))
    tn = _pick(n, (1024, 512, 256, 128))
    return pl.pallas_call(
        _mm_kernel,
        out_shape=jax.ShapeDtypeStruct((groups, m, n), out_dtype),
        grid=(groups, m // tm, n // tn),
        in_specs=[pl.BlockSpec((None, tm, k), lambda g, i, j: (g, i, 0)),
                  pl.BlockSpec((None, k, tn), lambda g, i, j: (g, 0, j))],
        out_specs=pl.BlockSpec((None, tm, tn), lambda g, i, j: (g, i, j)),
        compiler_params=_cparams(("parallel", "parallel", "arbitrary")),
        name="mm_batched",
    )(a, w)


def _class_segment(c, dil):
    return c if dil <= ROW_STRIDE else ROW_STRIDE * (c % ROW_STRIDE) + c // ROW_STRIDE


def _mm_norm_kernel(x_ref, g_ref, w_ref, o_ref, h_ref, *scratch, dil):
    tm, k = x_ref.shape
    rows = tm // dil

    @pl.when(pl.program_id(1) == 0)
    def _():
        x = x_ref[...]
        inv = lax.rsqrt(jnp.mean(x * x, axis=-1, keepdims=True) + RMS_EPS)
        for kk in range(k // LANES):
            cols = slice(kk * LANES, (kk + 1) * LANES)
            hk = x_ref[:, cols] * inv * g_ref[:, cols]
            if dil == 1:
                h_ref[:, cols] = hk.astype(BF16)
                continue
            a_ref, b_ref = scratch
            a_ref[...] = hk
            quarter = tm // ROW_STRIDE
            if dil == ROW_STRIDE:
                for c0 in range(ROW_STRIDE):
                    h_ref[c0 * rows:(c0 + 1) * rows, cols] = a_ref[pl.ds(c0, rows, stride=ROW_STRIDE), :].astype(BF16)
            else:
                for c0 in range(ROW_STRIDE):
                    b_ref[c0 * quarter:(c0 + 1) * quarter, :] = a_ref[pl.ds(c0, quarter, stride=ROW_STRIDE), :]
                for c0 in range(ROW_STRIDE):
                    for b in range(ROW_STRIDE):
                        seg = ROW_STRIDE * c0 + b
                        h_ref[seg * rows:(seg + 1) * rows, cols] = b_ref[
                            pl.ds(c0 * quarter + b, rows, stride=ROW_STRIDE), :].astype(BF16)

    res = _dot(h_ref[...], w_ref[...])
    for c in range(dil):
        seg = _class_segment(c, dil)
        o_ref[c] = res[seg * rows:(seg + 1) * rows].astype(o_ref.dtype)


def _matmul_norm_classes(x, gain, w, group, ngroups, dil, out_dtype):
    m, k = x.shape
    ng = w.shape[1] // ngroups
    tm = _pick(m, (1024, 512, 256))
    tn = _pick(ng, (1024, 512, 256, 128))
    nj = ng // tn
    scratch = [pltpu.VMEM((tm, k), BF16)]
    if dil > 1:
        assert dil in (ROW_STRIDE, ROW_STRIDE ** 2)
        scratch += [pltpu.VMEM((tm, LANES), F32), pltpu.VMEM((tm, LANES), F32)]
    return pl.pallas_call(
        functools.partial(_mm_norm_kernel, dil=dil),
        out_shape=jax.ShapeDtypeStruct((dil, m // dil, ng), out_dtype),
        grid=(m // tm, nj),
        in_specs=[pl.BlockSpec((tm, k), lambda i, j: (i, 0)),
                  pl.BlockSpec((1, k), lambda i, j: (0, 0)),
                  pl.BlockSpec((k, tn), lambda i, j: (0, group * nj + j))],
        out_specs=pl.BlockSpec((dil, tm // dil, tn), lambda i, j: (0, i, j)),
        scratch_shapes=scratch,
        compiler_params=_cparams(("parallel", "arbitrary")),
        name=f"mm_norm_g{group}",
    )(x, gain, w)


def _mm_res_kernel(a_ref, w_ref, r_ref, o_ref):
    o_ref[...] = r_ref[...] + _dot(a_ref[...], w_ref[...])


def _matmul_residual(a, w, res):
    m, k = a.shape
    n = w.shape[1]
    tm = _pick(m, (1024, 512, 256, 128))
    tn = _pick(n, (1024, 512, 256, 128))
    return pl.pallas_call(
        _mm_res_kernel,
        out_shape=jax.ShapeDtypeStruct((m, n), F32),
        grid=(m // tm, n // tn),
        in_specs=[pl.BlockSpec((tm, k), lambda i, j: (i, 0)),
                  pl.BlockSpec((k, tn), lambda i, j: (0, j)),
                  pl.BlockSpec((tm, tn), lambda i, j: (i, j))],
        out_specs=pl.BlockSpec((tm, tn), lambda i, j: (i, j)),
        input_output_aliases={2: 0},
        compiler_params=_cparams(("parallel", "arbitrary")),
        name="mm_residual",
    )(a, w, res)


def _ffn_kernel(x_ref, g_ref, wg_ref, wu_ref, wd_ref, gf_ref, o_ref, h_ref, acc_ref, *, final_norm):
    j = pl.program_id(1)

    @pl.when(j == 0)
    def _():
        x = x_ref[...]
        h_ref[...] = _rms(x, g_ref[...]).astype(BF16)
        acc_ref[...] = x

    h = h_ref[...]
    gate = _dot(h, wg_ref[...])
    up = _dot(h, wu_ref[...])
    act = (gate * _sigmoid(gate) * up).astype(BF16)
    acc_ref[...] += _dot(act, wd_ref[...])

    @pl.when(j == pl.num_programs(1) - 1)
    def _():
        y = acc_ref[...]
        if final_norm:
            y = _rms(y, gf_ref[...])
        o_ref[...] = y


def _ffn(x, gain, wg, wu, wd, final_gain, final_norm):
    m, d = x.shape
    ff = wg.shape[1]
    tm = _pick(m, (512, 256, 128))
    tf = _pick(ff, (512, 256, 128))
    return pl.pallas_call(
        functools.partial(_ffn_kernel, final_norm=final_norm),
        out_shape=jax.ShapeDtypeStruct((m, d), F32),
        grid=(m // tm, ff // tf),
        in_specs=[pl.BlockSpec((tm, d), lambda i, j: (i, 0)),
                  pl.BlockSpec((1, d), lambda i, j: (0, 0)),
                  pl.BlockSpec((d, tf), lambda i, j: (0, j)),
                  pl.BlockSpec((d, tf), lambda i, j: (0, j)),
                  pl.BlockSpec((tf, d), lambda i, j: (j, 0)),
                  pl.BlockSpec((1, d), lambda i, j: (0, 0))],
        out_specs=pl.BlockSpec((tm, d), lambda i, j: (i, 0)),
        scratch_shapes=[pltpu.VMEM((tm, d), BF16), pltpu.VMEM((tm, d), F32)],
        input_output_aliases={0: 0},
        compiler_params=_cparams(("parallel", "arbitrary")),
        name="ffn",
    )(x, gain, wg, wu, wd, final_gain)


def _shift_kernel(x_ref, xp_ref, xn_ref, g_ref, mu_ref, o_ref, *, tt, n1, t1, t2):
    pos0, seq_len = _seq_pos(pl.program_id(0) * tt, n1, t1, t2)
    g = g_ref[...]
    h = _rms(x_ref[...], g)
    h_before = jnp.where(pos0 == 0, 0.0, _rms(xp_ref[7:8, :], g))
    h_after = jnp.where(pos0 + tt == seq_len, 0.0, _rms(xn_ref[0:1, :], g))
    row = lax.broadcasted_iota(jnp.int32, (tt, 1), 0)
    h_prev = jnp.where(row == 0, h_before, pltpu.roll(h, 1, 0))
    h_next = jnp.where(row == tt - 1, h_after, pltpu.roll(h, tt - 1, 0))
    xx = 0.5 * (h_prev + h_next) - h
    for c in range(6):
        o_ref[c] = (h + xx * mu_ref[c:c + 1, :]).astype(BF16)


def _token_shift(x, gain, mu, geom):
    n, d = x.shape
    n1, t1, t2 = geom
    tt = _pick(math.gcd(t1, t2), (256, 128, 64, 32, 16, 8))
    nb8 = n // 8
    return pl.pallas_call(
        functools.partial(_shift_kernel, tt=tt, n1=n1, t1=t1, t2=t2),
        out_shape=jax.ShapeDtypeStruct((6, n, d), BF16),
        grid=(n // tt,),
        in_specs=[pl.BlockSpec((tt, d), lambda i: (i, 0)),
                  pl.BlockSpec((8, d), lambda i: (jnp.maximum(i * (tt // 8) - 1, 0), 0)),
                  pl.BlockSpec((8, d), lambda i: (jnp.minimum((i + 1) * (tt // 8), nb8 - 1), 0)),
                  pl.BlockSpec((1, d), lambda i: (0, 0)),
                  pl.BlockSpec((6, d), lambda i: (0, 0))],
        out_specs=pl.BlockSpec((6, tt, d), lambda i: (0, i, 0)),
        compiler_params=_cparams(("parallel",)),
        name="token_shift",
    )(x, x, x, gain, mu)


def _mid_kernel(*refs, has_vres):
    if has_vres:
        (xv_ref, xw_ref, xa_ref, xg_ref, k_ref, v_ref, vf_ref,
         w0_ref, w1_ref, w2_ref, a0_ref, a1_ref, a2_ref, g1_ref, g2_ref, ka_ref,
         v0_ref, v1_ref, v2_ref,
         logw_ref, kd_ref, a_ref, vo_ref, g_ref) = refs
    else:
        (xv_ref, xw_ref, xa_ref, xg_ref, k_ref, v_ref,
         w0_ref, w1_ref, w2_ref, a0_ref, a1_ref, a2_ref, g1_ref, g2_ref, ka_ref,
         logw_ref, kd_ref, a_ref, vo_ref, g_ref) = refs
    xw = xw_ref[...]
    xa = xa_ref[...]
    k = k_ref[...].astype(F32)
    k_a = ka_ref[...]
    for z in range(2):
        lora = _dot(jnp.tanh(_dot(xw, w1_ref[z])).astype(BF16), w2_ref[z])
        y = -(w0_ref[z:z + 1, :] + lora)
        softplus = jnp.maximum(y, 0.0) + jnp.log(1.0 + jnp.exp(-jnp.abs(y)))
        logw_ref[z] = -jnp.exp(-softplus - 0.5)
        aa = _sigmoid(a0_ref[z:z + 1, :] + _dot(_dot(xa, a1_ref[z]).astype(BF16), a2_ref[z]))
        a_ref[z] = aa.astype(a_ref.dtype)
        kd_ref[z] = (k * (1.0 + (aa - 1.0) * k_a)).astype(kd_ref.dtype)
    v = v_ref[...].astype(F32)
    if has_vres:
        gate = _sigmoid(v0_ref[...] + _dot(_dot(xv_ref[...], v1_ref[...]).astype(BF16), v2_ref[...]))
        v = v + (vf_ref[...].astype(F32) - v) * gate
    vo_ref[...] = v.astype(vo_ref.dtype)
    g_ref[...] = _dot(_sigmoid(_dot(xg_ref[...], g1_ref[...])).astype(BF16), g2_ref[...]).astype(g_ref.dtype)


def _rwkv_mid(xs, rkv, v_first, p, vres):
    _, n, d = xs.shape
    tm = _pick(n, (128, 64, 32, 16, 8))
    has_vres = vres is not None

    def slab(c):
        return pl.BlockSpec((None, tm, d), lambda i, c=c: (c, i, 0))

    def whole(a):
        nd = a.ndim
        return pl.BlockSpec(a.shape, lambda i, nd=nd: (0,) * nd)

    tok = pl.BlockSpec((tm, d), lambda i: (i, 0))
    tok2 = pl.BlockSpec((2, tm, d), lambda i: (0, i, 0))
    weights = [p["w0"], p["w1"], p["w2"], p["a0"], p["a1"], p["a2"], p["g1"], p["g2"], p["k_a"]]
    args = [xs, xs, xs, xs, rkv, rkv]
    specs = [slab(2), slab(3), slab(4), slab(5), slab(1), slab(2)]
    if has_vres:
        args.append(v_first)
        specs.append(slab(2))
        weights += list(vres)
    args += weights
    specs += [whole(a) for a in weights]
    return pl.pallas_call(
        functools.partial(_mid_kernel, has_vres=has_vres),
        out_shape=(jax.ShapeDtypeStruct((2, n, d), F32),
                   jax.ShapeDtypeStruct((2, n, d), BF16),
                   jax.ShapeDtypeStruct((2, n, d), BF16),
                   jax.ShapeDtypeStruct((n, d), BF16),
                   jax.ShapeDtypeStruct((n, d), BF16)),
        grid=(n // tm,),
        in_specs=specs,
        out_specs=(tok2, tok2, tok2, tok, tok),
        compiler_params=_cparams(("parallel",)),
        name="rwkv_mid",
    )(*args)


def _pair_rows(x, head0):
    zero = jnp.zeros_like(x)
    return jnp.concatenate([jnp.where(head0, x, zero), jnp.where(head0, zero, x)], axis=0)


def _wkv1_kernel(r_ref, lw_ref, kd_ref, v_ref, k_ref, a_ref, kk_ref,
                 rp_ref, y0_ref, g_ref, h_ref, *, nchunk, npair, unroll, reverse):
    lane = lax.broadcasted_iota(jnp.int32, (CHUNK, PAIR), 1)
    head0 = lane < RW_HEAD
    col = jnp.where(head0, lane, lane - RW_HEAD)
    row = lax.broadcasted_iota(jnp.int32, (CHUNK, PAIR), 0)
    strict = (row < col) if reverse else (row > col)
    incl = (row <= col) if reverse else (row >= col)
    eye = row == col
    r2 = lax.broadcasted_iota(jnp.int32, (PAIR, PAIR), 0)
    q2 = lax.broadcasted_iota(jnp.int32, (PAIR, PAIR), 1)
    ones_bd = ((r2 < RW_HEAD) == (q2 < RW_HEAD)).astype(BF16)
    t1 = lax.broadcasted_iota(jnp.int32, (CHUNK, CHUNK), 0)
    s1 = lax.broadcasted_iota(jnp.int32, (CHUNK, CHUNK), 1)
    tri = ((t1 <= s1) if reverse else (t1 >= s1)).astype(BF16)
    tri2 = jnp.concatenate([tri, tri], axis=1)
    last = 0 if reverse else CHUNK - 1
    bd = functools.partial(_pair_rows, head0=head0)

    def prepare(rows, ln, lw, cum):
        kk = k_ref[rows, ln].astype(F32) * kk_ref[:, ln]
        ss = _dot((kk * kk).astype(BF16), ones_bd)
        kk = kk / jnp.maximum(jnp.sqrt(ss), 1e-12)
        b = kk * a_ref[rows, ln].astype(F32)
        kd = kd_ref[rows, ln].astype(F32)
        tot = cum[last:last + 1, :]
        e_neg = jnp.exp(-cum)
        e_tail = jnp.exp(tot - cum)
        w = dict(tot=tot)
        w["at"] = (-kk * jnp.exp(cum - lw)).astype(BF16)
        w["rt"] = r_ref[rows, ln].astype(F32) * jnp.exp(cum)
        w["bk_in"] = jnp.concatenate([bd((b * e_neg).astype(BF16)), bd((kd * e_neg).astype(BF16))], axis=0)
        w["bk_out"] = jnp.concatenate([(b * e_tail).astype(BF16), (kd * e_tail).astype(BF16)], axis=0)
        w["v"] = v_ref[rows, ln]
        return w

    def chunk_body(ci, carry):
        items = []
        for u in range(unroll):
            rows = pl.ds(pl.multiple_of((ci * unroll + u) * CHUNK, CHUNK), CHUNK)
            lw_all = lw_ref[rows, :]
            hi, lo = _split(lw_all)
            cum_all = _dot(tri2, jnp.concatenate([hi, lo], axis=0))
            for p in range(npair):
                ln = slice(p * PAIR, (p + 1) * PAIR)
                w = prepare(rows, ln, lw_all[:, ln], cum_all[:, ln])
                w["rows"], w["ln"] = rows, ln
                items.append(w)
        for w in items:
            m1 = _dot_nt(jnp.concatenate([w["at"], w["rt"].astype(BF16)], axis=0), w["bk_in"])
            a_ab = jnp.where(strict, m1[:CHUNK, :PAIR], 0.0)
            w["a_kv"] = jnp.concatenate([jnp.where(strict, m1[:CHUNK, PAIR:], 0.0),
                                         jnp.where(incl, m1[CHUNK:, PAIR:], 0.0)], axis=0).astype(BF16)
            w["a_rb"] = jnp.where(incl, m1[CHUNK:, :PAIR], 0.0).astype(BF16)
            w["pw"] = a_ab.astype(BF16)
            w["inv"] = jnp.where(eye, 1.0, a_ab)
        for w in items:
            w["pw"] = _dot(w["pw"], bd(w["pw"])).astype(BF16)
        for f in range(1, int(math.log2(CHUNK)) - 1):
            for w in items:
                both = _dot(jnp.concatenate([w["pw"], w["inv"].astype(BF16)], axis=0), bd(w["pw"]))
                w["pw"] = both[:CHUNK].astype(BF16)
                w["inv"] = w["inv"] + both[CHUNK:]
        for w in items:
            w["inv"] = w["inv"] + _dot(w["inv"].astype(BF16), bd(w["pw"]))
        for w in items:
            w["av"] = _dot(w["a_kv"], bd(w["v"]))
        for w in items:
            rhs = jnp.concatenate([bd(w["at"]), bd(w["av"][:CHUNK].astype(BF16))], axis=1)
            w["wu"] = _dot(w["inv"].astype(BF16), rhs).astype(BF16)
        for w in items:
            rows, ln = w["rows"], w["ln"]
            wu = w["wu"]
            rw = _dot(w["a_rb"], jnp.concatenate([bd(wu[:, :PAIR]), bd(wu[:, PAIR:])], axis=1))
            rp_ref[rows, ln] = (w["rt"] + rw[:, :PAIR]).astype(rp_ref.dtype)
            y0_ref[rows, ln] = (rw[:, PAIR:] + w["av"][CHUNK:]).astype(y0_ref.dtype)
            gfull = _dot_tn(w["bk_out"][:CHUNK], wu[:, :PAIR])
            gdiag = jnp.where(eye, jnp.exp(w["tot"]), 0.0)
            g_ref[rows, ln] = (jnp.where(head0, gfull[:CHUNK], gfull[CHUNK:]) + gdiag).astype(g_ref.dtype)
            hfull = _dot_tn(w["bk_out"], jnp.concatenate([wu[:, PAIR:], w["v"]], axis=0))
            h_ref[rows, ln] = jnp.where(head0, hfull[:CHUNK], hfull[CHUNK:]).astype(h_ref.dtype)
        return carry

    lax.fori_loop(0, nchunk // unroll, chunk_body, 0)


def _wkv_stage1(r3, r_slab, logw, kd, v, k3, k_slab, a, kk_scale, z, reverse):
    n, d = v.shape
    npair = _pick(d // PAIR, (4, 2, 1))
    lanes = npair * PAIR
    tb = _pick(n, (512, 256))

    def slab(c):
        return pl.BlockSpec((None, tb, lanes), lambda i, j, c=c: (c, i, j))

    tok = pl.BlockSpec((tb, lanes), lambda i, j: (i, j))
    return pl.pallas_call(
        functools.partial(_wkv1_kernel, nchunk=tb // CHUNK, npair=npair, unroll=WKV_UNROLL, reverse=reverse),
        out_shape=(jax.ShapeDtypeStruct((n, d), BF16),
                   jax.ShapeDtypeStruct((n, d), WKV_MAP_DTYPE),
                   jax.ShapeDtypeStruct((n, d), BF16),
                   jax.ShapeDtypeStruct((n, d), WKV_MAP_DTYPE)),
        grid=(n // tb, d // lanes),
        in_specs=[slab(r_slab), slab(z), slab(z), tok, slab(k_slab), slab(z),
                  pl.BlockSpec((1, lanes), lambda i, j: (0, j))],
        out_specs=(tok, tok, tok, tok),
        compiler_params=_cparams(("parallel", "parallel")),
        name="wkv_stage1_rev" if reverse else "wkv_stage1_fwd",
    )(r3, logw, kd, v, k3, a, kk_scale)


def _wkv2_kernel(rpf_ref, y0f_ref, gf_ref, hf_ref, rpb_ref, y0b_ref, gb_ref, hb_ref,
                 yf_ref, yb_ref, stf_ref, stb_ref, *, npairs, nchunks, n1, t1, t2):
    i = pl.program_id(0)
    pos_f, _ = _seq_pos(i * CHUNK, n1, t1, t2)
    pos_b, len_b = _seq_pos((nchunks - 1 - i) * CHUNK, n1, t1, t2)

    @pl.when(pos_f == 0)
    def _():
        stf_ref[...] = jnp.zeros_like(stf_ref)

    @pl.when(pos_b + CHUNK == len_b)
    def _():
        stb_ref[...] = jnp.zeros_like(stb_ref)

    lane = lax.broadcasted_iota(jnp.int32, (CHUNK, PAIR), 1)
    head0 = lane < RW_HEAD
    bd = functools.partial(_pair_rows, head0=head0)
    dirs = ((rpf_ref, y0f_ref, gf_ref, hf_ref, yf_ref, stf_ref), (rpb_ref, y0b_ref, gb_ref, hb_ref, yb_ref, stb_ref))
    items = [(refs, slice(p * PAIR, (p + 1) * PAIR)) for p in range(npairs) for refs in dirs]
    results = []
    for (rp_ref, _, g_ref, _, _, st_ref), ln in items:
        hi, lo = _split(st_ref[:, ln])
        lhs = jnp.concatenate([rp_ref[:, ln], g_ref[:, ln]], axis=0)
        results.append(_dot(jnp.concatenate([lhs, lhs], axis=1), jnp.concatenate([bd(hi), bd(lo)], axis=0)))
    for ((_, y0_ref, _, h_ref, y_ref, st_ref), ln), res in zip(items, results):
        y_ref[:, ln] = (y0_ref[:, ln].astype(F32) + res[:CHUNK]).astype(y_ref.dtype)
        st_ref[:, ln] = res[CHUNK:] + h_ref[:, ln].astype(F32)


def _wkv_stage2(fwd, bwd, geom):
    n, d = fwd[0].shape
    n1, t1, t2 = geom
    nchunks = n // CHUNK
    tok_f = pl.BlockSpec((CHUNK, d), lambda i: (i, 0))
    tok_b = pl.BlockSpec((CHUNK, d), lambda i: (nchunks - 1 - i, 0))
    return pl.pallas_call(
        functools.partial(_wkv2_kernel, npairs=d // PAIR, nchunks=nchunks, n1=n1, t1=t1, t2=t2),
        out_shape=(jax.ShapeDtypeStruct((n, d), WKV_OUT_DTYPE), jax.ShapeDtypeStruct((n, d), WKV_OUT_DTYPE)),
        grid=(nchunks,),
        in_specs=[tok_f] * 4 + [tok_b] * 4,
        out_specs=(tok_f, tok_b),
        scratch_shapes=[pltpu.VMEM((CHUNK, d), F32), pltpu.VMEM((CHUNK, d), F32)],
        compiler_params=_cparams(("arbitrary",)),
        name="wkv_stage2",
    )(*fwd, *bwd)


def _post_kernel(yf_ref, yb_ref, r_ref, kd_ref, v_ref, g_ref, rk_ref, lw_ref, lb_ref, o_ref, *, ngroups):
    r2 = lax.broadcasted_iota(jnp.int32, (PAIR, PAIR), 0)
    q2 = lax.broadcasted_iota(jnp.int32, (PAIR, PAIR), 1)
    same = (r2 < RW_HEAD) == (q2 < RW_HEAD)
    ones_bd = same.astype(BF16)
    mean_bd = (same.astype(F32) * (1.0 / RW_HEAD)).astype(BF16)
    for p in range(ngroups):
        ln = slice(p * PAIR, (p + 1) * PAIR)
        y = yf_ref[:, ln].astype(F32) + yb_ref[:, ln].astype(F32)
        hi, lo = _split(y)
        mean = _dot(hi, mean_bd) + _dot(lo, mean_bd)
        dev = y - mean
        hi, lo = _split(dev * dev)
        var = _dot(hi, mean_bd) + _dot(lo, mean_bd)
        yn = dev * lax.rsqrt(var + LNX_EPS) * lw_ref[:, ln] + lb_ref[:, ln]
        kd = kd_ref[0, :, ln].astype(F32) + kd_ref[1, :, ln].astype(F32)
        hi, lo = _split(r_ref[:, ln].astype(F32) * kd * rk_ref[:, ln])
        bonus = (_dot(hi, ones_bd) + _dot(lo, ones_bd)) * v_ref[:, ln].astype(F32)
        o_ref[:, ln] = ((yn + bonus) * g_ref[:, ln].astype(F32)).astype(o_ref.dtype)


def _rwkv_post(yf, yb, rkv, kd, v, g, r_k, lnx_w, lnx_b):
    n, d = yf.shape
    tm = _pick(n, (256, 128, 64, 32, 16, 8))
    tok = pl.BlockSpec((tm, d), lambda i: (i, 0))
    row = pl.BlockSpec((1, d), lambda i: (0, 0))
    return pl.pallas_call(
        functools.partial(_post_kernel, ngroups=d // PAIR),
        out_shape=jax.ShapeDtypeStruct((n, d), BF16),
        grid=(n // tm,),
        in_specs=[tok, tok, pl.BlockSpec((None, tm, d), lambda i: (0, i, 0)),
                  pl.BlockSpec((2, tm, d), lambda i: (0, i, 0)), tok, tok, row, row, row],
        out_specs=tok,
        compiler_params=_cparams(("parallel",)),
        name="rwkv_post",
    )(yf, yb, rkv, kd, v, g, r_k, lnx_w, lnx_b)


def _band_bias(tq, dil, nheads):
    nk = tq + 2 * BAND
    dist = jnp.abs(jnp.arange(nk, dtype=jnp.int32)[None, :] - BAND - jnp.arange(tq, dtype=jnp.int32)[:, None])
    slopes = jnp.exp2(-8.0 * jnp.arange(1, nheads + 1, dtype=F32) / nheads)
    bias = -slopes[:, None, None] * (dil * dist).astype(F32)[None] * LOG2E
    return jnp.where((dist <= BAND)[None], bias, NEG_BIG)


def _band_kernel(q_ref, kp_ref, kc_ref, kn_ref, vp_ref, vc_ref, vn_ref, bias_ref, o_ref, lse_ref,
                 *, tq, nheads, n1l, l1, l2):
    pos0, seq_len = _seq_pos(pl.program_id(0) * tq, n1l, l1, l2)
    prev_ok = pos0 > 0
    next_ok = pos0 + tq < seq_len
    col = lax.broadcasted_iota(jnp.int32, (1, tq + 2 * BAND), 1)
    in_seq = ((col >= BAND) | prev_ok) & ((col < tq + BAND) | next_ok)
    edge = jnp.where(in_seq, 0.0, NEG_BIG)
    head_lane = lax.broadcasted_iota(jnp.int32, (1, ATT_HEAD), 1)
    lse_all = jnp.zeros((tq, ATT_HEAD), F32)
    scale2 = ATT_HEAD ** -0.5 * LOG2E
    lanes = [slice(h * ATT_HEAD, (h + 1) * ATT_HEAD) for h in range(nheads)]
    for h0 in range(0, nheads, HEAD_BATCH):
        batch = range(h0, min(h0 + HEAD_BATCH, nheads))
        scores, probs, dens = {}, {}, {}
        for h in batch:
            keys = jnp.concatenate([kp_ref[:, lanes[h]], kc_ref[:, lanes[h]], kn_ref[:, lanes[h]]], axis=0)
            scores[h] = _dot_nt(q_ref[:, lanes[h]], keys) * scale2 + (bias_ref[h] + edge)
        for h in batch:
            m = jnp.max(scores[h], axis=-1, keepdims=True)
            probs[h] = jnp.exp2(scores[h] - m)
            dens[h] = jnp.sum(probs[h], axis=-1, keepdims=True)
            lse_all = lse_all + jnp.where(head_lane == h, (m + jnp.log2(dens[h])) * LN2, 0.0)
        for h in batch:
            vals = jnp.concatenate([vp_ref[:, lanes[h]], vc_ref[:, lanes[h]], vn_ref[:, lanes[h]]], axis=0)
            o_ref[:, lanes[h]] = (_dot(probs[h].astype(BF16), vals) / dens[h]).astype(o_ref.dtype)
    lse_ref[...] = lse_all


def _band_attention(qkv, group, geom, d):
    dil, nl, _ = qkv.shape
    n1, t1, t2 = geom
    n1l, l1, l2 = n1 // dil, t1 // dil, t2 // dil
    tq = _pick(math.gcd(l1, l2), (128, 64))
    hb = tq // BAND
    nhalo = nl // BAND
    nheads = d // ATT_HEAD

    def cur(s):
        return pl.BlockSpec((None, tq, d), lambda i, c, s=s: (c, i, s))

    def before(s):
        return pl.BlockSpec((None, BAND, d), lambda i, c, s=s: (c, jnp.maximum(i * hb - 1, 0), s))

    def after(s):
        return pl.BlockSpec((None, BAND, d), lambda i, c, s=s: (c, jnp.minimum((i + 1) * hb, nhalo - 1), s))

    return pl.pallas_call(
        functools.partial(_band_kernel, tq=tq, nheads=nheads, n1l=n1l, l1=l1, l2=l2),
        out_shape=(jax.ShapeDtypeStruct((dil, nl, d), BF16),
                   jax.ShapeDtypeStruct((dil, nl, ATT_HEAD), F32)),
        grid=(nl // tq, dil),
        in_specs=[cur(0), before(1), cur(1), after(1), before(2), cur(2), after(2),
                  pl.BlockSpec((nheads, tq, tq + 2 * BAND), lambda i, c: (0, 0, 0))],
        out_specs=(pl.BlockSpec((None, tq, d), lambda i, c: (c, i, 0)),
                   pl.BlockSpec((None, tq, ATT_HEAD), lambda i, c: (c, i, 0))),
        compiler_params=_cparams(("parallel", "parallel")),
        name=f"band_attention_g{group}",
    )(qkv, qkv, qkv, qkv, qkv, qkv, qkv, _band_bias(tq, dil, nheads))


def _combine_kernel(o0_ref, o1_ref, o2_ref, l0_ref, l1_ref, l2_ref, e_ref, out_ref, lse_scr, o_scr, *, dils):
    tm = out_ref.shape[0]

    def token_order(ref, scr, dil):
        if dil == 1:
            return ref[0].astype(F32)
        rows = tm // dil
        ngrp = ref.shape[2] // LANES
        for k in range(ngrp):
            for c in range(dil):
                scr[k, pl.ds(c, rows, stride=dil), :] = ref[c, :, k * LANES:(k + 1) * LANES].astype(F32)
        return jnp.concatenate([scr[k] for k in range(ngrp)], axis=1)

    lses = [token_order(l_ref, lse_scr.at[pl.ds(g, 1)], dil)
            for g, (l_ref, dil) in enumerate(zip((l0_ref, l1_ref, l2_ref), dils))]
    m = jnp.maximum(jnp.maximum(lses[0], lses[1]), lses[2])
    ws = [jnp.exp(l - m) for l in lses]
    tot = ws[0] + ws[1] + ws[2]
    e = e_ref[...]
    acc = None
    for w, o_ref, dil in zip(ws, (o0_ref, o1_ref, o2_ref), dils):
        hi, lo = _split(w / tot)
        term = (_dot(hi, e) + _dot(lo, e)) * token_order(o_ref, o_scr, dil)
        acc = term if acc is None else acc + term
    out_ref[...] = acc.astype(out_ref.dtype)


def _combine(os_, lses, d):
    dils = tuple(o.shape[0] for o in os_)
    n = os_[0].shape[0] * os_[0].shape[1]
    tm = _pick(n, (256,))
    head_of_lane = jnp.arange(d, dtype=jnp.int32) // ATT_HEAD
    expand = (jnp.arange(ATT_HEAD, dtype=jnp.int32)[:, None] == head_of_lane[None, :]).astype(BF16)

    def classes(dil, width):
        return pl.BlockSpec((dil, tm // dil, width), lambda i: (0, i, 0))

    return pl.pallas_call(
        functools.partial(_combine_kernel, dils=dils),
        out_shape=jax.ShapeDtypeStruct((n, d), BF16),
        grid=(n // tm,),
        in_specs=[classes(dil, d) for dil in dils] + [classes(dil, ATT_HEAD) for dil in dils]
                 + [pl.BlockSpec((ATT_HEAD, d), lambda i: (0, 0))],
        out_specs=pl.BlockSpec((tm, d), lambda i: (i, 0)),
        scratch_shapes=[pltpu.VMEM((len(dils), tm, LANES), F32), pltpu.VMEM((d // LANES, tm, LANES), F32)],
        compiler_params=_cparams(("parallel",)),
        name="attention_combine",
    )(*os_, *lses, expand)


def _rwkv_layer(x, gain, p, v_first, vres, geom):
    xs = _token_shift(x, gain, p["mu"], geom)
    rkv = _matmul_batched(xs, p["w_rkv"], 3, RKV_DTYPE)
    if v_first is None:
        v_first = rkv
    logw, kd, a, v, g = _rwkv_mid(xs, rkv, v_first, p, vres)
    maps = [_wkv_stage1(rkv, 0, logw, kd, v, rkv, 1, a, p["k_k"], z, reverse=(z == 1)) for z in range(2)]
    yf, yb = _wkv_stage2(maps[0], maps[1], geom)
    o = _rwkv_post(yf, yb, rkv, kd, v, g, p["r_k"], p["lnx_w"], p["lnx_b"])
    return _matmul_residual(o, p["w_o"], x), v_first


def _attention_layer(x, gain, w_qkv, w_o, geom):
    d = x.shape[1]
    outs, lses = [], []
    for gi, (window, dil) in enumerate(DIL_PATTERNS):
        assert (window // 2) // dil == BAND
        qkv = _matmul_norm_classes(x, gain, w_qkv, gi, N_GROUPS, dil, BF16)
        o, lse = _band_attention(qkv, gi, geom, d)
        outs.append(o)
        lses.append(lse)
    return _matmul_residual(_combine(outs, lses, d), w_o, x)


def kernel(x_prompt, x_sample, ln1, ln2, ln_f, rw_mu, rw_w_rkv, rw_w0, rw_w1, rw_w2, rw_a0, rw_a1, rw_a2, rw_v0, rw_v1, rw_v2, rw_g1, rw_g2, rw_k_k, rw_k_a, rw_r_k, rw_lnx_w, rw_lnx_b, rw_w_o, at_w_qkv, at_w_o, ffn_w_gate, ffn_w_up, ffn_w_down):
    b1, t1, d = x_prompt.shape
    b2, t2, _ = x_sample.shape
    n1, n2 = b1 * t1, b2 * t2
    geom = (n1, t1, t2)
    depth = ln1.shape[0]
    assert d % PAIR == 0 and d % ATT_HEAD == 0
    max_dil = max(dil for _, dil in DIL_PATTERNS)
    assert t1 % (max_dil * BAND) == 0 and t2 % (max_dil * BAND) == 0

    x = jnp.concatenate([x_prompt.reshape(n1, d), x_sample.reshape(n2, d)], axis=0)
    bf = lambda w: w.astype(BF16)
    row = lambda w: w.reshape(1, d)

    v_first = None
    for i in range(depth):
        j = i // 2
        if i % 2 == 0:
            p = dict(mu=rw_mu[j], w_rkv=bf(rw_w_rkv[j]), w0=rw_w0[j], w1=bf(rw_w1[j]), w2=bf(rw_w2[j]),
                     a0=rw_a0[j], a1=bf(rw_a1[j]), a2=bf(rw_a2[j]), g1=bf(rw_g1[j]), g2=bf(rw_g2[j]),
                     k_k=row(rw_k_k[j]), k_a=row(rw_k_a[j]), r_k=rw_r_k[j].reshape(1, d),
                     lnx_w=row(rw_lnx_w[j]), lnx_b=row(rw_lnx_b[j]), w_o=bf(rw_w_o[j]))
            vres = None if j == 0 else (row(rw_v0[j - 1]), bf(rw_v1[j - 1]), bf(rw_v2[j - 1]))
            x, v_first = _rwkv_layer(x, row(ln1[i]), p, v_first, vres, geom)
        else:
            w_qkv = bf(at_w_qkv[j]).reshape(d, -1)
            x = _attention_layer(x, row(ln1[i]), w_qkv, bf(at_w_o[j]), geom)
        x = _ffn(x, row(ln2[i]), bf(ffn_w_gate[i]), bf(ffn_w_up[i]), bf(ffn_w_down[i]),
                 row(ln_f), final_norm=(i == depth - 1))
    return x[:n1].reshape(b1, t1, d), x[n1:].reshape(b2, t2, d)
```

```python
import functools
import math

import jax
import jax.numpy as jnp
from jax import lax
from jax.experimental import pallas as pl
from jax.experimental.pallas import tpu as pltpu

F32 = jnp.float32
BF16 = jnp.bfloat16

RW_HEAD = 64
PAIR = 2 * RW_HEAD
ATT_HEAD = 128
LANES = 128
ROW_STRIDE = 4
DIL_PATTERNS = ((128, 1), (512, 4), (2048, 16))
BAND = 64
LNX_EPS = 64e-5
RMS_EPS = 1e-6
CHUNK = 64
NEG_BIG = -1e30
LOG2E = math.log2(math.e)
LN2 = math.log(2.0)
EXP_NEG_HALF = math.exp(-0.5)
RKV_DTYPE = BF16
WKV_MAP_DTYPE = BF16
WKV_OUT_DTYPE = BF16
HEAD_BATCH = 8
WKV_UNROLL = 4
V7X_VMEM_LIMIT_BYTES = 56 * 1024 * 1024


def _pick(n, candidates):
    for c in candidates:
        if n % c == 0:
            return c
    raise ValueError(f"no tile in {candidates} divides {n}")


def _cparams(sem):
    return pltpu.CompilerParams(dimension_semantics=sem, vmem_limit_bytes=V7X_VMEM_LIMIT_BYTES)


def _dot(a, b):
    return jnp.dot(a, b, preferred_element_type=F32)


def _dot_nt(a, b):
    return lax.dot_general(a, b, (((1,), (1,)), ((), ())), preferred_element_type=F32)


def _dot_tn(a, b):
    return lax.dot_general(a, b, (((0,), (0,)), ((), ())), preferred_element_type=F32)


def _split(x):
    hi = x.astype(BF16)
    lo = (x - hi.astype(F32)).astype(BF16)
    return hi, lo


def _sigmoid(x):
    return 1.0 / (1.0 + jnp.exp(-x))


def _rms(x, g):
    return x * lax.rsqrt(jnp.mean(x * x, axis=-1, keepdims=True) + RMS_EPS) * g


def _seq_pos(idx, n1, t1, t2):
    first = idx < n1
    pos = jnp.where(first, lax.rem(idx, t1), lax.rem(idx - n1, t2))
    return pos, jnp.where(first, t1, t2)


def _mm_kernel(a_ref, w_ref, o_ref):
    o_ref[...] = _dot(a_ref[...], w_ref[...]).astype(o_ref.dtype)


def _matmul_batched(a, w, groups, out_dtype):
    _, m, k = a.shape
    n = w.shape[2]
    tm = _pick(m, (1024, 512, 256, 128))
    tn = _pick(n, (1024, 512, 256, 128))
    return pl.pallas_call(
        _mm_kernel,
        out_shape=jax.ShapeDtypeStruct((groups, m, n), out_dtype),
        grid=(groups, m // tm, n // tn),
        in_specs=[pl.BlockSpec((None, tm, k), lambda g, i, j: (g, i, 0)),
                  pl.BlockSpec((None, k, tn), lambda g, i, j: (g, 0, j))],
        out_specs=pl.BlockSpec((None, tm, tn), lambda g, i, j: (g, i, j)),
        compiler_params=_cparams(("parallel", "parallel", "arbitrary")),
        name="mm_batched",
    )(a, w)


def _class_segment(c, dil):
    return c if dil <= ROW_STRIDE else ROW_STRIDE * (c % ROW_STRIDE) + c // ROW_STRIDE


def _store_rows_by_class(hk, h_ref, cols, dil, a_ref, b_ref):
    tm = hk.shape[0]
    rows = tm // dil
    if dil == 1:
        h_ref[:, cols] = hk.astype(BF16)
        return
    a_ref[...] = hk
    if dil == ROW_STRIDE:
        for c0 in range(ROW_STRIDE):
            h_ref[c0 * rows:(c0 + 1) * rows, cols] = a_ref[pl.ds(c0, rows, stride=ROW_STRIDE), :].astype(BF16)
        return
    quarter = tm // ROW_STRIDE
    for c0 in range(ROW_STRIDE):
        b_ref[c0 * quarter:(c0 + 1) * quarter, :] = a_ref[pl.ds(c0, quarter, stride=ROW_STRIDE), :]
    for c0 in range(ROW_STRIDE):
        for b in range(ROW_STRIDE):
            seg = ROW_STRIDE * c0 + b
            h_ref[seg * rows:(seg + 1) * rows, cols] = b_ref[
                pl.ds(c0 * quarter + b, rows, stride=ROW_STRIDE), :].astype(BF16)


def _mm_qkv_kernel(x_ref, g_ref, w_ref, *refs, dils):
    ng = len(dils)
    o_refs, h_ref, a_ref, b_ref = refs[:ng], refs[ng], refs[ng + 1], refs[ng + 2]
    tm, k = x_ref.shape
    group = pl.program_id(1)

    @pl.when((group == 0) & (pl.program_id(2) == 0))
    def _():
        x = x_ref[...]
        inv = lax.rsqrt(jnp.mean(x * x, axis=-1, keepdims=True) + RMS_EPS)
        for kk in range(k // LANES):
            cols = slice(kk * LANES, (kk + 1) * LANES)
            hk = x_ref[:, cols] * inv * g_ref[:, cols]
            for gi, dil in enumerate(dils):
                _store_rows_by_class(hk, h_ref.at[gi], cols, dil, a_ref, b_ref)

    for gi, (o_ref, dil) in enumerate(zip(o_refs, dils)):
        @pl.when(group == gi)
        def _(gi=gi, o_ref=o_ref, dil=dil):
            rows = tm // dil
            res = _dot(h_ref[gi], w_ref[...])
            for c in range(dil):
                seg = _class_segment(c, dil)
                o_ref[c] = res[seg * rows:(seg + 1) * rows].astype(o_ref.dtype)


def _matmul_norm_classes(x, gain, w, dils, out_dtype):
    m, k = x.shape
    ngroups = len(dils)
    ng = w.shape[1] // ngroups
    tm = _pick(m, (1024, 512, 256))
    tn = _pick(ng, (1024, 512, 256, 128))
    nj = ng // tn
    assert all(dil in (1, ROW_STRIDE, ROW_STRIDE ** 2) for dil in dils)

    def out_spec(gi, dil):
        def index(i, g, j):
            return (0, i, jnp.where(g < gi, 0, jnp.where(g == gi, j, nj - 1)))
        return pl.BlockSpec((dil, tm // dil, tn), index)

    return pl.pallas_call(
        functools.partial(_mm_qkv_kernel, dils=tuple(dils)),
        out_shape=tuple(jax.ShapeDtypeStruct((dil, m // dil, ng), out_dtype) for dil in dils),
        grid=(m // tm, ngroups, nj),
        in_specs=[pl.BlockSpec((tm, k), lambda i, g, j: (i, 0)),
                  pl.BlockSpec((1, k), lambda i, g, j: (0, 0)),
                  pl.BlockSpec((k, tn), lambda i, g, j: (0, g * nj + j))],
        out_specs=tuple(out_spec(gi, dil) for gi, dil in enumerate(dils)),
        scratch_shapes=[pltpu.VMEM((ngroups, tm, k), BF16),
                        pltpu.VMEM((tm, LANES), F32), pltpu.VMEM((tm, LANES), F32)],
        compiler_params=_cparams(("parallel", "arbitrary", "arbitrary")),
        name="mm_norm_qkv",
    )(x, gain, w)


def _mm_res_kernel(a_ref, w_ref, r_ref, o_ref):
    o_ref[...] = r_ref[...] + _dot(a_ref[...], w_ref[...])


def _matmul_residual(a, w, res):
    m, k = a.shape
    n = w.shape[1]
    tm = _pick(m, (1024, 512, 256, 128))
    tn = _pick(n, (1024, 512, 256, 128))
    return pl.pallas_call(
        _mm_res_kernel,
        out_shape=jax.ShapeDtypeStruct((m, n), F32),
        grid=(m // tm, n // tn),
        in_specs=[pl.BlockSpec((tm, k), lambda i, j: (i, 0)),
                  pl.BlockSpec((k, tn), lambda i, j: (0, j)),
                  pl.BlockSpec((tm, tn), lambda i, j: (i, j))],
        out_specs=pl.BlockSpec((tm, tn), lambda i, j: (i, j)),
        input_output_aliases={2: 0},
        compiler_params=_cparams(("parallel", "arbitrary")),
        name="mm_residual",
    )(a, w, res)


def _ffn_kernel(x_ref, g_ref, wg_ref, wu_ref, wd_ref, gf_ref, o_ref, h_ref, acc_ref, *, final_norm):
    j = pl.program_id(1)

    @pl.when(j == 0)
    def _():
        x = x_ref[...]
        h_ref[...] = _rms(x, g_ref[...]).astype(BF16)
        acc_ref[...] = x

    h = h_ref[...]
    gate = _dot(h, wg_ref[...])
    up = _dot(h, wu_ref[...])
    act = (gate * _sigmoid(gate) * up).astype(BF16)
    acc_ref[...] += _dot(act, wd_ref[...])

    @pl.when(j == pl.num_programs(1) - 1)
    def _():
        y = acc_ref[...]
        if final_norm:
            y = _rms(y, gf_ref[...])
        o_ref[...] = y


def _ffn(x, gain, wg, wu, wd, final_gain, final_norm):
    m, d = x.shape
    ff = wg.shape[1]
    tm = _pick(m, (512, 256, 128))
    tf = _pick(ff, (512, 256, 128))
    return pl.pallas_call(
        functools.partial(_ffn_kernel, final_norm=final_norm),
        out_shape=jax.ShapeDtypeStruct((m, d), F32),
        grid=(m // tm, ff // tf),
        in_specs=[pl.BlockSpec((tm, d), lambda i, j: (i, 0)),
                  pl.BlockSpec((1, d), lambda i, j: (0, 0)),
                  pl.BlockSpec((d, tf), lambda i, j: (0, j)),
                  pl.BlockSpec((d, tf), lambda i, j: (0, j)),
                  pl.BlockSpec((tf, d), lambda i, j: (j, 0)),
                  pl.BlockSpec((1, d), lambda i, j: (0, 0))],
        out_specs=pl.BlockSpec((tm, d), lambda i, j: (i, 0)),
        scratch_shapes=[pltpu.VMEM((tm, d), BF16), pltpu.VMEM((tm, d), F32)],
        input_output_aliases={0: 0},
        compiler_params=_cparams(("parallel", "arbitrary")),
        name="ffn",
    )(x, gain, wg, wu, wd, final_gain)


def _shift_kernel(x_ref, xp_ref, xn_ref, g_ref, mu_ref, o_ref, *, tt, n1, t1, t2):
    pos0, seq_len = _seq_pos(pl.program_id(0) * tt, n1, t1, t2)

    def inv_rms(x):
        return lax.rsqrt(jnp.mean(x * x, axis=-1, keepdims=True) + RMS_EPS)

    inv = inv_rms(x_ref[...])
    inv_before = jnp.where(pos0 == 0, 0.0, inv_rms(xp_ref[7:8, :]))
    inv_after = jnp.where(pos0 + tt == seq_len, 0.0, inv_rms(xn_ref[0:1, :]))
    row = lax.broadcasted_iota(jnp.int32, (tt, 1), 0)
    for k in range(x_ref.shape[1] // LANES):
        cols = slice(k * LANES, (k + 1) * LANES)
        g = g_ref[:, cols]
        h = x_ref[:, cols] * inv * g
        h_prev = jnp.where(row == 0, xp_ref[7:8, cols] * inv_before * g, pltpu.roll(h, 1, 0))
        h_next = jnp.where(row == tt - 1, xn_ref[0:1, cols] * inv_after * g, pltpu.roll(h, tt - 1, 0))
        xx = 0.5 * (h_prev + h_next) - h
        for c in range(6):
            o_ref[c, :, cols] = (h + xx * mu_ref[c:c + 1, cols]).astype(BF16)


def _token_shift(x, gain, mu, geom):
    n, d = x.shape
    n1, t1, t2 = geom
    tt = _pick(math.gcd(t1, t2), (256, 128, 64, 32, 16, 8))
    nb8 = n // 8
    return pl.pallas_call(
        functools.partial(_shift_kernel, tt=tt, n1=n1, t1=t1, t2=t2),
        out_shape=jax.ShapeDtypeStruct((6, n, d), BF16),
        grid=(n // tt,),
        in_specs=[pl.BlockSpec((tt, d), lambda i: (i, 0)),
                  pl.BlockSpec((8, d), lambda i: (jnp.maximum(i * (tt // 8) - 1, 0), 0)),
                  pl.BlockSpec((8, d), lambda i: (jnp.minimum((i + 1) * (tt // 8), nb8 - 1), 0)),
                  pl.BlockSpec((1, d), lambda i: (0, 0)),
                  pl.BlockSpec((6, d), lambda i: (0, 0))],
        out_specs=pl.BlockSpec((6, tt, d), lambda i: (0, i, 0)),
        compiler_params=_cparams(("parallel",)),
        name="token_shift",
    )(x, x, x, gain, mu)


def _mid_kernel(*refs, has_vres):
    if has_vres:
        (xv_ref, xw_ref, xa_ref, xg_ref, k_ref, v_ref, vf_ref,
         w0_ref, w1_ref, w2_ref, a0_ref, a1_ref, a2_ref, g1_ref, g2_ref, ka_ref,
         v0_ref, v1_ref, v2_ref,
         logw_ref, kd_ref, a_ref, vo_ref, g_ref) = refs
    else:
        (xv_ref, xw_ref, xa_ref, xg_ref, k_ref, v_ref,
         w0_ref, w1_ref, w2_ref, a0_ref, a1_ref, a2_ref, g1_ref, g2_ref, ka_ref,
         logw_ref, kd_ref, a_ref, vo_ref, g_ref) = refs
    xw = xw_ref[...]
    xa = xa_ref[...]
    k = k_ref[...].astype(F32)
    k_a = ka_ref[...]
    for z in range(2):
        lora = _dot(jnp.tanh(_dot(xw, w1_ref[z])).astype(BF16), w2_ref[z])
        logw_ref[z] = -EXP_NEG_HALF * _sigmoid(w0_ref[z:z + 1, :] + lora)
        aa = _sigmoid(a0_ref[z:z + 1, :] + _dot(_dot(xa, a1_ref[z]).astype(BF16), a2_ref[z]))
        a_ref[z] = aa.astype(a_ref.dtype)
        kd_ref[z] = (k * (1.0 + (aa - 1.0) * k_a)).astype(kd_ref.dtype)
    v = v_ref[...].astype(F32)
    if has_vres:
        gate = _sigmoid(v0_ref[...] + _dot(_dot(xv_ref[...], v1_ref[...]).astype(BF16), v2_ref[...]))
        v = v + (vf_ref[...].astype(F32) - v) * gate
    vo_ref[...] = v.astype(vo_ref.dtype)
    g_ref[...] = _dot(_sigmoid(_dot(xg_ref[...], g1_ref[...])).astype(BF16), g2_ref[...]).astype(g_ref.dtype)


def _rwkv_mid(xs, rkv, v_first, p, vres):
    _, n, d = xs.shape
    tm = _pick(n, (128, 64, 32, 16, 8))
    has_vres = vres is not None

    def slab(c):
        return pl.BlockSpec((None, tm, d), lambda i, c=c: (c, i, 0))

    def whole(a):
        nd = a.ndim
        return pl.BlockSpec(a.shape, lambda i, nd=nd: (0,) * nd)

    tok = pl.BlockSpec((tm, d), lambda i: (i, 0))
    tok2 = pl.BlockSpec((2, tm, d), lambda i: (0, i, 0))
    weights = [p["w0"], p["w1"], p["w2"], p["a0"], p["a1"], p["a2"], p["g1"], p["g2"], p["k_a"]]
    args = [xs, xs, xs, xs, rkv, rkv]
    specs = [slab(2), slab(3), slab(4), slab(5), slab(1), slab(2)]
    if has_vres:
        args.append(v_first)
        specs.append(slab(2))
        weights += list(vres)
    args += weights
    specs += [whole(a) for a in weights]
    return pl.pallas_call(
        functools.partial(_mid_kernel, has_vres=has_vres),
        out_shape=(jax.ShapeDtypeStruct((2, n, d), F32),
                   jax.ShapeDtypeStruct((2, n, d), BF16),
                   jax.ShapeDtypeStruct((2, n, d), BF16),
                   jax.ShapeDtypeStruct((n, d), BF16),
                   jax.ShapeDtypeStruct((n, d), BF16)),
        grid=(n // tm,),
        in_specs=specs,
        out_specs=(tok2, tok2, tok2, tok, tok),
        compiler_params=_cparams(("parallel",)),
        name="rwkv_mid",
    )(*args)


def _pair_rows(x, head0):
    zero = jnp.zeros_like(x)
    return jnp.concatenate([jnp.where(head0, x, zero), jnp.where(head0, zero, x)], axis=0)


def _wkv1_kernel(r_ref, lw_ref, kd_ref, v_ref, k_ref, a_ref, kk_ref,
                 rp_ref, y0_ref, g_ref, h_ref, *, nchunk, npair, unroll, reverse):
    lane = lax.broadcasted_iota(jnp.int32, (CHUNK, PAIR), 1)
    head0 = lane < RW_HEAD
    col = jnp.where(head0, lane, lane - RW_HEAD)
    row = lax.broadcasted_iota(jnp.int32, (CHUNK, PAIR), 0)
    strict = (row < col) if reverse else (row > col)
    incl = (row <= col) if reverse else (row >= col)
    eye = row == col
    r2 = lax.broadcasted_iota(jnp.int32, (PAIR, PAIR), 0)
    q2 = lax.broadcasted_iota(jnp.int32, (PAIR, PAIR), 1)
    ones_bd = ((r2 < RW_HEAD) == (q2 < RW_HEAD)).astype(BF16)
    t1 = lax.broadcasted_iota(jnp.int32, (CHUNK, CHUNK), 0)
    s1 = lax.broadcasted_iota(jnp.int32, (CHUNK, CHUNK), 1)
    tri = ((t1 <= s1) if reverse else (t1 >= s1)).astype(BF16)
    tri2 = jnp.concatenate([tri, tri], axis=1)
    last = 0 if reverse else CHUNK - 1
    bd = functools.partial(_pair_rows, head0=head0)

    def prepare(rows, ln, lw, cum):
        kk = k_ref[rows, ln].astype(F32) * kk_ref[:, ln]
        ss = _dot((kk * kk).astype(BF16), ones_bd)
        kk = kk / jnp.maximum(jnp.sqrt(ss), 1e-12)
        b = kk * a_ref[rows, ln].astype(F32)
        kd = kd_ref[rows, ln].astype(F32)
        tot = cum[last:last + 1, :]
        e_neg = jnp.exp(-cum)
        e_tail = jnp.exp(tot - cum)
        w = dict(tot=tot)
        w["at"] = (-kk * jnp.exp(cum - lw)).astype(BF16)
        w["rt"] = r_ref[rows, ln].astype(F32) * jnp.exp(cum)
        w["bk_in"] = jnp.concatenate([bd((b * e_neg).astype(BF16)), bd((kd * e_neg).astype(BF16))], axis=0)
        w["bk_out"] = jnp.concatenate([(b * e_tail).astype(BF16), (kd * e_tail).astype(BF16)], axis=0)
        w["v"] = v_ref[rows, ln]
        return w

    def chunk_body(ci, carry):
        items = []
        for u in range(unroll):
            rows = pl.ds(pl.multiple_of((ci * unroll + u) * CHUNK, CHUNK), CHUNK)
            lw_all = lw_ref[rows, :]
            hi, lo = _split(lw_all)
            cum_all = _dot(tri2, jnp.concatenate([hi, lo], axis=0))
            for p in range(npair):
                ln = slice(p * PAIR, (p + 1) * PAIR)
                w = prepare(rows, ln, lw_all[:, ln], cum_all[:, ln])
                w["rows"], w["ln"] = rows, ln
                items.append(w)
        for w in items:
            m1 = _dot_nt(jnp.concatenate([w["at"], w["rt"].astype(BF16)], axis=0), w["bk_in"])
            a_ab = jnp.where(strict, m1[:CHUNK, :PAIR], 0.0)
            w["a_kv"] = jnp.concatenate([jnp.where(strict, m1[:CHUNK, PAIR:], 0.0),
                                         jnp.where(incl, m1[CHUNK:, PAIR:], 0.0)], axis=0).astype(BF16)
            w["a_rb"] = jnp.where(incl, m1[CHUNK:, :PAIR], 0.0).astype(BF16)
            w["pw"] = a_ab.astype(BF16)
            w["inv"] = jnp.where(eye, 1.0, a_ab)
        for w in items:
            w["pw"] = _dot(w["pw"], bd(w["pw"])).astype(BF16)
        for f in range(1, int(math.log2(CHUNK)) - 1):
            for w in items:
                both = _dot(jnp.concatenate([w["pw"], w["inv"].astype(BF16)], axis=0), bd(w["pw"]))
                w["pw"] = both[:CHUNK].astype(BF16)
                w["inv"] = w["inv"] + both[CHUNK:]
        for w in items:
            w["inv"] = w["inv"] + _dot(w["inv"].astype(BF16), bd(w["pw"]))
        for w in items:
            w["av"] = _dot(w["a_kv"], bd(w["v"]))
        for w in items:
            rhs = jnp.concatenate([bd(w["at"]), bd(w["av"][:CHUNK].astype(BF16))], axis=1)
            w["wu"] = _dot(w["inv"].astype(BF16), rhs).astype(BF16)
        for w in items:
            rows, ln = w["rows"], w["ln"]
            wu = w["wu"]
            rw = _dot(w["a_rb"], jnp.concatenate([bd(wu[:, :PAIR]), bd(wu[:, PAIR:])], axis=1))
            rp_ref[rows, ln] = (w["rt"] + rw[:, :PAIR]).astype(rp_ref.dtype)
            y0_ref[rows, ln] = (rw[:, PAIR:] + w["av"][CHUNK:]).astype(y0_ref.dtype)
            gfull = _dot_tn(w["bk_out"][:CHUNK], wu[:, :PAIR])
            gdiag = jnp.where(eye, jnp.exp(w["tot"]), 0.0)
            g_ref[rows, ln] = (jnp.where(head0, gfull[:CHUNK], gfull[CHUNK:]) + gdiag).astype(g_ref.dtype)
            hfull = _dot_tn(w["bk_out"], jnp.concatenate([wu[:, PAIR:], w["v"]], axis=0))
            h_ref[rows, ln] = jnp.where(head0, hfull[:CHUNK], hfull[CHUNK:]).astype(h_ref.dtype)
        return carry

    lax.fori_loop(0, nchunk // unroll, chunk_body, 0)


def _wkv_stage1(r3, r_slab, logw, kd, v, k3, k_slab, a, kk_scale, z, reverse):
    n, d = v.shape
    npair = _pick(d // PAIR, (4, 2, 1))
    lanes = npair * PAIR
    tb = _pick(n, (512, 256))

    def slab(c):
        return pl.BlockSpec((None, tb, lanes), lambda i, j, c=c: (c, i, j))

    tok = pl.BlockSpec((tb, lanes), lambda i, j: (i, j))
    return pl.pallas_call(
        functools.partial(_wkv1_kernel, nchunk=tb // CHUNK, npair=npair, unroll=WKV_UNROLL, reverse=reverse),
        out_shape=(jax.ShapeDtypeStruct((n, d), BF16),
                   jax.ShapeDtypeStruct((n, d), WKV_MAP_DTYPE),
                   jax.ShapeDtypeStruct((n, d), BF16),
                   jax.ShapeDtypeStruct((n, d), WKV_MAP_DTYPE)),
        grid=(n // tb, d // lanes),
        in_specs=[slab(r_slab), slab(z), slab(z), tok, slab(k_slab), slab(z),
                  pl.BlockSpec((1, lanes), lambda i, j: (0, j))],
        out_specs=(tok, tok, tok, tok),
        compiler_params=_cparams(("parallel", "parallel")),
        name="wkv_stage1_rev" if reverse else "wkv_stage1_fwd",
    )(r3, logw, kd, v, k3, a, kk_scale)


def _wkv2_kernel(rpf_ref, y0f_ref, gf_ref, hf_ref, rpb_ref, y0b_ref, gb_ref, hb_ref,
                 yf_ref, yb_ref, stf_ref, stb_ref, *, npairs, nchunks, n1, t1, t2):
    i = pl.program_id(0)
    pos_f, _ = _seq_pos(i * CHUNK, n1, t1, t2)
    pos_b, len_b = _seq_pos((nchunks - 1 - i) * CHUNK, n1, t1, t2)

    @pl.when(pos_f == 0)
    def _():
        stf_ref[...] = jnp.zeros_like(stf_ref)

    @pl.when(pos_b + CHUNK == len_b)
    def _():
        stb_ref[...] = jnp.zeros_like(stb_ref)

    lane = lax.broadcasted_iota(jnp.int32, (CHUNK, PAIR), 1)
    head0 = lane < RW_HEAD
    bd = functools.partial(_pair_rows, head0=head0)
    dirs = ((rpf_ref, y0f_ref, gf_ref, hf_ref, yf_ref, stf_ref), (rpb_ref, y0b_ref, gb_ref, hb_ref, yb_ref, stb_ref))
    items = [(refs, slice(p * PAIR, (p + 1) * PAIR)) for p in range(npairs) for refs in dirs]
    results = []
    for (rp_ref, _, g_ref, _, _, st_ref), ln in items:
        hi, lo = _split(st_ref[:, ln])
        lhs = jnp.concatenate([rp_ref[:, ln], g_ref[:, ln]], axis=0)
        results.append(_dot(jnp.concatenate([lhs, lhs], axis=1), jnp.concatenate([bd(hi), bd(lo)], axis=0)))
    for ((_, y0_ref, _, h_ref, y_ref, st_ref), ln), res in zip(items, results):
        y_ref[:, ln] = (y0_ref[:, ln].astype(F32) + res[:CHUNK]).astype(y_ref.dtype)
        st_ref[:, ln] = res[CHUNK:] + h_ref[:, ln].astype(F32)


def _wkv_stage2(fwd, bwd, geom):
    n, d = fwd[0].shape
    n1, t1, t2 = geom
    nchunks = n // CHUNK
    tok_f = pl.BlockSpec((CHUNK, d), lambda i: (i, 0))
    tok_b = pl.BlockSpec((CHUNK, d), lambda i: (nchunks - 1 - i, 0))
    return pl.pallas_call(
        functools.partial(_wkv2_kernel, npairs=d // PAIR, nchunks=nchunks, n1=n1, t1=t1, t2=t2),
        out_shape=(jax.ShapeDtypeStruct((n, d), WKV_OUT_DTYPE), jax.ShapeDtypeStruct((n, d), WKV_OUT_DTYPE)),
        grid=(nchunks,),
        in_specs=[tok_f] * 4 + [tok_b] * 4,
        out_specs=(tok_f, tok_b),
        scratch_shapes=[pltpu.VMEM((CHUNK, d), F32), pltpu.VMEM((CHUNK, d), F32)],
        compiler_params=_cparams(("arbitrary",)),
        name="wkv_stage2",
    )(*fwd, *bwd)


def _post_kernel(yf_ref, yb_ref, r_ref, kd_ref, v_ref, g_ref, rk_ref, lw_ref, lb_ref, o_ref, *, ngroups):
    r2 = lax.broadcasted_iota(jnp.int32, (PAIR, PAIR), 0)
    q2 = lax.broadcasted_iota(jnp.int32, (PAIR, PAIR), 1)
    same = (r2 < RW_HEAD) == (q2 < RW_HEAD)
    ones_bd = same.astype(BF16)
    mean_bd = (same.astype(F32) * (1.0 / RW_HEAD)).astype(BF16)
    for p in range(ngroups):
        ln = slice(p * PAIR, (p + 1) * PAIR)
        y = yf_ref[:, ln].astype(F32) + yb_ref[:, ln].astype(F32)
        hi, lo = _split(y)
        mean = _dot(hi, mean_bd) + _dot(lo, mean_bd)
        dev = y - mean
        var = _dot((dev * dev).astype(BF16), mean_bd)
        yn = dev * lax.rsqrt(var + LNX_EPS) * lw_ref[:, ln] + lb_ref[:, ln]
        kd = kd_ref[0, :, ln].astype(F32) + kd_ref[1, :, ln].astype(F32)
        rkd = (r_ref[:, ln].astype(F32) * kd * rk_ref[:, ln]).astype(BF16)
        bonus = _dot(rkd, ones_bd) * v_ref[:, ln].astype(F32)
        o_ref[:, ln] = ((yn + bonus) * g_ref[:, ln].astype(F32)).astype(o_ref.dtype)


def _rwkv_post(yf, yb, rkv, kd, v, g, r_k, lnx_w, lnx_b):
    n, d = yf.shape
    tm = _pick(n, (256, 128, 64, 32, 16, 8))
    tok = pl.BlockSpec((tm, d), lambda i: (i, 0))
    row = pl.BlockSpec((1, d), lambda i: (0, 0))
    return pl.pallas_call(
        functools.partial(_post_kernel, ngroups=d // PAIR),
        out_shape=jax.ShapeDtypeStruct((n, d), BF16),
        grid=(n // tm,),
        in_specs=[tok, tok, pl.BlockSpec((None, tm, d), lambda i: (0, i, 0)),
                  pl.BlockSpec((2, tm, d), lambda i: (0, i, 0)), tok, tok, row, row, row],
        out_specs=tok,
        compiler_params=_cparams(("parallel",)),
        name="rwkv_post",
    )(yf, yb, rkv, kd, v, g, r_k, lnx_w, lnx_b)


def _band_bias(tq, dil, nheads):
    nk = tq + 2 * BAND
    dist = jnp.abs(jnp.arange(nk, dtype=jnp.int32)[None, :] - BAND - jnp.arange(tq, dtype=jnp.int32)[:, None])
    slopes = jnp.exp2(-8.0 * jnp.arange(1, nheads + 1, dtype=F32) / nheads)
    bias = -slopes[:, None, None] * (dil * dist).astype(F32)[None] * LOG2E
    return jnp.where((dist <= BAND)[None], bias, NEG_BIG)


def _band_kernel(q_ref, kp_ref, kc_ref, kn_ref, vp_ref, vc_ref, vn_ref, bias_ref, o_ref, lse_ref,
                 *, tq, nheads, n1l, l1, l2):
    pos0, seq_len = _seq_pos(pl.program_id(0) * tq, n1l, l1, l2)
    prev_ok = pos0 > 0
    next_ok = pos0 + tq < seq_len
    col = lax.broadcasted_iota(jnp.int32, (1, tq + 2 * BAND), 1)
    in_seq = ((col >= BAND) | prev_ok) & ((col < tq + BAND) | next_ok)
    edge = jnp.where(in_seq, 0.0, NEG_BIG)
    head_lane = lax.broadcasted_iota(jnp.int32, (1, ATT_HEAD), 1)
    lse_all = jnp.zeros((tq, ATT_HEAD), F32)
    scale2 = ATT_HEAD ** -0.5 * LOG2E
    lanes = [slice(h * ATT_HEAD, (h + 1) * ATT_HEAD) for h in range(nheads)]
    for h0 in range(0, nheads, HEAD_BATCH):
        batch = range(h0, min(h0 + HEAD_BATCH, nheads))
        scores, probs, dens = {}, {}, {}
        for h in batch:
            keys = jnp.concatenate([kp_ref[:, lanes[h]], kc_ref[:, lanes[h]], kn_ref[:, lanes[h]]], axis=0)
            scores[h] = _dot_nt(q_ref[:, lanes[h]], keys) * scale2 + (bias_ref[h] + edge)
        for h in batch:
            m = jnp.max(scores[h], axis=-1, keepdims=True)
            probs[h] = jnp.exp2(scores[h] - m)
            dens[h] = jnp.sum(probs[h], axis=-1, keepdims=True)
            lse_all = lse_all + jnp.where(head_lane == h, (m + jnp.log2(dens[h])) * LN2, 0.0)
        for h in batch:
            vals = jnp.concatenate([vp_ref[:, lanes[h]], vc_ref[:, lanes[h]], vn_ref[:, lanes[h]]], axis=0)
            o_ref[:, lanes[h]] = (_dot(probs[h].astype(BF16), vals) / dens[h]).astype(o_ref.dtype)
    lse_ref[...] = lse_all


def _band_attention(qkv, group, geom, d):
    dil, nl, _ = qkv.shape
    n1, t1, t2 = geom
    n1l, l1, l2 = n1 // dil, t1 // dil, t2 // dil
    tq = _pick(math.gcd(l1, l2), (128, 64))
    hb = tq // BAND
    nhalo = nl // BAND
    nheads = d // ATT_HEAD

    def cur(s):
        return pl.BlockSpec((None, tq, d), lambda i, c, s=s: (c, i, s))

    def before(s):
        return pl.BlockSpec((None, BAND, d), lambda i, c, s=s: (c, jnp.maximum(i * hb - 1, 0), s))

    def after(s):
        return pl.BlockSpec((None, BAND, d), lambda i, c, s=s: (c, jnp.minimum((i + 1) * hb, nhalo - 1), s))

    return pl.pallas_call(
        functools.partial(_band_kernel, tq=tq, nheads=nheads, n1l=n1l, l1=l1, l2=l2),
        out_shape=(jax.ShapeDtypeStruct((dil, nl, d), BF16),
                   jax.ShapeDtypeStruct((dil, nl, ATT_HEAD), F32)),
        grid=(nl // tq, dil),
        in_specs=[cur(0), before(1), cur(1), after(1), before(2), cur(2), after(2),
                  pl.BlockSpec((nheads, tq, tq + 2 * BAND), lambda i, c: (0, 0, 0))],
        out_specs=(pl.BlockSpec((None, tq, d), lambda i, c: (c, i, 0)),
                   pl.BlockSpec((None, tq, ATT_HEAD), lambda i, c: (c, i, 0))),
        compiler_params=_cparams(("parallel", "parallel")),
        name=f"band_attention_g{group}",
    )(qkv, qkv, qkv, qkv, qkv, qkv, qkv, _band_bias(tq, dil, nheads))


def _combine_kernel(o0_ref, o1_ref, o2_ref, l0_ref, l1_ref, l2_ref, e_ref, out_ref, lse_scr, o_scr, *, dils):
    tm = out_ref.shape[0]

    def token_order(ref, scr, dil):
        if dil == 1:
            return ref[0].astype(F32)
        rows = tm // dil
        ngrp = ref.shape[2] // LANES
        for k in range(ngrp):
            for c in range(dil):
                scr[k, pl.ds(c, rows, stride=dil), :] = ref[c, :, k * LANES:(k + 1) * LANES].astype(F32)
        return jnp.concatenate([scr[k] for k in range(ngrp)], axis=1)

    lses = [token_order(l_ref, lse_scr.at[pl.ds(g, 1)], dil)
            for g, (l_ref, dil) in enumerate(zip((l0_ref, l1_ref, l2_ref), dils))]
    m = jnp.maximum(jnp.maximum(lses[0], lses[1]), lses[2])
    ws = [jnp.exp(l - m) for l in lses]
    tot = ws[0] + ws[1] + ws[2]
    e = e_ref[...]
    acc = None
    for w, o_ref, dil in zip(ws, (o0_ref, o1_ref, o2_ref), dils):
        hi, lo = _split(w / tot)
        term = (_dot(hi, e) + _dot(lo, e)) * token_order(o_ref, o_scr, dil)
        acc = term if acc is None else acc + term
    out_ref[...] = acc.astype(out_ref.dtype)


def _combine(os_, lses, d):
    dils = tuple(o.shape[0] for o in os_)
    n = os_[0].shape[0] * os_[0].shape[1]
    tm = _pick(n, (256,))
    head_of_lane = jnp.arange(d, dtype=jnp.int32) // ATT_HEAD
    expand = (jnp.arange(ATT_HEAD, dtype=jnp.int32)[:, None] == head_of_lane[None, :]).astype(BF16)

    def classes(dil, width):
        return pl.BlockSpec((dil, tm // dil, width), lambda i: (0, i, 0))

    return pl.pallas_call(
        functools.partial(_combine_kernel, dils=dils),
        out_shape=jax.ShapeDtypeStruct((n, d), BF16),
        grid=(n // tm,),
        in_specs=[classes(dil, d) for dil in dils] + [classes(dil, ATT_HEAD) for dil in dils]
                 + [pl.BlockSpec((ATT_HEAD, d), lambda i: (0, 0))],
        out_specs=pl.BlockSpec((tm, d), lambda i: (i, 0)),
        scratch_shapes=[pltpu.VMEM((len(dils), tm, LANES), F32), pltpu.VMEM((d // LANES, tm, LANES), F32)],
        compiler_params=_cparams(("parallel",)),
        name="attention_combine",
    )(*os_, *lses, expand)


def _rwkv_layer(x, gain, p, v_first, vres, geom):
    xs = _token_shift(x, gain, p["mu"], geom)
    rkv = _matmul_batched(xs, p["w_rkv"], 3, RKV_DTYPE)
    if v_first is None:
        v_first = rkv
    logw, kd, a, v, g = _rwkv_mid(xs, rkv, v_first, p, vres)
    maps = [_wkv_stage1(rkv, 0, logw, kd, v, rkv, 1, a, p["k_k"], z, reverse=(z == 1)) for z in range(2)]
    yf, yb = _wkv_stage2(maps[0], maps[1], geom)
    o = _rwkv_post(yf, yb, rkv, kd, v, g, p["r_k"], p["lnx_w"], p["lnx_b"])
    return _matmul_residual(o, p["w_o"], x), v_first


def _attention_layer(x, gain, w_qkv, w_o, geom):
    d = x.shape[1]
    assert all((window // 2) // dil == BAND for window, dil in DIL_PATTERNS)
    qkvs = _matmul_norm_classes(x, gain, w_qkv, [dil for _, dil in DIL_PATTERNS], BF16)
    outs, lses = zip(*[_band_attention(qkv, gi, geom, d) for gi, qkv in enumerate(qkvs)])
    return _matmul_residual(_combine(outs, lses, d), w_o, x)


def kernel(x_prompt, x_sample, ln1, ln2, ln_f, rw_mu, rw_w_rkv, rw_w0, rw_w1, rw_w2, rw_a0, rw_a1, rw_a2, rw_v0, rw_v1, rw_v2, rw_g1, rw_g2, rw_k_k, rw_k_a, rw_r_k, rw_lnx_w, rw_lnx_b, rw_w_o, at_w_qkv, at_w_o, ffn_w_gate, ffn_w_up, ffn_w_down):
    b1, t1, d = x_prompt.shape
    b2, t2, _ = x_sample.shape
    n1, n2 = b1 * t1, b2 * t2
    geom = (n1, t1, t2)
    depth = ln1.shape[0]
    assert d % PAIR == 0 and d % ATT_HEAD == 0
    max_dil = max(dil for _, dil in DIL_PATTERNS)
    assert t1 % (max_dil * BAND) == 0 and t2 % (max_dil * BAND) == 0

    x = jnp.concatenate([x_prompt.reshape(n1, d), x_sample.reshape(n2, d)], axis=0)
    bf = lambda w: w.astype(BF16)
    row = lambda w: w.reshape(1, d)

    v_first = None
    for i in range(depth):
        j = i // 2
        if i % 2 == 0:
            p = dict(mu=rw_mu[j], w_rkv=bf(rw_w_rkv[j]), w0=rw_w0[j], w1=bf(rw_w1[j]), w2=bf(rw_w2[j]),
                     a0=rw_a0[j], a1=bf(rw_a1[j]), a2=bf(rw_a2[j]), g1=bf(rw_g1[j]), g2=bf(rw_g2[j]),
                     k_k=row(rw_k_k[j]), k_a=row(rw_k_a[j]), r_k=rw_r_k[j].reshape(1, d),
                     lnx_w=row(rw_lnx_w[j]), lnx_b=row(rw_lnx_b[j]), w_o=bf(rw_w_o[j]))
            vres = None if j == 0 else (row(rw_v0[j - 1]), bf(rw_v1[j - 1]), bf(rw_v2[j - 1]))
            x, v_first = _rwkv_layer(x, row(ln1[i]), p, v_first, vres, geom)
        else:
            w_qkv = bf(at_w_qkv[j]).reshape(d, -1)
            x = _attention_layer(x, row(ln1[i]), w_qkv, bf(at_w_o[j]), geom)
        x = _ffn(x, row(ln2[i]), bf(ffn_w_gate[i]), bf(ffn_w_up[i]), bf(ffn_w_down[i]),
                 row(ln_f), final_norm=(i == depth - 1))
    return x[:n1].reshape(b1, t1, d), x[n1:].reshape(b2, t2, d)
```

```python
import functools
import math

import numpy as np

import jax
import jax.numpy as jnp
from jax import lax
from jax.experimental import pallas as pl
from jax.experimental.pallas import tpu as pltpu

F32 = jnp.float32
BF16 = jnp.bfloat16

RW_HEAD = 64
PAIR = 2 * RW_HEAD
ATT_HEAD = 128
LANES = 128
ROW_STRIDE = 4
DIL_PATTERNS = ((128, 1), (512, 4), (2048, 16))
BAND = 64
LNX_EPS = 64e-5
RMS_EPS = 1e-6
CHUNK = 64
NEG_BIG = -1e30
LOG2E = math.log2(math.e)
LN2 = math.log(2.0)
EXP_NEG_HALF = math.exp(-0.5)
RKV_DTYPE = BF16
WKV_MAP_DTYPE = BF16
WKV_OUT_DTYPE = BF16
WKV_SCAN_CHUNKS = 2
BAND_TILES = 2
HEAD_BATCH = 8
WKV_UNROLL = 4
V7X_VMEM_LIMIT_BYTES = 56 * 1024 * 1024


def _pick(n, candidates):
    for c in candidates:
        if n % c == 0:
            return c
    raise ValueError(f"no tile in {candidates} divides {n}")


def _cparams(sem):
    return pltpu.CompilerParams(dimension_semantics=sem, vmem_limit_bytes=V7X_VMEM_LIMIT_BYTES)


def _dot(a, b):
    return jnp.dot(a, b, preferred_element_type=F32)


def _dot_nt(a, b):
    return lax.dot_general(a, b, (((1,), (1,)), ((), ())), preferred_element_type=F32)


def _dot_tn(a, b):
    return lax.dot_general(a, b, (((0,), (0,)), ((), ())), preferred_element_type=F32)


def _split(x):
    hi = x.astype(BF16)
    lo = (x - hi.astype(F32)).astype(BF16)
    return hi, lo


def _sigmoid(x):
    return 1.0 / (1.0 + jnp.exp2(x * -LOG2E))


def _rms(x, g):
    return x * lax.rsqrt(jnp.mean(x * x, axis=-1, keepdims=True) + RMS_EPS) * g


def _seq_pos(idx, n1, t1, t2):
    first = idx < n1
    pos = jnp.where(first, lax.rem(idx, t1), lax.rem(idx - n1, t2))
    return pos, jnp.where(first, t1, t2)


def _mm_kernel(a_ref, w_ref, o_ref):
    o_ref[...] = _dot(a_ref[...], w_ref[...]).astype(o_ref.dtype)


def _matmul_batched(a, w, groups, out_dtype):
    _, m, k = a.shape
    n = w.shape[2]
    tm = _pick(m, (1024, 512, 256, 128))
    tn = _pick(n, (1024, 512, 256, 128))
    return pl.pallas_call(
        _mm_kernel,
        out_shape=jax.ShapeDtypeStruct((groups, m, n), out_dtype),
        grid=(groups, m // tm, n // tn),
        in_specs=[pl.BlockSpec((None, tm, k), lambda g, i, j: (g, i, 0)),
                  pl.BlockSpec((None, k, tn), lambda g, i, j: (g, 0, j))],
        out_specs=pl.BlockSpec((None, tm, tn), lambda g, i, j: (g, i, j)),
        compiler_params=_cparams(("parallel", "parallel", "arbitrary")),
        name="mm_batched",
    )(a, w)


def _class_segment(c, dil):
    return c if dil <= ROW_STRIDE else ROW_STRIDE * (c % ROW_STRIDE) + c // ROW_STRIDE


def _store_rows_by_class(hk, h_ref, cols, dil, a_ref, b_ref):
    tm = hk.shape[0]
    rows = tm // dil
    if dil == 1:
        h_ref[:, cols] = hk.astype(BF16)
        return
    a_ref[...] = hk
    if dil == ROW_STRIDE:
        for c0 in range(ROW_STRIDE):
            h_ref[c0 * rows:(c0 + 1) * rows, cols] = a_ref[pl.ds(c0, rows, stride=ROW_STRIDE), :].astype(BF16)
        return
    quarter = tm // ROW_STRIDE
    for c0 in range(ROW_STRIDE):
        b_ref[c0 * quarter:(c0 + 1) * quarter, :] = a_ref[pl.ds(c0, quarter, stride=ROW_STRIDE), :]
    for c0 in range(ROW_STRIDE):
        for b in range(ROW_STRIDE):
            seg = ROW_STRIDE * c0 + b
            h_ref[seg * rows:(seg + 1) * rows, cols] = b_ref[
                pl.ds(c0 * quarter + b, rows, stride=ROW_STRIDE), :].astype(BF16)


def _mm_qkv_kernel(x_ref, g_ref, w_ref, *refs, dils):
    ng = len(dils)
    o_refs, h_ref, a_ref, b_ref = refs[:ng], refs[ng], refs[ng + 1], refs[ng + 2]
    tm, k = x_ref.shape
    group = pl.program_id(1)

    @pl.when((group == 0) & (pl.program_id(2) == 0))
    def _():
        x = x_ref[...]
        inv = lax.rsqrt(jnp.mean(x * x, axis=-1, keepdims=True) + RMS_EPS)
        for kk in range(k // LANES):
            cols = slice(kk * LANES, (kk + 1) * LANES)
            hk = x_ref[:, cols] * inv * g_ref[:, cols]
            for gi, dil in enumerate(dils):
                _store_rows_by_class(hk, h_ref.at[gi], cols, dil, a_ref, b_ref)

    for gi, (o_ref, dil) in enumerate(zip(o_refs, dils)):
        @pl.when(group == gi)
        def _(gi=gi, o_ref=o_ref, dil=dil):
            rows = tm // dil
            res = _dot(h_ref[gi], w_ref[...])
            for c in range(dil):
                seg = _class_segment(c, dil)
                o_ref[c] = res[seg * rows:(seg + 1) * rows].astype(o_ref.dtype)


def _matmul_norm_classes(x, gain, w, dils, out_dtype):
    m, k = x.shape
    ngroups = len(dils)
    ng = w.shape[1] // ngroups
    tm = _pick(m, (1024, 512, 256))
    tn = _pick(ng, (1024, 512, 256, 128))
    nj = ng // tn
    assert all(dil in (1, ROW_STRIDE, ROW_STRIDE ** 2) for dil in dils)

    def out_spec(gi, dil):
        def index(i, g, j):
            return (0, i, jnp.where(g < gi, 0, jnp.where(g == gi, j, nj - 1)))
        return pl.BlockSpec((dil, tm // dil, tn), index)

    return pl.pallas_call(
        functools.partial(_mm_qkv_kernel, dils=tuple(dils)),
        out_shape=tuple(jax.ShapeDtypeStruct((dil, m // dil, ng), out_dtype) for dil in dils),
        grid=(m // tm, ngroups, nj),
        in_specs=[pl.BlockSpec((tm, k), lambda i, g, j: (i, 0)),
                  pl.BlockSpec((1, k), lambda i, g, j: (0, 0)),
                  pl.BlockSpec((k, tn), lambda i, g, j: (0, g * nj + j))],
        out_specs=tuple(out_spec(gi, dil) for gi, dil in enumerate(dils)),
        scratch_shapes=[pltpu.VMEM((ngroups, tm, k), BF16),
                        pltpu.VMEM((tm, LANES), F32), pltpu.VMEM((tm, LANES), F32)],
        compiler_params=_cparams(("parallel", "arbitrary", "arbitrary")),
        name="mm_norm_qkv",
    )(x, gain, w)


def _mm_res_kernel(a_ref, w_ref, r_ref, o_ref):
    o_ref[...] = r_ref[...] + _dot(a_ref[...], w_ref[...])


def _matmul_residual(a, w, res):
    m, k = a.shape
    n = w.shape[1]
    tm = _pick(m, (1024, 512, 256, 128))
    tn = _pick(n, (1024, 512, 256, 128))
    return pl.pallas_call(
        _mm_res_kernel,
        out_shape=jax.ShapeDtypeStruct((m, n), F32),
        grid=(m // tm, n // tn),
        in_specs=[pl.BlockSpec((tm, k), lambda i, j: (i, 0)),
                  pl.BlockSpec((k, tn), lambda i, j: (0, j)),
                  pl.BlockSpec((tm, tn), lambda i, j: (i, j))],
        out_specs=pl.BlockSpec((tm, tn), lambda i, j: (i, j)),
        input_output_aliases={2: 0},
        compiler_params=_cparams(("parallel", "arbitrary")),
        name="mm_residual",
    )(a, w, res)


def _ffn_kernel(x_ref, g_ref, wg_ref, wu_ref, wd_ref, gf_ref, o_ref, h_ref, acc_ref, *, final_norm):
    j = pl.program_id(1)

    @pl.when(j == 0)
    def _():
        x = x_ref[...]
        h_ref[...] = _rms(x, g_ref[...]).astype(BF16)
        acc_ref[...] = x

    h = h_ref[...]
    gate = _dot(h, wg_ref[...])
    up = _dot(h, wu_ref[...])
    act = (gate * _sigmoid(gate) * up).astype(BF16)
    acc_ref[...] += _dot(act, wd_ref[...])

    @pl.when(j == pl.num_programs(1) - 1)
    def _():
        y = acc_ref[...]
        if final_norm:
            y = _rms(y, gf_ref[...])
        o_ref[...] = y


def _ffn(x, gain, wg, wu, wd, final_gain, final_norm):
    m, d = x.shape
    ff = wg.shape[1]
    tm = _pick(m, (512, 256, 128))
    tf = _pick(ff, (512, 256, 128))
    return pl.pallas_call(
        functools.partial(_ffn_kernel, final_norm=final_norm),
        out_shape=jax.ShapeDtypeStruct((m, d), F32),
        grid=(m // tm, ff // tf),
        in_specs=[pl.BlockSpec((tm, d), lambda i, j: (i, 0)),
                  pl.BlockSpec((1, d), lambda i, j: (0, 0)),
                  pl.BlockSpec((d, tf), lambda i, j: (0, j)),
                  pl.BlockSpec((d, tf), lambda i, j: (0, j)),
                  pl.BlockSpec((tf, d), lambda i, j: (j, 0)),
                  pl.BlockSpec((1, d), lambda i, j: (0, 0))],
        out_specs=pl.BlockSpec((tm, d), lambda i, j: (i, 0)),
        scratch_shapes=[pltpu.VMEM((tm, d), BF16), pltpu.VMEM((tm, d), F32)],
        input_output_aliases={0: 0},
        compiler_params=_cparams(("parallel", "arbitrary")),
        name="ffn",
    )(x, gain, wg, wu, wd, final_gain)


def _shift_kernel(x_ref, xp_ref, xn_ref, g_ref, mu_ref, o_ref, *, tt, n1, t1, t2):
    pos0, seq_len = _seq_pos(pl.program_id(0) * tt, n1, t1, t2)

    def inv_rms(x):
        return lax.rsqrt(jnp.mean(x * x, axis=-1, keepdims=True) + RMS_EPS)

    inv = inv_rms(x_ref[...])
    inv_before = jnp.where(pos0 == 0, 0.0, inv_rms(xp_ref[7:8, :]))
    inv_after = jnp.where(pos0 + tt == seq_len, 0.0, inv_rms(xn_ref[0:1, :]))
    row = lax.broadcasted_iota(jnp.int32, (tt, 1), 0)
    for k in range(x_ref.shape[1] // LANES):
        cols = slice(k * LANES, (k + 1) * LANES)
        g = g_ref[:, cols]
        h = x_ref[:, cols] * inv * g
        h_prev = jnp.where(row == 0, xp_ref[7:8, cols] * inv_before * g, pltpu.roll(h, 1, 0))
        h_next = jnp.where(row == tt - 1, xn_ref[0:1, cols] * inv_after * g, pltpu.roll(h, tt - 1, 0))
        xx = 0.5 * (h_prev + h_next) - h
        for c in range(6):
            o_ref[c, :, cols] = (h + xx * mu_ref[c:c + 1, cols]).astype(BF16)


def _token_shift(x, gain, mu, geom):
    n, d = x.shape
    n1, t1, t2 = geom
    tt = _pick(math.gcd(t1, t2), (256, 128, 64, 32, 16, 8))
    nb8 = n // 8
    return pl.pallas_call(
        functools.partial(_shift_kernel, tt=tt, n1=n1, t1=t1, t2=t2),
        out_shape=jax.ShapeDtypeStruct((6, n, d), BF16),
        grid=(n // tt,),
        in_specs=[pl.BlockSpec((tt, d), lambda i: (i, 0)),
                  pl.BlockSpec((8, d), lambda i: (jnp.maximum(i * (tt // 8) - 1, 0), 0)),
                  pl.BlockSpec((8, d), lambda i: (jnp.minimum((i + 1) * (tt // 8), nb8 - 1), 0)),
                  pl.BlockSpec((1, d), lambda i: (0, 0)),
                  pl.BlockSpec((6, d), lambda i: (0, 0))],
        out_specs=pl.BlockSpec((6, tt, d), lambda i: (0, i, 0)),
        compiler_params=_cparams(("parallel",)),
        name="token_shift",
    )(x, x, x, gain, mu)


def _mid_kernel(*refs, has_vres):
    if has_vres:
        (xv_ref, xw_ref, xa_ref, xg_ref, k_ref, v_ref, vf_ref,
         w0_ref, w1_ref, w2_ref, a0_ref, a1_ref, a2_ref, g1_ref, g2_ref, ka_ref,
         v0_ref, v1_ref, v2_ref,
         logw_ref, kd_ref, a_ref, vo_ref, g_ref) = refs
    else:
        (xv_ref, xw_ref, xa_ref, xg_ref, k_ref, v_ref,
         w0_ref, w1_ref, w2_ref, a0_ref, a1_ref, a2_ref, g1_ref, g2_ref, ka_ref,
         logw_ref, kd_ref, a_ref, vo_ref, g_ref) = refs
    xw = xw_ref[...]
    xa = xa_ref[...]
    k = k_ref[...].astype(F32)
    kka = k * ka_ref[...]
    k_rest = k - kka
    for z in range(2):
        lora = _dot(jnp.tanh(_dot(xw, w1_ref[z])).astype(BF16), w2_ref[z])
        logw_ref[z] = -EXP_NEG_HALF * _sigmoid(w0_ref[z:z + 1, :] + lora)
        aa = _sigmoid(a0_ref[z:z + 1, :] + _dot(_dot(xa, a1_ref[z]).astype(BF16), a2_ref[z]))
        a_ref[z] = aa.astype(a_ref.dtype)
        kd_ref[z] = (k_rest + kka * aa).astype(kd_ref.dtype)
    v = v_ref[...].astype(F32)
    if has_vres:
        gate = _sigmoid(v0_ref[...] + _dot(_dot(xv_ref[...], v1_ref[...]).astype(BF16), v2_ref[...]))
        v = v + (vf_ref[...].astype(F32) - v) * gate
    vo_ref[...] = v.astype(vo_ref.dtype)
    g_ref[...] = _dot(_sigmoid(_dot(xg_ref[...], g1_ref[...])).astype(BF16), g2_ref[...]).astype(g_ref.dtype)


def _rwkv_mid(xs, rkv, v_first, p, vres):
    _, n, d = xs.shape
    tm = _pick(n, (128, 64, 32, 16, 8))
    has_vres = vres is not None

    def slab(c):
        return pl.BlockSpec((None, tm, d), lambda i, c=c: (c, i, 0))

    def whole(a):
        nd = a.ndim
        return pl.BlockSpec(a.shape, lambda i, nd=nd: (0,) * nd)

    tok = pl.BlockSpec((tm, d), lambda i: (i, 0))
    tok2 = pl.BlockSpec((2, tm, d), lambda i: (0, i, 0))
    weights = [p["w0"], p["w1"], p["w2"], p["a0"], p["a1"], p["a2"], p["g1"], p["g2"], p["k_a"]]
    args = [xs, xs, xs, xs, rkv, rkv]
    specs = [slab(2), slab(3), slab(4), slab(5), slab(1), slab(2)]
    if has_vres:
        args.append(v_first)
        specs.append(slab(2))
        weights += list(vres)
    args += weights
    specs += [whole(a) for a in weights]
    return pl.pallas_call(
        functools.partial(_mid_kernel, has_vres=has_vres),
        out_shape=(jax.ShapeDtypeStruct((2, n, d), F32),
                   jax.ShapeDtypeStruct((2, n, d), BF16),
                   jax.ShapeDtypeStruct((2, n, d), BF16),
                   jax.ShapeDtypeStruct((n, d), BF16),
                   jax.ShapeDtypeStruct((n, d), BF16)),
        grid=(n // tm,),
        in_specs=specs,
        out_specs=(tok2, tok2, tok2, tok, tok),
        compiler_params=_cparams(("parallel",)),
        name="rwkv_mid",
    )(*args)


def _pair_rows(x, head0):
    zero = jnp.zeros_like(x)
    return jnp.concatenate([jnp.where(head0, x, zero), jnp.where(head0, zero, x)], axis=0)


def _wkv1_kernel(r_ref, lw_ref, kd_ref, v_ref, k_ref, a_ref, kk_ref,
                 rp_ref, y0_ref, g_ref, h_ref, *, nchunk, npair, unroll, reverse):
    lane = lax.broadcasted_iota(jnp.int32, (CHUNK, PAIR), 1)
    head0 = lane < RW_HEAD
    col = jnp.where(head0, lane, lane - RW_HEAD)
    row = lax.broadcasted_iota(jnp.int32, (CHUNK, PAIR), 0)
    strict = (row < col) if reverse else (row > col)
    incl = (row <= col) if reverse else (row >= col)
    eye = row == col
    r2 = lax.broadcasted_iota(jnp.int32, (PAIR, PAIR), 0)
    q2 = lax.broadcasted_iota(jnp.int32, (PAIR, PAIR), 1)
    ones_bd = ((r2 < RW_HEAD) == (q2 < RW_HEAD)).astype(BF16)
    t1 = lax.broadcasted_iota(jnp.int32, (CHUNK, CHUNK), 0)
    s1 = lax.broadcasted_iota(jnp.int32, (CHUNK, CHUNK), 1)
    tri = ((t1 <= s1) if reverse else (t1 >= s1)).astype(BF16)
    tri2 = jnp.concatenate([tri, tri], axis=1)
    last = 0 if reverse else CHUNK - 1
    bd = functools.partial(_pair_rows, head0=head0)

    def prepare(rows, ln, lw, cum):
        kk = k_ref[rows, ln].astype(F32) * kk_ref[:, ln]
        ss = _dot((kk * kk).astype(BF16), ones_bd)
        kk = kk / jnp.maximum(jnp.sqrt(ss), 1e-12)
        b = kk * a_ref[rows, ln].astype(F32)
        kd = kd_ref[rows, ln].astype(F32)
        tot = cum[last:last + 1, :]
        e_neg = jnp.exp(-cum)
        e_tail = jnp.exp(tot - cum)
        w = dict(tot=tot)
        w["at"] = (-kk * jnp.exp(cum - lw)).astype(BF16)
        w["rt"] = r_ref[rows, ln].astype(F32) * jnp.exp(cum)
        w["bk_in"] = jnp.concatenate([bd((b * e_neg).astype(BF16)), bd((kd * e_neg).astype(BF16))], axis=0)
        w["bk_out"] = jnp.concatenate([(b * e_tail).astype(BF16), (kd * e_tail).astype(BF16)], axis=0)
        w["v"] = v_ref[rows, ln]
        return w

    def chunk_body(ci, carry):
        items = []
        for u in range(unroll):
            rows = pl.ds(pl.multiple_of((ci * unroll + u) * CHUNK, CHUNK), CHUNK)
            lw_all = lw_ref[rows, :]
            hi, lo = _split(lw_all)
            cum_all = _dot(tri2, jnp.concatenate([hi, lo], axis=0))
            for p in range(npair):
                ln = slice(p * PAIR, (p + 1) * PAIR)
                w = prepare(rows, ln, lw_all[:, ln], cum_all[:, ln])
                w["rows"], w["ln"] = rows, ln
                items.append(w)
        for w in items:
            m1 = _dot_nt(jnp.concatenate([w["at"], w["rt"].astype(BF16)], axis=0), w["bk_in"])
            a_ab = jnp.where(strict, m1[:CHUNK, :PAIR], 0.0)
            w["a_kv"] = jnp.concatenate([jnp.where(strict, m1[:CHUNK, PAIR:], 0.0),
                                         jnp.where(incl, m1[CHUNK:, PAIR:], 0.0)], axis=0).astype(BF16)
            w["a_rb"] = jnp.where(incl, m1[CHUNK:, :PAIR], 0.0).astype(BF16)
            w["pw"] = a_ab.astype(BF16)
            w["inv"] = jnp.where(eye, 1.0, a_ab)
        for w in items:
            w["pw"] = _dot(w["pw"], bd(w["pw"])).astype(BF16)
        for f in range(1, int(math.log2(CHUNK)) - 1):
            for w in items:
                both = _dot(jnp.concatenate([w["pw"], w["inv"].astype(BF16)], axis=0), bd(w["pw"]))
                w["pw"] = both[:CHUNK].astype(BF16)
                w["inv"] = w["inv"] + both[CHUNK:]
        for w in items:
            w["inv"] = w["inv"] + _dot(w["inv"].astype(BF16), bd(w["pw"]))
        for w in items:
            w["av"] = _dot(w["a_kv"], bd(w["v"]))
        for w in items:
            rhs = jnp.concatenate([bd(w["at"]), bd(w["av"][:CHUNK].astype(BF16))], axis=1)
            w["wu"] = _dot(w["inv"].astype(BF16), rhs).astype(BF16)
        for w in items:
            rows, ln = w["rows"], w["ln"]
            wu = w["wu"]
            rw = _dot(w["a_rb"], jnp.concatenate([bd(wu[:, :PAIR]), bd(wu[:, PAIR:])], axis=1))
            rp_ref[rows, ln] = (w["rt"] + rw[:, :PAIR]).astype(rp_ref.dtype)
            y0_ref[rows, ln] = (rw[:, PAIR:] + w["av"][CHUNK:]).astype(y0_ref.dtype)
            gfull = _dot_tn(w["bk_out"][:CHUNK], wu[:, :PAIR])
            gdiag = jnp.where(eye, jnp.exp(w["tot"]), 0.0)
            g_ref[rows, ln] = (jnp.where(head0, gfull[:CHUNK], gfull[CHUNK:]) + gdiag).astype(g_ref.dtype)
            hfull = _dot_tn(w["bk_out"], jnp.concatenate([wu[:, PAIR:], w["v"]], axis=0))
            h_ref[rows, ln] = jnp.where(head0, hfull[:CHUNK], hfull[CHUNK:]).astype(h_ref.dtype)
        return carry

    lax.fori_loop(0, nchunk // unroll, chunk_body, 0)


def _wkv_stage1(r3, r_slab, logw, kd, v, k3, k_slab, a, kk_scale, z, reverse):
    n, d = v.shape
    npair = _pick(d // PAIR, (4, 2, 1))
    lanes = npair * PAIR
    tb = _pick(n, (512, 256))

    def slab(c):
        return pl.BlockSpec((None, tb, lanes), lambda i, j, c=c: (c, i, j))

    tok = pl.BlockSpec((tb, lanes), lambda i, j: (i, j))
    return pl.pallas_call(
        functools.partial(_wkv1_kernel, nchunk=tb // CHUNK, npair=npair, unroll=WKV_UNROLL, reverse=reverse),
        out_shape=(jax.ShapeDtypeStruct((n, d), BF16),
                   jax.ShapeDtypeStruct((n, d), WKV_MAP_DTYPE),
                   jax.ShapeDtypeStruct((n, d), BF16),
                   jax.ShapeDtypeStruct((n, d), WKV_MAP_DTYPE)),
        grid=(n // tb, d // lanes),
        in_specs=[slab(r_slab), slab(z), slab(z), tok, slab(k_slab), slab(z),
                  pl.BlockSpec((1, lanes), lambda i, j: (0, j))],
        out_specs=(tok, tok, tok, tok),
        compiler_params=_cparams(("parallel", "parallel")),
        name="wkv_stage1_rev" if reverse else "wkv_stage1_fwd",
    )(r3, logw, kd, v, k3, a, kk_scale)


def _wkv2_kernel(rpf_ref, y0f_ref, gf_ref, hf_ref, rpb_ref, y0b_ref, gb_ref, hb_ref,
                 yf_ref, yb_ref, stf_ref, stb_ref, *, npairs, nblocks, sub, n1, t1, t2):
    i = pl.program_id(0)

    @pl.when(i == 0)
    def _():
        stf_ref[...] = jnp.zeros_like(stf_ref)
        stb_ref[...] = jnp.zeros_like(stb_ref)

    lane = lax.broadcasted_iota(jnp.int32, (CHUNK, PAIR), 1)
    head0 = lane < RW_HEAD
    bd = functools.partial(_pair_rows, head0=head0)
    fwd_refs = (rpf_ref, y0f_ref, gf_ref, hf_ref, yf_ref, stf_ref)
    bwd_refs = (rpb_ref, y0b_ref, gb_ref, hb_ref, yb_ref, stb_ref)
    for s in range(sub):
        cf, cb = s, sub - 1 - s
        pos_f, _ = _seq_pos((i * sub + cf) * CHUNK, n1, t1, t2)
        pos_b, len_b = _seq_pos(((nblocks - 1 - i) * sub + cb) * CHUNK, n1, t1, t2)
        dirs = ((fwd_refs, pl.ds(cf * CHUNK, CHUNK), pos_f == 0),
                (bwd_refs, pl.ds(cb * CHUNK, CHUNK), pos_b + CHUNK == len_b))
        items = [(refs, rows, fresh, slice(p * PAIR, (p + 1) * PAIR))
                 for p in range(npairs) for refs, rows, fresh in dirs]
        results = []
        for (rp_ref, _, g_ref, _, _, st_ref), rows, fresh, ln in items:
            state = jnp.where(fresh, 0.0, st_ref[:, ln])
            hi, lo = _split(state)
            lhs = jnp.concatenate([rp_ref[rows, ln], g_ref[rows, ln]], axis=0)
            results.append(_dot(jnp.concatenate([lhs, lhs], axis=1), jnp.concatenate([bd(hi), bd(lo)], axis=0)))
        for ((_, y0_ref, _, h_ref, y_ref, st_ref), rows, _, ln), res in zip(items, results):
            y_ref[rows, ln] = (y0_ref[rows, ln].astype(F32) + res[:CHUNK]).astype(y_ref.dtype)
            st_ref[:, ln] = res[CHUNK:] + h_ref[rows, ln].astype(F32)


def _wkv_stage2(fwd, bwd, geom):
    n, d = fwd[0].shape
    n1, t1, t2 = geom
    sub = _pick(n // CHUNK, (WKV_SCAN_CHUNKS, 1))
    nblocks = n // (sub * CHUNK)
    tok_f = pl.BlockSpec((sub * CHUNK, d), lambda i: (i, 0))
    tok_b = pl.BlockSpec((sub * CHUNK, d), lambda i: (nblocks - 1 - i, 0))
    return pl.pallas_call(
        functools.partial(_wkv2_kernel, npairs=d // PAIR, nblocks=nblocks, sub=sub, n1=n1, t1=t1, t2=t2),
        out_shape=(jax.ShapeDtypeStruct((n, d), WKV_OUT_DTYPE), jax.ShapeDtypeStruct((n, d), WKV_OUT_DTYPE)),
        grid=(nblocks,),
        in_specs=[tok_f] * 4 + [tok_b] * 4,
        out_specs=(tok_f, tok_b),
        scratch_shapes=[pltpu.VMEM((CHUNK, d), F32), pltpu.VMEM((CHUNK, d), F32)],
        compiler_params=_cparams(("arbitrary",)),
        name="wkv_stage2",
    )(*fwd, *bwd)


def _post_kernel(yf_ref, yb_ref, r_ref, kd_ref, v_ref, g_ref, rk_ref, lw_ref, lb_ref, o_ref, *, ngroups):
    r2 = lax.broadcasted_iota(jnp.int32, (PAIR, PAIR), 0)
    q2 = lax.broadcasted_iota(jnp.int32, (PAIR, PAIR), 1)
    same = (r2 < RW_HEAD) == (q2 < RW_HEAD)
    ones_bd = same.astype(BF16)
    mean_bd = (same.astype(F32) * (1.0 / RW_HEAD)).astype(BF16)
    for p in range(ngroups):
        ln = slice(p * PAIR, (p + 1) * PAIR)
        y = yf_ref[:, ln].astype(F32) + yb_ref[:, ln].astype(F32)
        hi, lo = _split(y)
        mean = _dot(hi, mean_bd) + _dot(lo, mean_bd)
        dev = y - mean
        var = _dot((dev * dev).astype(BF16), mean_bd)
        yn = dev * lax.rsqrt(var + LNX_EPS) * lw_ref[:, ln] + lb_ref[:, ln]
        kd = kd_ref[0, :, ln].astype(F32) + kd_ref[1, :, ln].astype(F32)
        rkd = (r_ref[:, ln].astype(F32) * kd * rk_ref[:, ln]).astype(BF16)
        bonus = _dot(rkd, ones_bd) * v_ref[:, ln].astype(F32)
        o_ref[:, ln] = ((yn + bonus) * g_ref[:, ln].astype(F32)).astype(o_ref.dtype)


def _rwkv_post(yf, yb, rkv, kd, v, g, r_k, lnx_w, lnx_b):
    n, d = yf.shape
    tm = _pick(n, (256, 128, 64, 32, 16, 8))
    tok = pl.BlockSpec((tm, d), lambda i: (i, 0))
    row = pl.BlockSpec((1, d), lambda i: (0, 0))
    return pl.pallas_call(
        functools.partial(_post_kernel, ngroups=d // PAIR),
        out_shape=jax.ShapeDtypeStruct((n, d), BF16),
        grid=(n // tm,),
        in_specs=[tok, tok, pl.BlockSpec((None, tm, d), lambda i: (0, i, 0)),
                  pl.BlockSpec((2, tm, d), lambda i: (0, i, 0)), tok, tok, row, row, row],
        out_specs=tok,
        compiler_params=_cparams(("parallel",)),
        name="rwkv_post",
    )(yf, yb, rkv, kd, v, g, r_k, lnx_w, lnx_b)


def _band_bias(tq, dil, nheads):
    nk = tq + 2 * BAND
    dist = np.abs(np.arange(nk)[None, :] - BAND - np.arange(tq)[:, None])
    slopes = np.exp2(-8.0 * np.arange(1, nheads + 1) / nheads)
    bias = -slopes[:, None, None] * (dil * dist)[None] * LOG2E
    return jnp.asarray(np.where((dist <= BAND)[None], bias, NEG_BIG), dtype=F32)


def _band_kernel(q_ref, kp_ref, kc_ref, kn_ref, vp_ref, vc_ref, vn_ref, bias_ref, o_ref, lse_ref,
                 *, tq, nsub, nheads, n1l, l1, l2):
    tb = tq * nsub
    pos0, seq_len = _seq_pos(pl.program_id(0) * tb, n1l, l1, l2)
    nk = tq + 2 * BAND
    col = lax.broadcasted_iota(jnp.int32, (1, nk), 1)
    head_lane = lax.broadcasted_iota(jnp.int32, (1, ATT_HEAD), 1)
    scale2 = ATT_HEAD ** -0.5 * LOG2E
    lanes = [slice(h * ATT_HEAD, (h + 1) * ATT_HEAD) for h in range(nheads)]

    def window(p_ref, c_ref, n_ref, sub, ln):
        lo, hi = sub * tq - BAND, sub * tq + tq + BAND
        parts = []
        if lo < 0:
            parts.append(p_ref[:, ln])
        parts.append(c_ref[max(lo, 0):min(hi, tb), ln])
        if hi > tb:
            parts.append(n_ref[:, ln])
        return jnp.concatenate(parts, axis=0) if len(parts) > 1 else parts[0]

    edges = []
    for sub in range(nsub):
        ok = jnp.full((1, nk), True)
        if sub == 0:
            ok = ok & ((col >= BAND) | (pos0 > 0))
        if sub == nsub - 1:
            ok = ok & ((col < tq + BAND) | (pos0 + tb < seq_len))
        edges.append(jnp.where(ok, 0.0, NEG_BIG))
    lse_all = [jnp.zeros((tq, ATT_HEAD), F32) for _ in range(nsub)]
    items = [(sub, h) for sub in range(nsub) for h in range(nheads)]
    for i0 in range(0, len(items), HEAD_BATCH):
        batch = items[i0:i0 + HEAD_BATCH]
        scores, probs, dens = {}, {}, {}
        for sub, h in batch:
            q = q_ref[sub * tq:(sub + 1) * tq, lanes[h]]
            keys = window(kp_ref, kc_ref, kn_ref, sub, lanes[h])
            scores[sub, h] = _dot_nt(q, keys) * scale2 + (bias_ref[h] + edges[sub])
        for sub, h in batch:
            m = jnp.max(scores[sub, h], axis=-1, keepdims=True)
            probs[sub, h] = jnp.exp2(scores[sub, h] - m)
            dens[sub, h] = jnp.sum(probs[sub, h], axis=-1, keepdims=True)
            lse_all[sub] = lse_all[sub] + jnp.where(head_lane == h, (m + jnp.log2(dens[sub, h])) * LN2, 0.0)
        for sub, h in batch:
            vals = window(vp_ref, vc_ref, vn_ref, sub, lanes[h])
            out = _dot(probs[sub, h].astype(BF16), vals) / dens[sub, h]
            o_ref[sub * tq:(sub + 1) * tq, lanes[h]] = out.astype(o_ref.dtype)
    for sub in range(nsub):
        lse_ref[sub * tq:(sub + 1) * tq, :] = lse_all[sub]


def _band_attention(qkv, group, geom, d):
    dil, nl, _ = qkv.shape
    n1, t1, t2 = geom
    n1l, l1, l2 = n1 // dil, t1 // dil, t2 // dil
    tq = _pick(math.gcd(l1, l2), (128, 64))
    nsub = _pick(math.gcd(l1, l2) // tq, (BAND_TILES, 1))
    tb = tq * nsub
    hb = tb // BAND
    nhalo = nl // BAND
    nheads = d // ATT_HEAD

    def cur(s):
        return pl.BlockSpec((None, tb, d), lambda i, c, s=s: (c, i, s))

    def before(s):
        return pl.BlockSpec((None, BAND, d), lambda i, c, s=s: (c, jnp.maximum(i * hb - 1, 0), s))

    def after(s):
        return pl.BlockSpec((None, BAND, d), lambda i, c, s=s: (c, jnp.minimum((i + 1) * hb, nhalo - 1), s))

    return pl.pallas_call(
        functools.partial(_band_kernel, tq=tq, nsub=nsub, nheads=nheads, n1l=n1l, l1=l1, l2=l2),
        out_shape=(jax.ShapeDtypeStruct((dil, nl, d), BF16),
                   jax.ShapeDtypeStruct((dil, nl, ATT_HEAD), F32)),
        grid=(nl // tb, dil),
        in_specs=[cur(0), before(1), cur(1), after(1), before(2), cur(2), after(2),
                  pl.BlockSpec((nheads, tq, tq + 2 * BAND), lambda i, c: (0, 0, 0))],
        out_specs=(pl.BlockSpec((None, tb, d), lambda i, c: (c, i, 0)),
                   pl.BlockSpec((None, tb, ATT_HEAD), lambda i, c: (c, i, 0))),
        compiler_params=_cparams(("parallel", "parallel")),
        name=f"band_attention_g{group}",
    )(qkv, qkv, qkv, qkv, qkv, qkv, qkv, _band_bias(tq, dil, nheads))


def _combine_kernel(o0_ref, o1_ref, o2_ref, l0_ref, l1_ref, l2_ref, e_ref, out_ref, lse_scr, o_scr, *, dils):
    tm = out_ref.shape[0]

    def token_order(ref, scr, dil):
        if dil == 1:
            return ref[0].astype(F32)
        rows = tm // dil
        ngrp = ref.shape[2] // LANES
        for k in range(ngrp):
            for c in range(dil):
                scr[k, pl.ds(c, rows, stride=dil), :] = ref[c, :, k * LANES:(k + 1) * LANES].astype(F32)
        return jnp.concatenate([scr[k] for k in range(ngrp)], axis=1)

    lses = [token_order(l_ref, lse_scr.at[pl.ds(g, 1)], dil)
            for g, (l_ref, dil) in enumerate(zip((l0_ref, l1_ref, l2_ref), dils))]
    m = jnp.maximum(jnp.maximum(lses[0], lses[1]), lses[2])
    ws = [jnp.exp(l - m) for l in lses]
    tot = ws[0] + ws[1] + ws[2]
    e = e_ref[...]
    acc = None
    for w, o_ref, dil in zip(ws, (o0_ref, o1_ref, o2_ref), dils):
        hi, lo = _split(w / tot)
        term = (_dot(hi, e) + _dot(lo, e)) * token_order(o_ref, o_scr, dil)
        acc = term if acc is None else acc + term
    out_ref[...] = acc.astype(out_ref.dtype)


def _combine(os_, lses, d):
    dils = tuple(o.shape[0] for o in os_)
    n = os_[0].shape[0] * os_[0].shape[1]
    tm = _pick(n, (256,))
    head_of_lane = jnp.arange(d, dtype=jnp.int32) // ATT_HEAD
    expand = (jnp.arange(ATT_HEAD, dtype=jnp.int32)[:, None] == head_of_lane[None, :]).astype(BF16)

    def classes(dil, width):
        return pl.BlockSpec((dil, tm // dil, width), lambda i: (0, i, 0))

    return pl.pallas_call(
        functools.partial(_combine_kernel, dils=dils),
        out_shape=jax.ShapeDtypeStruct((n, d), BF16),
        grid=(n // tm,),
        in_specs=[classes(dil, d) for dil in dils] + [classes(dil, ATT_HEAD) for dil in dils]
                 + [pl.BlockSpec((ATT_HEAD, d), lambda i: (0, 0))],
        out_specs=pl.BlockSpec((tm, d), lambda i: (i, 0)),
        scratch_shapes=[pltpu.VMEM((len(dils), tm, LANES), F32), pltpu.VMEM((d // LANES, tm, LANES), F32)],
        compiler_params=_cparams(("parallel",)),
        name="attention_combine",
    )(*os_, *lses, expand)


def _rwkv_layer(x, gain, p, v_first, vres, geom):
    xs = _token_shift(x, gain, p["mu"], geom)
    rkv = _matmul_batched(xs, p["w_rkv"], 3, RKV_DTYPE)
    if v_first is None:
        v_first = rkv
    logw, kd, a, v, g = _rwkv_mid(xs, rkv, v_first, p, vres)
    maps = [_wkv_stage1(rkv, 0, logw, kd, v, rkv, 1, a, p["k_k"], z, reverse=(z == 1)) for z in range(2)]
    yf, yb = _wkv_stage2(maps[0], maps[1], geom)
    o = _rwkv_post(yf, yb, rkv, kd, v, g, p["r_k"], p["lnx_w"], p["lnx_b"])
    return _matmul_residual(o, p["w_o"], x), v_first


def _attention_layer(x, gain, w_qkv, w_o, geom):
    d = x.shape[1]
    assert all((window // 2) // dil == BAND for window, dil in DIL_PATTERNS)
    qkvs = _matmul_norm_classes(x, gain, w_qkv, [dil for _, dil in DIL_PATTERNS], BF16)
    outs, lses = zip(*[_band_attention(qkv, gi, geom, d) for gi, qkv in enumerate(qkvs)])
    return _matmul_residual(_combine(outs, lses, d), w_o, x)


def kernel(x_prompt, x_sample, ln1, ln2, ln_f, rw_mu, rw_w_rkv, rw_w0, rw_w1, rw_w2, rw_a0, rw_a1, rw_a2, rw_v0, rw_v1, rw_v2, rw_g1, rw_g2, rw_k_k, rw_k_a, rw_r_k, rw_lnx_w, rw_lnx_b, rw_w_o, at_w_qkv, at_w_o, ffn_w_gate, ffn_w_up, ffn_w_down):
    b1, t1, d = x_prompt.shape
    b2, t2, _ = x_sample.shape
    n1, n2 = b1 * t1, b2 * t2
    geom = (n1, t1, t2)
    depth = ln1.shape[0]
    assert d % PAIR == 0 and d % ATT_HEAD == 0
    max_dil = max(dil for _, dil in DIL_PATTERNS)
    assert t1 % (max_dil * BAND) == 0 and t2 % (max_dil * BAND) == 0

    x = jnp.concatenate([x_prompt.reshape(n1, d), x_sample.reshape(n2, d)], axis=0)
    bf = lambda w: w.astype(BF16)
    row = lambda w: w.reshape(1, d)

    v_first = None
    for i in range(depth):
        j = i // 2
        if i % 2 == 0:
            p = dict(mu=rw_mu[j], w_rkv=bf(rw_w_rkv[j]), w0=rw_w0[j], w1=bf(rw_w1[j]), w2=bf(rw_w2[j]),
                     a0=rw_a0[j], a1=bf(rw_a1[j]), a2=bf(rw_a2[j]), g1=bf(rw_g1[j]), g2=bf(rw_g2[j]),
                     k_k=row(rw_k_k[j]), k_a=row(rw_k_a[j]), r_k=rw_r_k[j].reshape(1, d),
                     lnx_w=row(rw_lnx_w[j]), lnx_b=row(rw_lnx_b[j]), w_o=bf(rw_w_o[j]))
            vres = None if j == 0 else (row(rw_v0[j - 1]), bf(rw_v1[j - 1]), bf(rw_v2[j - 1]))
            x, v_first = _rwkv_layer(x, row(ln1[i]), p, v_first, vres, geom)
        else:
            w_qkv = bf(at_w_qkv[j]).reshape(d, -1)
            x = _attention_layer(x, row(ln1[i]), w_qkv, bf(at_w_o[j]), geom)
        x = _ffn(x, row(ln2[i]), bf(ffn_w_gate[i]), bf(ffn_w_up[i]), bf(ffn_w_down[i]),
                 row(ln_f), final_norm=(i == depth - 1))
    return x[:n1].reshape(b1, t1, d), x[n1:].reshape(b2, t2, d)
```

```python
import functools
import math

import numpy as np

import jax
import jax.numpy as jnp
from jax import lax
from jax.experimental import pallas as pl
from jax.experimental.pallas import tpu as pltpu

F32 = jnp.float32
BF16 = jnp.bfloat16

RW_HEAD = 64
PAIR = 2 * RW_HEAD
ATT_HEAD = 128
LANES = 128
ROW_STRIDE = 4
DIL_PATTERNS = ((128, 1), (512, 4), (2048, 16))
BAND = 64
LNX_EPS = 64e-5
RMS_EPS = 1e-6
CHUNK = 64
NEG_BIG = -1e30
LOG2E = math.log2(math.e)
LN2 = math.log(2.0)
EXP_NEG_HALF = math.exp(-0.5)
RKV_DTYPE = BF16
WKV_MAP_DTYPE = BF16
WKV_OUT_DTYPE = BF16
WKV_SCAN_CHUNKS = 2
BAND_TILES = 2
HEAD_BATCH = 8
WKV_UNROLL = 8
V7X_VMEM_LIMIT_BYTES = 56 * 1024 * 1024


def _pick(n, candidates):
    for c in candidates:
        if n % c == 0:
            return c
    raise ValueError(f"no tile in {candidates} divides {n}")


def _cparams(sem):
    return pltpu.CompilerParams(dimension_semantics=sem, vmem_limit_bytes=V7X_VMEM_LIMIT_BYTES)


def _dot(a, b):
    return jnp.dot(a, b, preferred_element_type=F32)


def _dot_nt(a, b):
    return lax.dot_general(a, b, (((1,), (1,)), ((), ())), preferred_element_type=F32)


def _dot_tn(a, b):
    return lax.dot_general(a, b, (((0,), (0,)), ((), ())), preferred_element_type=F32)


def _split(x):
    hi = x.astype(BF16)
    lo = (x - hi.astype(F32)).astype(BF16)
    return hi, lo


def _sigmoid(x):
    return 1.0 / (1.0 + jnp.exp2(x * -LOG2E))


def _rms(x, g):
    return x * lax.rsqrt(jnp.mean(x * x, axis=-1, keepdims=True) + RMS_EPS) * g


def _seq_pos(idx, n1, t1, t2):
    first = idx < n1
    pos = jnp.where(first, lax.rem(idx, t1), lax.rem(idx - n1, t2))
    return pos, jnp.where(first, t1, t2)


def _mm_kernel(a_ref, w_ref, o_ref):
    o_ref[...] = _dot(a_ref[...], w_ref[...]).astype(o_ref.dtype)


def _matmul_batched(a, w, groups, out_dtype):
    _, m, k = a.shape
    n = w.shape[2]
    tm = _pick(m, (1024, 512, 256, 128))
    tn = _pick(n, (1024, 512, 256, 128))
    return pl.pallas_call(
        _mm_kernel,
        out_shape=jax.ShapeDtypeStruct((groups, m, n), out_dtype),
        grid=(groups, m // tm, n // tn),
        in_specs=[pl.BlockSpec((None, tm, k), lambda g, i, j: (g, i, 0)),
                  pl.BlockSpec((None, k, tn), lambda g, i, j: (g, 0, j))],
        out_specs=pl.BlockSpec((None, tm, tn), lambda g, i, j: (g, i, j)),
        compiler_params=_cparams(("parallel", "parallel", "arbitrary")),
        name="mm_batched",
    )(a, w)


def _class_segment(c, dil):
    return c if dil <= ROW_STRIDE else ROW_STRIDE * (c % ROW_STRIDE) + c // ROW_STRIDE


def _store_rows_by_class(hk, h_ref, cols, dil, a_ref, b_ref):
    tm = hk.shape[0]
    rows = tm // dil
    if dil == 1:
        h_ref[:, cols] = hk.astype(BF16)
        return
    a_ref[...] = hk
    if dil == ROW_STRIDE:
        for c0 in range(ROW_STRIDE):
            h_ref[c0 * rows:(c0 + 1) * rows, cols] = a_ref[pl.ds(c0, rows, stride=ROW_STRIDE), :].astype(BF16)
        return
    quarter = tm // ROW_STRIDE
    for c0 in range(ROW_STRIDE):
        b_ref[c0 * quarter:(c0 + 1) * quarter, :] = a_ref[pl.ds(c0, quarter, stride=ROW_STRIDE), :]
    for c0 in range(ROW_STRIDE):
        for b in range(ROW_STRIDE):
            seg = ROW_STRIDE * c0 + b
            h_ref[seg * rows:(seg + 1) * rows, cols] = b_ref[
                pl.ds(c0 * quarter + b, rows, stride=ROW_STRIDE), :].astype(BF16)


def _mm_qkv_kernel(x_ref, g_ref, w_ref, *refs, dils):
    ng = len(dils)
    o_refs, h_ref, a_ref, b_ref = refs[:ng], refs[ng], refs[ng + 1], refs[ng + 2]
    tm, k = x_ref.shape
    group = pl.program_id(1)

    @pl.when((group == 0) & (pl.program_id(2) == 0))
    def _():
        x = x_ref[...]
        inv = lax.rsqrt(jnp.mean(x * x, axis=-1, keepdims=True) + RMS_EPS)
        for kk in range(k // LANES):
            cols = slice(kk * LANES, (kk + 1) * LANES)
            hk = x_ref[:, cols] * inv * g_ref[:, cols]
            for gi, dil in enumerate(dils):
                _store_rows_by_class(hk, h_ref.at[gi], cols, dil, a_ref, b_ref)

    for gi, (o_ref, dil) in enumerate(zip(o_refs, dils)):
        @pl.when(group == gi)
        def _(gi=gi, o_ref=o_ref, dil=dil):
            rows = tm // dil
            res = _dot(h_ref[gi], w_ref[...])
            for c in range(dil):
                seg = _class_segment(c, dil)
                o_ref[c] = res[seg * rows:(seg + 1) * rows].astype(o_ref.dtype)


def _matmul_norm_classes(x, gain, w, dils, out_dtype):
    m, k = x.shape
    ngroups = len(dils)
    ng = w.shape[1] // ngroups
    tm = _pick(m, (1024, 512, 256))
    tn = _pick(ng, (1024, 512, 256, 128))
    nj = ng // tn
    assert all(dil in (1, ROW_STRIDE, ROW_STRIDE ** 2) for dil in dils)

    def out_spec(gi, dil):
        def index(i, g, j):
            return (0, i, jnp.where(g < gi, 0, jnp.where(g == gi, j, nj - 1)))
        return pl.BlockSpec((dil, tm // dil, tn), index)

    return pl.pallas_call(
        functools.partial(_mm_qkv_kernel, dils=tuple(dils)),
        out_shape=tuple(jax.ShapeDtypeStruct((dil, m // dil, ng), out_dtype) for dil in dils),
        grid=(m // tm, ngroups, nj),
        in_specs=[pl.BlockSpec((tm, k), lambda i, g, j: (i, 0)),
                  pl.BlockSpec((1, k), lambda i, g, j: (0, 0)),
                  pl.BlockSpec((k, tn), lambda i, g, j: (0, g * nj + j))],
        out_specs=tuple(out_spec(gi, dil) for gi, dil in enumerate(dils)),
        scratch_shapes=[pltpu.VMEM((ngroups, tm, k), BF16),
                        pltpu.VMEM((tm, LANES), F32), pltpu.VMEM((tm, LANES), F32)],
        compiler_params=_cparams(("parallel", "arbitrary", "arbitrary")),
        name="mm_norm_qkv",
    )(x, gain, w)


def _mm_res_kernel(a_ref, w_ref, r_ref, *refs):
    o_ref = refs[-1]
    o_ref[...] = r_ref[...] + _dot(a_ref[...], w_ref[...])


def _matmul_residual(a, w, res, row_off=0, out_rows=None, prev=None):
    m, n = res.shape
    k = a.shape[1]
    tm = _pick(math.gcd(m, row_off) if row_off else m, (1024, 512, 256, 128))
    tn = _pick(n, (1024, 512, 256, 128))
    ib = row_off // tm
    in_specs = [pl.BlockSpec((tm, k), lambda i, j: (i + ib, 0)),
                pl.BlockSpec((k, tn), lambda i, j: (0, j)),
                pl.BlockSpec((tm, tn), lambda i, j: (i, j))]
    args = [a, w, res]
    if out_rows is None:
        out_rows, aliases = m, {2: 0}
    elif prev is None:
        aliases = {}
    else:
        in_specs.append(pl.BlockSpec(memory_space=pl.ANY))
        args.append(prev)
        aliases = {3: 0}
    return pl.pallas_call(
        _mm_res_kernel,
        out_shape=jax.ShapeDtypeStruct((out_rows, n), F32),
        grid=(m // tm, n // tn),
        in_specs=in_specs,
        out_specs=pl.BlockSpec((tm, tn), lambda i, j: (i + ib, j)),
        input_output_aliases=aliases,
        compiler_params=_cparams(("parallel", "arbitrary")),
        name="mm_residual",
    )(*args)


def _ffn_kernel(x_ref, g_ref, wg_ref, wu_ref, wd_ref, gf_ref, o_ref, h_ref, acc_ref, *, final_norm):
    j = pl.program_id(1)

    @pl.when(j == 0)
    def _():
        x = x_ref[...]
        h_ref[...] = _rms(x, g_ref[...]).astype(BF16)
        acc_ref[...] = x

    h = h_ref[...]
    gate = _dot(h, wg_ref[...])
    up = _dot(h, wu_ref[...])
    act = (gate * _sigmoid(gate) * up).astype(BF16)
    acc_ref[...] += _dot(act, wd_ref[...])

    @pl.when(j == pl.num_programs(1) - 1)
    def _():
        y = acc_ref[...]
        if final_norm:
            y = _rms(y, gf_ref[...])
        o_ref[...] = y


def _ffn(x, gain, wg, wu, wd, final_gain, final_norm, rows=None):
    d = x.shape[1]
    start, m = rows if rows is not None else (0, x.shape[0])
    ff = wg.shape[1]
    tm = _pick(math.gcd(m, start) if start else m, (512, 256, 128))
    tf = _pick(ff, (512, 256, 128))
    ib = start // tm
    return pl.pallas_call(
        functools.partial(_ffn_kernel, final_norm=final_norm),
        out_shape=jax.ShapeDtypeStruct((m, d), F32),
        grid=(m // tm, ff // tf),
        in_specs=[pl.BlockSpec((tm, d), lambda i, j: (i + ib, 0)),
                  pl.BlockSpec((1, d), lambda i, j: (0, 0)),
                  pl.BlockSpec((d, tf), lambda i, j: (0, j)),
                  pl.BlockSpec((d, tf), lambda i, j: (0, j)),
                  pl.BlockSpec((tf, d), lambda i, j: (j, 0)),
                  pl.BlockSpec((1, d), lambda i, j: (0, 0))],
        out_specs=pl.BlockSpec((tm, d), lambda i, j: (i, 0)),
        scratch_shapes=[pltpu.VMEM((tm, d), BF16), pltpu.VMEM((tm, d), F32)],
        input_output_aliases={} if rows is not None else {0: 0},
        compiler_params=_cparams(("parallel", "arbitrary")),
        name="ffn",
    )(x, gain, wg, wu, wd, final_gain)


def _shift_kernel(x_ref, xp_ref, xn_ref, g_ref, mu_ref, *refs, tt, n1, t1, t2):
    o_ref = refs[-1]
    pos0, seq_len = _seq_pos(pl.program_id(0) * tt, n1, t1, t2)

    def inv_rms(x):
        return lax.rsqrt(jnp.mean(x * x, axis=-1, keepdims=True) + RMS_EPS)

    inv = inv_rms(x_ref[...])
    inv_before = jnp.where(pos0 == 0, 0.0, inv_rms(xp_ref[7:8, :]))
    inv_after = jnp.where(pos0 + tt == seq_len, 0.0, inv_rms(xn_ref[0:1, :]))
    row = lax.broadcasted_iota(jnp.int32, (tt, 1), 0)
    for k in range(x_ref.shape[1] // LANES):
        cols = slice(k * LANES, (k + 1) * LANES)
        g = g_ref[:, cols]
        h = x_ref[:, cols] * inv * g
        h_prev = jnp.where(row == 0, xp_ref[7:8, cols] * inv_before * g, pltpu.roll(h, 1, 0))
        h_next = jnp.where(row == tt - 1, xn_ref[0:1, cols] * inv_after * g, pltpu.roll(h, tt - 1, 0))
        xx = 0.5 * (h_prev + h_next) - h
        for c in range(6):
            o_ref[c, :, cols] = (h + xx * mu_ref[c:c + 1, cols]).astype(BF16)


def _token_shift(x, gain, mu, geom, row_off=0, out_rows=None, prev=None):
    n, d = x.shape
    n1, t1, t2 = geom
    tt = _pick(math.gcd(math.gcd(t1, t2), row_off) if row_off else math.gcd(t1, t2), (256, 128, 64, 32, 16, 8))
    nb8 = n // 8
    ib = row_off // tt
    in_specs = [pl.BlockSpec((tt, d), lambda i: (i, 0)),
                pl.BlockSpec((8, d), lambda i: (jnp.maximum(i * (tt // 8) - 1, 0), 0)),
                pl.BlockSpec((8, d), lambda i: (jnp.minimum((i + 1) * (tt // 8), nb8 - 1), 0)),
                pl.BlockSpec((1, d), lambda i: (0, 0)),
                pl.BlockSpec((6, d), lambda i: (0, 0))]
    args = [x, x, x, gain, mu]
    aliases = {}
    if prev is not None:
        in_specs.append(pl.BlockSpec(memory_space=pl.ANY))
        args.append(prev)
        aliases = {5: 0}
    return pl.pallas_call(
        functools.partial(_shift_kernel, tt=tt, n1=n1, t1=t1, t2=t2),
        out_shape=jax.ShapeDtypeStruct((6, out_rows or n, d), BF16),
        grid=(n // tt,),
        in_specs=in_specs,
        out_specs=pl.BlockSpec((6, tt, d), lambda i: (0, i + ib, 0)),
        input_output_aliases=aliases,
        compiler_params=_cparams(("parallel",)),
        name="token_shift",
    )(*args)


def _mid_kernel(*refs, has_vres):
    if has_vres:
        (xv_ref, xw_ref, xa_ref, xg_ref, k_ref, v_ref, vf_ref,
         w0_ref, w1_ref, w2_ref, a0_ref, a1_ref, a2_ref, g1_ref, g2_ref, ka_ref,
         v0_ref, v1_ref, v2_ref,
         logw_ref, kd_ref, a_ref, vo_ref, g_ref) = refs
    else:
        (xv_ref, xw_ref, xa_ref, xg_ref, k_ref, v_ref,
         w0_ref, w1_ref, w2_ref, a0_ref, a1_ref, a2_ref, g1_ref, g2_ref, ka_ref,
         logw_ref, kd_ref, a_ref, vo_ref, g_ref) = refs
    xw = xw_ref[...]
    xa = xa_ref[...]
    k = k_ref[...].astype(F32)
    kka = k * ka_ref[...]
    k_rest = k - kka
    for z in range(2):
        lora = _dot(jnp.tanh(_dot(xw, w1_ref[z])).astype(BF16), w2_ref[z])
        logw_ref[z] = -EXP_NEG_HALF * _sigmoid(w0_ref[z:z + 1, :] + lora)
        aa = _sigmoid(a0_ref[z:z + 1, :] + _dot(_dot(xa, a1_ref[z]).astype(BF16), a2_ref[z]))
        a_ref[z] = aa.astype(a_ref.dtype)
        kd_ref[z] = (k_rest + kka * aa).astype(kd_ref.dtype)
    v = v_ref[...].astype(F32)
    if has_vres:
        gate = _sigmoid(v0_ref[...] + _dot(_dot(xv_ref[...], v1_ref[...]).astype(BF16), v2_ref[...]))
        v = v + (vf_ref[...].astype(F32) - v) * gate
    vo_ref[...] = v.astype(vo_ref.dtype)
    g_ref[...] = _dot(_sigmoid(_dot(xg_ref[...], g1_ref[...])).astype(BF16), g2_ref[...]).astype(g_ref.dtype)


def _rwkv_mid(xs, rkv, v_first, p, vres):
    _, n, d = xs.shape
    tm = _pick(n, (128, 64, 32, 16, 8))
    has_vres = vres is not None

    def slab(c):
        return pl.BlockSpec((None, tm, d), lambda i, c=c: (c, i, 0))

    def whole(a):
        nd = a.ndim
        return pl.BlockSpec(a.shape, lambda i, nd=nd: (0,) * nd)

    tok = pl.BlockSpec((tm, d), lambda i: (i, 0))
    tok2 = pl.BlockSpec((2, tm, d), lambda i: (0, i, 0))
    weights = [p["w0"], p["w1"], p["w2"], p["a0"], p["a1"], p["a2"], p["g1"], p["g2"], p["k_a"]]
    args = [xs, xs, xs, xs, rkv, rkv]
    specs = [slab(2), slab(3), slab(4), slab(5), slab(1), slab(2)]
    if has_vres:
        args.append(v_first)
        specs.append(slab(2))
        weights += list(vres)
    args += weights
    specs += [whole(a) for a in weights]
    return pl.pallas_call(
        functools.partial(_mid_kernel, has_vres=has_vres),
        out_shape=(jax.ShapeDtypeStruct((2, n, d), F32),
                   jax.ShapeDtypeStruct((2, n, d), BF16),
                   jax.ShapeDtypeStruct((2, n, d), BF16),
                   jax.ShapeDtypeStruct((n, d), BF16),
                   jax.ShapeDtypeStruct((n, d), BF16)),
        grid=(n // tm,),
        in_specs=specs,
        out_specs=(tok2, tok2, tok2, tok, tok),
        compiler_params=_cparams(("parallel",)),
        name="rwkv_mid",
    )(*args)


def _pair_rows(x, head0):
    zero = jnp.zeros_like(x)
    return jnp.concatenate([jnp.where(head0, x, zero), jnp.where(head0, zero, x)], axis=0)


def _wkv1_kernel(r_ref, lw_ref, kd_ref, v_ref, k_ref, a_ref, kk_ref,
                 rp_ref, y0_ref, g_ref, h_ref, *, nchunk, npair, unroll, reverse):
    lane = lax.broadcasted_iota(jnp.int32, (CHUNK, PAIR), 1)
    head0 = lane < RW_HEAD
    col = jnp.where(head0, lane, lane - RW_HEAD)
    row = lax.broadcasted_iota(jnp.int32, (CHUNK, PAIR), 0)
    strict = (row < col) if reverse else (row > col)
    incl = (row <= col) if reverse else (row >= col)
    eye = row == col
    r2 = lax.broadcasted_iota(jnp.int32, (PAIR, PAIR), 0)
    q2 = lax.broadcasted_iota(jnp.int32, (PAIR, PAIR), 1)
    ones_bd = ((r2 < RW_HEAD) == (q2 < RW_HEAD)).astype(BF16)
    t1 = lax.broadcasted_iota(jnp.int32, (CHUNK, CHUNK), 0)
    s1 = lax.broadcasted_iota(jnp.int32, (CHUNK, CHUNK), 1)
    tri = ((t1 <= s1) if reverse else (t1 >= s1)).astype(BF16)
    tri2 = jnp.concatenate([tri, tri], axis=1)
    last = 0 if reverse else CHUNK - 1
    bd = functools.partial(_pair_rows, head0=head0)

    def prepare(rows, ln, lw, cum):
        kk = k_ref[rows, ln].astype(F32) * kk_ref[:, ln]
        ss = _dot((kk * kk).astype(BF16), ones_bd)
        kk = kk / jnp.maximum(jnp.sqrt(ss), 1e-12)
        b = kk * a_ref[rows, ln].astype(F32)
        kd = kd_ref[rows, ln].astype(F32)
        tot = cum[last:last + 1, :]
        e_neg = jnp.exp(-cum)
        e_tail = jnp.exp(tot - cum)
        w = dict(tot=tot)
        w["at"] = (-kk * jnp.exp(cum - lw)).astype(BF16)
        w["rt"] = r_ref[rows, ln].astype(F32) * jnp.exp(cum)
        w["bk_in"] = jnp.concatenate([bd((b * e_neg).astype(BF16)), bd((kd * e_neg).astype(BF16))], axis=0)
        w["bk_out"] = jnp.concatenate([(b * e_tail).astype(BF16), (kd * e_tail).astype(BF16)], axis=0)
        w["v"] = v_ref[rows, ln]
        return w

    def chunk_body(ci, carry):
        items = []
        for u in range(unroll):
            rows = pl.ds(pl.multiple_of((ci * unroll + u) * CHUNK, CHUNK), CHUNK)
            lw_all = lw_ref[rows, :]
            hi, lo = _split(lw_all)
            cum_all = _dot(tri2, jnp.concatenate([hi, lo], axis=0))
            for p in range(npair):
                ln = slice(p * PAIR, (p + 1) * PAIR)
                w = prepare(rows, ln, lw_all[:, ln], cum_all[:, ln])
                w["rows"], w["ln"] = rows, ln
                items.append(w)
        for w in items:
            m1 = _dot_nt(jnp.concatenate([w["at"], w["rt"].astype(BF16)], axis=0), w["bk_in"])
            a_ab = jnp.where(strict, m1[:CHUNK, :PAIR], 0.0)
            w["a_kv"] = jnp.concatenate([jnp.where(strict, m1[:CHUNK, PAIR:], 0.0),
                                         jnp.where(incl, m1[CHUNK:, PAIR:], 0.0)], axis=0).astype(BF16)
            w["a_rb"] = jnp.where(incl, m1[CHUNK:, :PAIR], 0.0).astype(BF16)
            w["pw"] = a_ab.astype(BF16)
            w["inv"] = jnp.where(eye, 1.0, a_ab)
        for w in items:
            w["pw"] = _dot(w["pw"], bd(w["pw"])).astype(BF16)
        for f in range(1, int(math.log2(CHUNK)) - 1):
            for w in items:
                both = _dot(jnp.concatenate([w["pw"], w["inv"].astype(BF16)], axis=0), bd(w["pw"]))
                w["pw"] = both[:CHUNK].astype(BF16)
                w["inv"] = w["inv"] + both[CHUNK:]
        for w in items:
            w["inv"] = w["inv"] + _dot(w["inv"].astype(BF16), bd(w["pw"]))
        for w in items:
            w["av"] = _dot(w["a_kv"], bd(w["v"]))
        for w in items:
            rhs = jnp.concatenate([bd(w["at"]), bd(w["av"][:CHUNK].astype(BF16))], axis=1)
            w["wu"] = _dot(w["inv"].astype(BF16), rhs).astype(BF16)
        for w in items:
            rows, ln = w["rows"], w["ln"]
            wu = w["wu"]
            rw = _dot(w["a_rb"], jnp.concatenate([bd(wu[:, :PAIR]), bd(wu[:, PAIR:])], axis=1))
            rp_ref[rows, ln] = (w["rt"] + rw[:, :PAIR]).astype(rp_ref.dtype)
            y0_ref[rows, ln] = (rw[:, PAIR:] + w["av"][CHUNK:]).astype(y0_ref.dtype)
            gfull = _dot_tn(w["bk_out"][:CHUNK], wu[:, :PAIR])
            gdiag = jnp.where(eye, jnp.exp(w["tot"]), 0.0)
            g_ref[rows, ln] = (jnp.where(head0, gfull[:CHUNK], gfull[CHUNK:]) + gdiag).astype(g_ref.dtype)
            hfull = _dot_tn(w["bk_out"], jnp.concatenate([wu[:, PAIR:], w["v"]], axis=0))
            h_ref[rows, ln] = jnp.where(head0, hfull[:CHUNK], hfull[CHUNK:]).astype(h_ref.dtype)
        return carry

    lax.fori_loop(0, nchunk // unroll, chunk_body, 0)


def _wkv_stage1(r3, r_slab, logw, kd, v, k3, k_slab, a, kk_scale, z, reverse):
    n, d = v.shape
    npair = _pick(d // PAIR, (4, 2, 1))
    lanes = npair * PAIR
    tb = _pick(n, (512, 256))

    def slab(c):
        return pl.BlockSpec((None, tb, lanes), lambda i, j, c=c: (c, i, j))

    tok = pl.BlockSpec((tb, lanes), lambda i, j: (i, j))
    return pl.pallas_call(
        functools.partial(_wkv1_kernel, nchunk=tb // CHUNK, npair=npair, unroll=WKV_UNROLL, reverse=reverse),
        out_shape=(jax.ShapeDtypeStruct((n, d), BF16),
                   jax.ShapeDtypeStruct((n, d), WKV_MAP_DTYPE),
                   jax.ShapeDtypeStruct((n, d), BF16),
                   jax.ShapeDtypeStruct((n, d), WKV_MAP_DTYPE)),
        grid=(n // tb, d // lanes),
        in_specs=[slab(r_slab), slab(z), slab(z), tok, slab(k_slab), slab(z),
                  pl.BlockSpec((1, lanes), lambda i, j: (0, j))],
        out_specs=(tok, tok, tok, tok),
        compiler_params=_cparams(("parallel", "parallel")),
        name="wkv_stage1_rev" if reverse else "wkv_stage1_fwd",
    )(r3, logw, kd, v, k3, a, kk_scale)


def _wkv2_kernel(rpf_ref, y0f_ref, gf_ref, hf_ref, rpb_ref, y0b_ref, gb_ref, hb_ref,
                 yf_ref, yb_ref, stf_ref, stb_ref, *, npairs, nblocks, sub, n1, t1, t2):
    i = pl.program_id(0)

    @pl.when(i == 0)
    def _():
        stf_ref[...] = jnp.zeros_like(stf_ref)
        stb_ref[...] = jnp.zeros_like(stb_ref)

    lane = lax.broadcasted_iota(jnp.int32, (CHUNK, PAIR), 1)
    head0 = lane < RW_HEAD
    bd = functools.partial(_pair_rows, head0=head0)
    fwd_refs = (rpf_ref, y0f_ref, gf_ref, hf_ref, yf_ref, stf_ref)
    bwd_refs = (rpb_ref, y0b_ref, gb_ref, hb_ref, yb_ref, stb_ref)
    for s in range(sub):
        cf, cb = s, sub - 1 - s
        pos_f, _ = _seq_pos((i * sub + cf) * CHUNK, n1, t1, t2)
        pos_b, len_b = _seq_pos(((nblocks - 1 - i) * sub + cb) * CHUNK, n1, t1, t2)
        dirs = ((fwd_refs, pl.ds(cf * CHUNK, CHUNK), pos_f == 0),
                (bwd_refs, pl.ds(cb * CHUNK, CHUNK), pos_b + CHUNK == len_b))
        items = [(refs, rows, fresh, slice(p * PAIR, (p + 1) * PAIR))
                 for p in range(npairs) for refs, rows, fresh in dirs]
        results = []
        for (rp_ref, _, g_ref, _, _, st_ref), rows, fresh, ln in items:
            state = jnp.where(fresh, 0.0, st_ref[:, ln])
            hi, lo = _split(state)
            lhs = jnp.concatenate([rp_ref[rows, ln], g_ref[rows, ln]], axis=0)
            results.append(_dot(jnp.concatenate([lhs, lhs], axis=1), jnp.concatenate([bd(hi), bd(lo)], axis=0)))
        for ((_, y0_ref, _, h_ref, y_ref, st_ref), rows, _, ln), res in zip(items, results):
            y_ref[rows, ln] = (y0_ref[rows, ln].astype(F32) + res[:CHUNK]).astype(y_ref.dtype)
            st_ref[:, ln] = res[CHUNK:] + h_ref[rows, ln].astype(F32)


def _wkv_stage2(fwd, bwd, geom):
    n, d = fwd[0].shape
    n1, t1, t2 = geom
    sub = _pick(n // CHUNK, (WKV_SCAN_CHUNKS, 1))
    nblocks = n // (sub * CHUNK)
    tok_f = pl.BlockSpec((sub * CHUNK, d), lambda i: (i, 0))
    tok_b = pl.BlockSpec((sub * CHUNK, d), lambda i: (nblocks - 1 - i, 0))
    return pl.pallas_call(
        functools.partial(_wkv2_kernel, npairs=d // PAIR, nblocks=nblocks, sub=sub, n1=n1, t1=t1, t2=t2),
        out_shape=(jax.ShapeDtypeStruct((n, d), WKV_OUT_DTYPE), jax.ShapeDtypeStruct((n, d), WKV_OUT_DTYPE)),
        grid=(nblocks,),
        in_specs=[tok_f] * 4 + [tok_b] * 4,
        out_specs=(tok_f, tok_b),
        scratch_shapes=[pltpu.VMEM((CHUNK, d), F32), pltpu.VMEM((CHUNK, d), F32)],
        compiler_params=_cparams(("arbitrary",)),
        name="wkv_stage2",
    )(*fwd, *bwd)


def _post_kernel(yf_ref, yb_ref, r_ref, kd_ref, v_ref, g_ref, rk_ref, lw_ref, lb_ref, o_ref, *, ngroups):
    r2 = lax.broadcasted_iota(jnp.int32, (PAIR, PAIR), 0)
    q2 = lax.broadcasted_iota(jnp.int32, (PAIR, PAIR), 1)
    same = (r2 < RW_HEAD) == (q2 < RW_HEAD)
    ones_bd = same.astype(BF16)
    mean_bd = (same.astype(F32) * (1.0 / RW_HEAD)).astype(BF16)
    mean_bd2 = jnp.concatenate([mean_bd, mean_bd], axis=0)
    for p in range(ngroups):
        ln = slice(p * PAIR, (p + 1) * PAIR)
        yf, yb = yf_ref[:, ln], yb_ref[:, ln]
        y = yf.astype(F32) + yb.astype(F32)
        assert yf.dtype == BF16 and yb.dtype == BF16
        mean = _dot(jnp.concatenate([yf, yb], axis=1), mean_bd2)
        dev = y - mean
        var = _dot((dev * dev).astype(BF16), mean_bd)
        yn = dev * lax.rsqrt(var + LNX_EPS) * lw_ref[:, ln] + lb_ref[:, ln]
        kd = kd_ref[0, :, ln].astype(F32) + kd_ref[1, :, ln].astype(F32)
        rkd = (r_ref[:, ln].astype(F32) * kd * rk_ref[:, ln]).astype(BF16)
        bonus = _dot(rkd, ones_bd) * v_ref[:, ln].astype(F32)
        o_ref[:, ln] = ((yn + bonus) * g_ref[:, ln].astype(F32)).astype(o_ref.dtype)


def _rwkv_post(yf, yb, rkv, kd, v, g, r_k, lnx_w, lnx_b):
    n, d = yf.shape
    tm = _pick(n, (256, 128, 64, 32, 16, 8))
    tok = pl.BlockSpec((tm, d), lambda i: (i, 0))
    row = pl.BlockSpec((1, d), lambda i: (0, 0))
    return pl.pallas_call(
        functools.partial(_post_kernel, ngroups=d // PAIR),
        out_shape=jax.ShapeDtypeStruct((n, d), BF16),
        grid=(n // tm,),
        in_specs=[tok, tok, pl.BlockSpec((None, tm, d), lambda i: (0, i, 0)),
                  pl.BlockSpec((2, tm, d), lambda i: (0, i, 0)), tok, tok, row, row, row],
        out_specs=tok,
        compiler_params=_cparams(("parallel",)),
        name="rwkv_post",
    )(yf, yb, rkv, kd, v, g, r_k, lnx_w, lnx_b)


def _band_bias(tq, dil, nheads):
    nk = tq + 2 * BAND
    dist = np.abs(np.arange(nk)[None, :] - BAND - np.arange(tq)[:, None])
    slopes = np.exp2(-8.0 * np.arange(1, nheads + 1) / nheads)
    bias = -slopes[:, None, None] * (dil * dist)[None] * LOG2E
    return jnp.asarray(np.where((dist <= BAND)[None], bias, NEG_BIG), dtype=F32)


def _band_kernel(q_ref, kp_ref, kc_ref, kn_ref, vp_ref, vc_ref, vn_ref, bias_ref, o_ref, lse_ref,
                 *, tq, nsub, nheads, n1l, l1, l2):
    tb = tq * nsub
    pos0, seq_len = _seq_pos(pl.program_id(0) * tb, n1l, l1, l2)
    nk = tq + 2 * BAND
    col = lax.broadcasted_iota(jnp.int32, (1, nk), 1)
    head_lane = lax.broadcasted_iota(jnp.int32, (1, ATT_HEAD), 1)
    scale2 = ATT_HEAD ** -0.5 * LOG2E
    lanes = [slice(h * ATT_HEAD, (h + 1) * ATT_HEAD) for h in range(nheads)]

    def window(p_ref, c_ref, n_ref, sub, ln):
        lo, hi = sub * tq - BAND, sub * tq + tq + BAND
        parts = []
        if lo < 0:
            parts.append(p_ref[:, ln])
        parts.append(c_ref[max(lo, 0):min(hi, tb), ln])
        if hi > tb:
            parts.append(n_ref[:, ln])
        return jnp.concatenate(parts, axis=0) if len(parts) > 1 else parts[0]

    edges = []
    for sub in range(nsub):
        ok = jnp.full((1, nk), True)
        if sub == 0:
            ok = ok & ((col >= BAND) | (pos0 > 0))
        if sub == nsub - 1:
            ok = ok & ((col < tq + BAND) | (pos0 + tb < seq_len))
        edges.append(jnp.where(ok, 0.0, NEG_BIG))
    lse_all = [jnp.zeros((tq, ATT_HEAD), F32) for _ in range(nsub)]
    items = [(sub, h) for sub in range(nsub) for h in range(nheads)]
    for i0 in range(0, len(items), HEAD_BATCH):
        batch = items[i0:i0 + HEAD_BATCH]
        scores, probs, dens = {}, {}, {}
        for sub, h in batch:
            q = q_ref[sub * tq:(sub + 1) * tq, lanes[h]]
            keys = window(kp_ref, kc_ref, kn_ref, sub, lanes[h])
            scores[sub, h] = _dot_nt(q, keys) * scale2 + (bias_ref[h] + edges[sub])
        for sub, h in batch:
            m = jnp.max(scores[sub, h], axis=-1, keepdims=True)
            probs[sub, h] = jnp.exp2(scores[sub, h] - m)
            dens[sub, h] = jnp.sum(probs[sub, h], axis=-1, keepdims=True)
            lse_all[sub] = lse_all[sub] + jnp.where(head_lane == h, (m + jnp.log2(dens[sub, h])) * LN2, 0.0)
        for sub, h in batch:
            vals = window(vp_ref, vc_ref, vn_ref, sub, lanes[h])
            out = _dot(probs[sub, h].astype(BF16), vals) / dens[sub, h]
            o_ref[sub * tq:(sub + 1) * tq, lanes[h]] = out.astype(o_ref.dtype)
    for sub in range(nsub):
        lse_ref[sub * tq:(sub + 1) * tq, :] = lse_all[sub]


def _band_attention(qkv, group, geom, d):
    dil, nl, _ = qkv.shape
    n1, t1, t2 = geom
    n1l, l1, l2 = n1 // dil, t1 // dil, t2 // dil
    tq = _pick(math.gcd(l1, l2), (128, 64))
    nsub = _pick(math.gcd(l1, l2) // tq, (BAND_TILES, 1))
    tb = tq * nsub
    hb = tb // BAND
    nhalo = nl // BAND
    nheads = d // ATT_HEAD

    def cur(s):
        return pl.BlockSpec((None, tb, d), lambda i, c, s=s: (c, i, s))

    def before(s):
        return pl.BlockSpec((None, BAND, d), lambda i, c, s=s: (c, jnp.maximum(i * hb - 1, 0), s))

    def after(s):
        return pl.BlockSpec((None, BAND, d), lambda i, c, s=s: (c, jnp.minimum((i + 1) * hb, nhalo - 1), s))

    return pl.pallas_call(
        functools.partial(_band_kernel, tq=tq, nsub=nsub, nheads=nheads, n1l=n1l, l1=l1, l2=l2),
        out_shape=(jax.ShapeDtypeStruct((dil, nl, d), BF16),
                   jax.ShapeDtypeStruct((dil, nl, ATT_HEAD), F32)),
        grid=(nl // tb, dil),
        in_specs=[cur(0), before(1), cur(1), after(1), before(2), cur(2), after(2),
                  pl.BlockSpec((nheads, tq, tq + 2 * BAND), lambda i, c: (0, 0, 0))],
        out_specs=(pl.BlockSpec((None, tb, d), lambda i, c: (c, i, 0)),
                   pl.BlockSpec((None, tb, ATT_HEAD), lambda i, c: (c, i, 0))),
        compiler_params=_cparams(("parallel", "parallel")),
        name=f"band_attention_g{group}",
    )(qkv, qkv, qkv, qkv, qkv, qkv, qkv, _band_bias(tq, dil, nheads))


def _combine_kernel(o0_ref, o1_ref, o2_ref, l0_ref, l1_ref, l2_ref, e_ref, out_ref, lse_scr, o_scr, *, dils):
    tm = out_ref.shape[0]

    def token_order(ref, scr, dil):
        if dil == 1:
            return ref[0].astype(F32)
        rows = tm // dil
        ngrp = ref.shape[2] // LANES
        for k in range(ngrp):
            for c in range(dil):
                scr[k, pl.ds(c, rows, stride=dil), :] = ref[c, :, k * LANES:(k + 1) * LANES].astype(F32)
        return jnp.concatenate([scr[k] for k in range(ngrp)], axis=1)

    lses = [token_order(l_ref, lse_scr.at[pl.ds(g, 1)], dil)
            for g, (l_ref, dil) in enumerate(zip((l0_ref, l1_ref, l2_ref), dils))]
    m = jnp.maximum(jnp.maximum(lses[0], lses[1]), lses[2])
    ws = [jnp.exp(l - m) for l in lses]
    tot = ws[0] + ws[1] + ws[2]
    e = e_ref[...]
    acc = None
    for w, o_ref, dil in zip(ws, (o0_ref, o1_ref, o2_ref), dils):
        hi, lo = _split(w / tot)
        term = _dot(jnp.concatenate([hi, lo], axis=1), e) * token_order(o_ref, o_scr, dil)
        acc = term if acc is None else acc + term
    out_ref[...] = acc.astype(out_ref.dtype)


def _combine(os_, lses, d):
    dils = tuple(o.shape[0] for o in os_)
    n = os_[0].shape[0] * os_[0].shape[1]
    tm = _pick(n, (256,))
    head_of_lane = jnp.arange(d, dtype=jnp.int32) // ATT_HEAD
    expand = (jnp.arange(ATT_HEAD, dtype=jnp.int32)[:, None] == head_of_lane[None, :]).astype(BF16)
    expand = jnp.concatenate([expand, expand], axis=0)

    def classes(dil, width):
        return pl.BlockSpec((dil, tm // dil, width), lambda i: (0, i, 0))

    return pl.pallas_call(
        functools.partial(_combine_kernel, dils=dils),
        out_shape=jax.ShapeDtypeStruct((n, d), BF16),
        grid=(n // tm,),
        in_specs=[classes(dil, d) for dil in dils] + [classes(dil, ATT_HEAD) for dil in dils]
                 + [pl.BlockSpec((2 * ATT_HEAD, d), lambda i: (0, 0))],
        out_specs=pl.BlockSpec((tm, d), lambda i: (i, 0)),
        scratch_shapes=[pltpu.VMEM((len(dils), tm, LANES), F32), pltpu.VMEM((d // LANES, tm, LANES), F32)],
        compiler_params=_cparams(("parallel",)),
        name="attention_combine",
    )(*os_, *lses, expand)


def _rwkv_layer(x, gain, p, v_first, vres, geom):
    parts = None if not isinstance(x, list) else x
    if parts is None:
        xs = _token_shift(x, gain, p["mu"], geom)
    else:
        total = sum(xp.shape[0] for xp, _ in parts)
        xs, off = None, 0
        for xp, tp in parts:
            xs = _token_shift(xp, gain, p["mu"], (xp.shape[0], tp, tp), row_off=off, out_rows=total, prev=xs)
            off += xp.shape[0]
    rkv = _matmul_batched(xs, p["w_rkv"], 3, RKV_DTYPE)
    if v_first is None:
        v_first = rkv
    logw, kd, a, v, g = _rwkv_mid(xs, rkv, v_first, p, vres)
    maps = [_wkv_stage1(rkv, 0, logw, kd, v, rkv, 1, a, p["k_k"], z, reverse=(z == 1)) for z in range(2)]
    yf, yb = _wkv_stage2(maps[0], maps[1], geom)
    o = _rwkv_post(yf, yb, rkv, kd, v, g, p["r_k"], p["lnx_w"], p["lnx_b"])
    if parts is None:
        return _matmul_residual(o, p["w_o"], x), v_first
    out, off = None, 0
    for xp, _ in parts:
        out = _matmul_residual(o, p["w_o"], xp, row_off=off, out_rows=o.shape[0], prev=out)
        off += xp.shape[0]
    return out, v_first


def _attention_layer(x, gain, w_qkv, w_o, geom):
    d = x.shape[1]
    assert all((window // 2) // dil == BAND for window, dil in DIL_PATTERNS)
    qkvs = _matmul_norm_classes(x, gain, w_qkv, [dil for _, dil in DIL_PATTERNS], BF16)
    outs, lses = zip(*[_band_attention(qkv, gi, geom, d) for gi, qkv in enumerate(qkvs)])
    return _matmul_residual(_combine(outs, lses, d), w_o, x)


def kernel(x_prompt, x_sample, ln1, ln2, ln_f, rw_mu, rw_w_rkv, rw_w0, rw_w1, rw_w2, rw_a0, rw_a1, rw_a2, rw_v0, rw_v1, rw_v2, rw_g1, rw_g2, rw_k_k, rw_k_a, rw_r_k, rw_lnx_w, rw_lnx_b, rw_w_o, at_w_qkv, at_w_o, ffn_w_gate, ffn_w_up, ffn_w_down):
    b1, t1, d = x_prompt.shape
    b2, t2, _ = x_sample.shape
    n1, n2 = b1 * t1, b2 * t2
    geom = (n1, t1, t2)
    depth = ln1.shape[0]
    assert d % PAIR == 0 and d % ATT_HEAD == 0
    max_dil = max(dil for _, dil in DIL_PATTERNS)
    assert t1 % (max_dil * BAND) == 0 and t2 % (max_dil * BAND) == 0

    assert depth >= 1
    x = [(x_prompt.reshape(n1, d), t1), (x_sample.reshape(n2, d), t2)]
    bf = lambda w: w.astype(BF16)
    row = lambda w: w.reshape(1, d)

    v_first = None
    for i in range(depth):
        j = i // 2
        if i % 2 == 0:
            p = dict(mu=rw_mu[j], w_rkv=bf(rw_w_rkv[j]), w0=rw_w0[j], w1=bf(rw_w1[j]), w2=bf(rw_w2[j]),
                     a0=rw_a0[j], a1=bf(rw_a1[j]), a2=bf(rw_a2[j]), g1=bf(rw_g1[j]), g2=bf(rw_g2[j]),
                     k_k=row(rw_k_k[j]), k_a=row(rw_k_a[j]), r_k=rw_r_k[j].reshape(1, d),
                     lnx_w=row(rw_lnx_w[j]), lnx_b=row(rw_lnx_b[j]), w_o=bf(rw_w_o[j]))
            vres = None if j == 0 else (row(rw_v0[j - 1]), bf(rw_v1[j - 1]), bf(rw_v2[j - 1]))
            x, v_first = _rwkv_layer(x, row(ln1[i]), p, v_first, vres, geom)
        else:
            w_qkv = bf(at_w_qkv[j]).reshape(d, -1)
            x = _attention_layer(x, row(ln1[i]), w_qkv, bf(at_w_o[j]), geom)
        ffn_w = (row(ln2[i]), bf(ffn_w_gate[i]), bf(ffn_w_up[i]), bf(ffn_w_down[i]), row(ln_f))
        if i < depth - 1:
            x = _ffn(x, *ffn_w, final_norm=False)
    y_prompt = _ffn(x, *ffn_w, final_norm=True, rows=(0, n1))
    y_sample = _ffn(x, *ffn_w, final_norm=True, rows=(n1, n2))
    return y_prompt.reshape(b1, t1, d), y_sample.reshape(b2, t2, d)
```

```python
import functools
import math

import numpy as np

import jax
import jax.numpy as jnp
from jax import lax
from jax.experimental import pallas as pl
from jax.experimental.pallas import tpu as pltpu

F32 = jnp.float32
BF16 = jnp.bfloat16

RW_HEAD = 64
PAIR = 2 * RW_HEAD
ATT_HEAD = 128
LANES = 128
ROW_STRIDE = 4
DIL_PATTERNS = ((128, 1), (512, 4), (2048, 16))
BAND = 64
LNX_EPS = 64e-5
RMS_EPS = 1e-6
CHUNK = 64
NEG_BIG = -1e30
LOG2E = math.log2(math.e)
LN2 = math.log(2.0)
EXP_NEG_HALF = math.exp(-0.5)
RKV_DTYPE = BF16
WKV_MAP_DTYPE = BF16
WKV_OUT_DTYPE = BF16
WKV_SCAN_CHUNKS = 2
BAND_TILES = 2
HEAD_BATCH = 8
WKV_UNROLL = 8
V7X_VMEM_LIMIT_BYTES = 56 * 1024 * 1024


def _pick(n, candidates):
    for c in candidates:
        if n % c == 0:
            return c
    raise ValueError(f"no tile in {candidates} divides {n}")


def _cparams(sem):
    return pltpu.CompilerParams(dimension_semantics=sem, vmem_limit_bytes=V7X_VMEM_LIMIT_BYTES)


def _dot(a, b):
    return jnp.dot(a, b, preferred_element_type=F32)


def _dot_nt(a, b):
    return lax.dot_general(a, b, (((1,), (1,)), ((), ())), preferred_element_type=F32)


def _dot_tn(a, b):
    return lax.dot_general(a, b, (((0,), (0,)), ((), ())), preferred_element_type=F32)


def _split(x):
    hi = x.astype(BF16)
    lo = (x - hi.astype(F32)).astype(BF16)
    return hi, lo


def _sigmoid(x):
    return 1.0 / (1.0 + jnp.exp2(x * -LOG2E))


def _rms(x, g):
    return x * lax.rsqrt(jnp.mean(x * x, axis=-1, keepdims=True) + RMS_EPS) * g


def _seq_pos(idx, n1, t1, t2):
    first = idx < n1
    pos = jnp.where(first, lax.rem(idx, t1), lax.rem(idx - n1, t2))
    return pos, jnp.where(first, t1, t2)


def _mm_kernel(a_ref, w_ref, o_ref):
    o_ref[...] = _dot(a_ref[...], w_ref[...]).astype(o_ref.dtype)


def _matmul_batched(a, w, groups, out_dtype):
    _, m, k = a.shape
    n = w.shape[2]
    tm = _pick(m, (1024, 512, 256, 128))
    tn = _pick(n, (1024, 512, 256, 128))
    return pl.pallas_call(
        _mm_kernel,
        out_shape=jax.ShapeDtypeStruct((groups, m, n), out_dtype),
        grid=(groups, m // tm, n // tn),
        in_specs=[pl.BlockSpec((None, tm, k), lambda g, i, j: (g, i, 0)),
                  pl.BlockSpec((None, k, tn), lambda g, i, j: (g, 0, j))],
        out_specs=pl.BlockSpec((None, tm, tn), lambda g, i, j: (g, i, j)),
        compiler_params=_cparams(("parallel", "parallel", "arbitrary")),
        name="mm_batched",
    )(a, w)


def _class_segment(c, dil):
    return c if dil <= ROW_STRIDE else ROW_STRIDE * (c % ROW_STRIDE) + c // ROW_STRIDE


def _store_rows_by_class(hk, h_ref, cols, dil, a_ref, b_ref):
    tm = hk.shape[0]
    rows = tm // dil
    if dil == 1:
        h_ref[:, cols] = hk.astype(BF16)
        return
    a_ref[...] = hk
    if dil == ROW_STRIDE:
        for c0 in range(ROW_STRIDE):
            h_ref[c0 * rows:(c0 + 1) * rows, cols] = a_ref[pl.ds(c0, rows, stride=ROW_STRIDE), :].astype(BF16)
        return
    quarter = tm // ROW_STRIDE
    for c0 in range(ROW_STRIDE):
        b_ref[c0 * quarter:(c0 + 1) * quarter, :] = a_ref[pl.ds(c0, quarter, stride=ROW_STRIDE), :]
    for c0 in range(ROW_STRIDE):
        for b in range(ROW_STRIDE):
            seg = ROW_STRIDE * c0 + b
            h_ref[seg * rows:(seg + 1) * rows, cols] = b_ref[
                pl.ds(c0 * quarter + b, rows, stride=ROW_STRIDE), :].astype(BF16)


def _mm_qkv_kernel(x_ref, g_ref, w_ref, *refs, dils):
    ng = len(dils)
    o_refs, h_ref, a_ref, b_ref = refs[:ng], refs[ng], refs[ng + 1], refs[ng + 2]
    tm, k = x_ref.shape
    group = pl.program_id(1)

    @pl.when((group == 0) & (pl.program_id(2) == 0))
    def _():
        x = x_ref[...]
        inv = lax.rsqrt(jnp.mean(x * x, axis=-1, keepdims=True) + RMS_EPS)
        for kk in range(k // LANES):
            cols = slice(kk * LANES, (kk + 1) * LANES)
            hk = x_ref[:, cols] * inv * g_ref[:, cols]
            for gi, dil in enumerate(dils):
                _store_rows_by_class(hk, h_ref.at[gi], cols, dil, a_ref, b_ref)

    for gi, (o_ref, dil) in enumerate(zip(o_refs, dils)):
        @pl.when(group == gi)
        def _(gi=gi, o_ref=o_ref, dil=dil):
            rows = tm // dil
            res = _dot(h_ref[gi], w_ref[...])
            for c in range(dil):
                seg = _class_segment(c, dil)
                o_ref[c] = res[seg * rows:(seg + 1) * rows].astype(o_ref.dtype)


def _matmul_norm_classes(x, gain, w, dils, out_dtype):
    m, k = x.shape
    ngroups = len(dils)
    ng = w.shape[1] // ngroups
    tm = _pick(m, (1024, 512, 256))
    tn = _pick(ng, (1024, 512, 256, 128))
    nj = ng // tn
    assert all(dil in (1, ROW_STRIDE, ROW_STRIDE ** 2) for dil in dils)

    def out_spec(gi, dil):
        def index(i, g, j):
            return (0, i, jnp.where(g < gi, 0, jnp.where(g == gi, j, nj - 1)))
        return pl.BlockSpec((dil, tm // dil, tn), index)

    return pl.pallas_call(
        functools.partial(_mm_qkv_kernel, dils=tuple(dils)),
        out_shape=tuple(jax.ShapeDtypeStruct((dil, m // dil, ng), out_dtype) for dil in dils),
        grid=(m // tm, ngroups, nj),
        in_specs=[pl.BlockSpec((tm, k), lambda i, g, j: (i, 0)),
                  pl.BlockSpec((1, k), lambda i, g, j: (0, 0)),
                  pl.BlockSpec((k, tn), lambda i, g, j: (0, g * nj + j))],
        out_specs=tuple(out_spec(gi, dil) for gi, dil in enumerate(dils)),
        scratch_shapes=[pltpu.VMEM((ngroups, tm, k), BF16),
                        pltpu.VMEM((tm, LANES), F32), pltpu.VMEM((tm, LANES), F32)],
        compiler_params=_cparams(("parallel", "arbitrary", "arbitrary")),
        name="mm_norm_qkv",
    )(x, gain, w)


def _mm_res_kernel(a_ref, w_ref, r_ref, *refs):
    o_ref = refs[-1]
    o_ref[...] = r_ref[...] + _dot(a_ref[...], w_ref[...])


def _matmul_residual(a, w, res, row_off=0, out_rows=None, prev=None):
    m, n = res.shape
    k = a.shape[1]
    tm = _pick(math.gcd(m, row_off) if row_off else m, (512, 256, 128))
    tn = _pick(n, (2048, 1024, 512, 256, 128))
    ib = row_off // tm
    in_specs = [pl.BlockSpec((tm, k), lambda i, j: (i + ib, 0)),
                pl.BlockSpec((k, tn), lambda i, j: (0, j)),
                pl.BlockSpec((tm, tn), lambda i, j: (i, j))]
    args = [a, w, res]
    if out_rows is None:
        out_rows, aliases = m, {2: 0}
    elif prev is None:
        aliases = {}
    else:
        in_specs.append(pl.BlockSpec(memory_space=pl.ANY))
        args.append(prev)
        aliases = {3: 0}
    return pl.pallas_call(
        _mm_res_kernel,
        out_shape=jax.ShapeDtypeStruct((out_rows, n), F32),
        grid=(m // tm, n // tn),
        in_specs=in_specs,
        out_specs=pl.BlockSpec((tm, tn), lambda i, j: (i + ib, j)),
        input_output_aliases=aliases,
        compiler_params=_cparams(("parallel", "arbitrary")),
        name="mm_residual",
    )(*args)


def _ffn_kernel(x_ref, g_ref, wg_ref, wu_ref, wd_ref, gf_ref, o_ref, h_ref, *, final_norm):
    j = pl.program_id(1)

    @pl.when(j == 0)
    def _():
        x = x_ref[...]
        h_ref[...] = _rms(x, g_ref[...]).astype(BF16)
        o_ref[...] = x

    h = h_ref[...]
    gate = _dot(h, wg_ref[...])
    up = _dot(h, wu_ref[...])
    act = (gate * _sigmoid(gate) * up).astype(BF16)
    o_ref[...] += _dot(act, wd_ref[...])

    if final_norm:
        @pl.when(j == pl.num_programs(1) - 1)
        def _():
            o_ref[...] = _rms(o_ref[...], gf_ref[...])


def _ffn(x, gain, wg, wu, wd, final_gain, final_norm, rows=None):
    d = x.shape[1]
    start, m = rows if rows is not None else (0, x.shape[0])
    ff = wg.shape[1]
    tm = _pick(math.gcd(m, start) if start else m, (512, 256, 128))
    tf = _pick(ff, (512, 256, 128))
    ib = start // tm
    return pl.pallas_call(
        functools.partial(_ffn_kernel, final_norm=final_norm),
        out_shape=jax.ShapeDtypeStruct((m, d), F32),
        grid=(m // tm, ff // tf),
        in_specs=[pl.BlockSpec((tm, d), lambda i, j: (i + ib, 0)),
                  pl.BlockSpec((1, d), lambda i, j: (0, 0)),
                  pl.BlockSpec((d, tf), lambda i, j: (0, j)),
                  pl.BlockSpec((d, tf), lambda i, j: (0, j)),
                  pl.BlockSpec((tf, d), lambda i, j: (j, 0)),
                  pl.BlockSpec((1, d), lambda i, j: (0, 0))],
        out_specs=pl.BlockSpec((tm, d), lambda i, j: (i, 0)),
        scratch_shapes=[pltpu.VMEM((tm, d), BF16)],
        input_output_aliases={} if rows is not None else {0: 0},
        compiler_params=_cparams(("parallel", "arbitrary")),
        name="ffn",
    )(x, gain, wg, wu, wd, final_gain)


def _shift_kernel(x_ref, xp_ref, xn_ref, g_ref, mu_ref, *refs, tt, n1, t1, t2):
    o_ref = refs[-1]
    pos0, seq_len = _seq_pos(pl.program_id(0) * tt, n1, t1, t2)

    def inv_rms(x):
        return lax.rsqrt(jnp.mean(x * x, axis=-1, keepdims=True) + RMS_EPS)

    inv = inv_rms(x_ref[...])
    inv_before = jnp.where(pos0 == 0, 0.0, inv_rms(xp_ref[7:8, :]))
    inv_after = jnp.where(pos0 + tt == seq_len, 0.0, inv_rms(xn_ref[0:1, :]))
    row = lax.broadcasted_iota(jnp.int32, (tt, 1), 0)
    for k in range(x_ref.shape[1] // LANES):
        cols = slice(k * LANES, (k + 1) * LANES)
        g = g_ref[:, cols]
        h = x_ref[:, cols] * inv * g
        h_prev = jnp.where(row == 0, xp_ref[7:8, cols] * inv_before * g, pltpu.roll(h, 1, 0))
        h_next = jnp.where(row == tt - 1, xn_ref[0:1, cols] * inv_after * g, pltpu.roll(h, tt - 1, 0))
        xx = 0.5 * (h_prev + h_next) - h
        for c in range(6):
            o_ref[c, :, cols] = (h + xx * mu_ref[c:c + 1, cols]).astype(BF16)


def _token_shift(x, gain, mu, geom, row_off=0, out_rows=None, prev=None):
    n, d = x.shape
    n1, t1, t2 = geom
    tt = _pick(math.gcd(math.gcd(t1, t2), row_off) if row_off else math.gcd(t1, t2), (256, 128, 64, 32, 16, 8))
    nb8 = n // 8
    ib = row_off // tt
    in_specs = [pl.BlockSpec((tt, d), lambda i: (i, 0)),
                pl.BlockSpec((8, d), lambda i: (jnp.maximum(i * (tt // 8) - 1, 0), 0)),
                pl.BlockSpec((8, d), lambda i: (jnp.minimum((i + 1) * (tt // 8), nb8 - 1), 0)),
                pl.BlockSpec((1, d), lambda i: (0, 0)),
                pl.BlockSpec((6, d), lambda i: (0, 0))]
    args = [x, x, x, gain, mu]
    aliases = {}
    if prev is not None:
        in_specs.append(pl.BlockSpec(memory_space=pl.ANY))
        args.append(prev)
        aliases = {5: 0}
    return pl.pallas_call(
        functools.partial(_shift_kernel, tt=tt, n1=n1, t1=t1, t2=t2),
        out_shape=jax.ShapeDtypeStruct((6, out_rows or n, d), BF16),
        grid=(n // tt,),
        in_specs=in_specs,
        out_specs=pl.BlockSpec((6, tt, d), lambda i: (0, i + ib, 0)),
        input_output_aliases=aliases,
        compiler_params=_cparams(("parallel",)),
        name="token_shift",
    )(*args)


def _mid_kernel(*refs, has_vres):
    if has_vres:
        (xv_ref, xw_ref, xa_ref, xg_ref, k_ref, v_ref, vf_ref,
         w0_ref, w1_ref, w2_ref, a0_ref, a1_ref, a2_ref, g1_ref, g2_ref, ka_ref,
         v0_ref, v1_ref, v2_ref,
         logw_ref, kd_ref, a_ref, vo_ref, g_ref) = refs
    else:
        (xv_ref, xw_ref, xa_ref, xg_ref, k_ref, v_ref,
         w0_ref, w1_ref, w2_ref, a0_ref, a1_ref, a2_ref, g1_ref, g2_ref, ka_ref,
         logw_ref, kd_ref, a_ref, vo_ref, g_ref) = refs
    xw = xw_ref[...]
    xa = xa_ref[...]
    k = k_ref[...].astype(F32)
    kka = k * ka_ref[...]
    k_rest = k - kka
    for z in range(2):
        lora = _dot(jnp.tanh(_dot(xw, w1_ref[z])).astype(BF16), w2_ref[z])
        logw_ref[z] = -EXP_NEG_HALF * _sigmoid(w0_ref[z:z + 1, :] + lora)
        aa = _sigmoid(a0_ref[z:z + 1, :] + _dot(_dot(xa, a1_ref[z]).astype(BF16), a2_ref[z]))
        a_ref[z] = aa.astype(a_ref.dtype)
        kd_ref[z] = (k_rest + kka * aa).astype(kd_ref.dtype)
    v = v_ref[...].astype(F32)
    if has_vres:
        gate = _sigmoid(v0_ref[...] + _dot(_dot(xv_ref[...], v1_ref[...]).astype(BF16), v2_ref[...]))
        v = v + (vf_ref[...].astype(F32) - v) * gate
    vo_ref[...] = v.astype(vo_ref.dtype)
    g_ref[...] = _dot(_sigmoid(_dot(xg_ref[...], g1_ref[...])).astype(BF16), g2_ref[...]).astype(g_ref.dtype)


def _rwkv_mid(xs, rkv, v_first, p, vres):
    _, n, d = xs.shape
    tm = _pick(n, (128, 64, 32, 16, 8))
    has_vres = vres is not None

    def slab(c):
        return pl.BlockSpec((None, tm, d), lambda i, c=c: (c, i, 0))

    def whole(a):
        nd = a.ndim
        return pl.BlockSpec(a.shape, lambda i, nd=nd: (0,) * nd)

    tok = pl.BlockSpec((tm, d), lambda i: (i, 0))
    tok2 = pl.BlockSpec((2, tm, d), lambda i: (0, i, 0))
    weights = [p["w0"], p["w1"], p["w2"], p["a0"], p["a1"], p["a2"], p["g1"], p["g2"], p["k_a"]]
    args = [xs, xs, xs, xs, rkv, rkv]
    specs = [slab(2), slab(3), slab(4), slab(5), slab(1), slab(2)]
    if has_vres:
        args.append(v_first)
        specs.append(slab(2))
        weights += list(vres)
    args += weights
    specs += [whole(a) for a in weights]
    return pl.pallas_call(
        functools.partial(_mid_kernel, has_vres=has_vres),
        out_shape=(jax.ShapeDtypeStruct((2, n, d), F32),
                   jax.ShapeDtypeStruct((2, n, d), BF16),
                   jax.ShapeDtypeStruct((2, n, d), BF16),
                   jax.ShapeDtypeStruct((n, d), BF16),
                   jax.ShapeDtypeStruct((n, d), BF16)),
        grid=(n // tm,),
        in_specs=specs,
        out_specs=(tok2, tok2, tok2, tok, tok),
        compiler_params=_cparams(("parallel",)),
        name="rwkv_mid",
    )(*args)


def _pair_rows(x, head0):
    zero = jnp.zeros_like(x)
    return jnp.concatenate([jnp.where(head0, x, zero), jnp.where(head0, zero, x)], axis=0)


def _wkv1_kernel(r_ref, lw_ref, kd_ref, v_ref, k_ref, a_ref, kk_ref,
                 rp_ref, y0_ref, g_ref, h_ref, *, nchunk, npair, unroll, reverse):
    lane = lax.broadcasted_iota(jnp.int32, (CHUNK, PAIR), 1)
    head0 = lane < RW_HEAD
    col = jnp.where(head0, lane, lane - RW_HEAD)
    row = lax.broadcasted_iota(jnp.int32, (CHUNK, PAIR), 0)
    strict = (row < col) if reverse else (row > col)
    incl = (row <= col) if reverse else (row >= col)
    eye = row == col
    r2 = lax.broadcasted_iota(jnp.int32, (PAIR, PAIR), 0)
    q2 = lax.broadcasted_iota(jnp.int32, (PAIR, PAIR), 1)
    ones_bd = ((r2 < RW_HEAD) == (q2 < RW_HEAD)).astype(BF16)
    t1 = lax.broadcasted_iota(jnp.int32, (CHUNK, CHUNK), 0)
    s1 = lax.broadcasted_iota(jnp.int32, (CHUNK, CHUNK), 1)
    tri = ((t1 <= s1) if reverse else (t1 >= s1)).astype(BF16)
    tri2 = jnp.concatenate([tri, tri], axis=1)
    last = 0 if reverse else CHUNK - 1
    bd = functools.partial(_pair_rows, head0=head0)

    def prepare(rows, ln, lw, cum):
        kk = k_ref[rows, ln].astype(F32) * kk_ref[:, ln]
        ss = _dot((kk * kk).astype(BF16), ones_bd)
        kk = kk / jnp.maximum(jnp.sqrt(ss), 1e-12)
        b = kk * a_ref[rows, ln].astype(F32)
        kd = kd_ref[rows, ln].astype(F32)
        tot = cum[last:last + 1, :]
        e_neg = jnp.exp(-cum)
        e_tail = jnp.exp(tot - cum)
        w = dict(tot=tot)
        w["at"] = (-kk * jnp.exp(cum - lw)).astype(BF16)
        w["rt"] = r_ref[rows, ln].astype(F32) * jnp.exp(cum)
        w["bk_in"] = jnp.concatenate([bd((b * e_neg).astype(BF16)), bd((kd * e_neg).astype(BF16))], axis=0)
        w["bk_out"] = jnp.concatenate([(b * e_tail).astype(BF16), (kd * e_tail).astype(BF16)], axis=0)
        w["v"] = v_ref[rows, ln]
        return w

    def chunk_body(ci, carry):
        items = []
        for u in range(unroll):
            rows = pl.ds(pl.multiple_of((ci * unroll + u) * CHUNK, CHUNK), CHUNK)
            lw_all = lw_ref[rows, :]
            hi, lo = _split(lw_all)
            cum_all = _dot(tri2, jnp.concatenate([hi, lo], axis=0))
            for p in range(npair):
                ln = slice(p * PAIR, (p + 1) * PAIR)
                w = prepare(rows, ln, lw_all[:, ln], cum_all[:, ln])
                w["rows"], w["ln"] = rows, ln
                items.append(w)
        for w in items:
            m1 = _dot_nt(jnp.concatenate([w["at"], w["rt"].astype(BF16)], axis=0), w["bk_in"])
            a_ab = jnp.where(strict, m1[:CHUNK, :PAIR], 0.0)
            w["a_kv"] = jnp.concatenate([jnp.where(strict, m1[:CHUNK, PAIR:], 0.0),
                                         jnp.where(incl, m1[CHUNK:, PAIR:], 0.0)], axis=0).astype(BF16)
            w["a_rb"] = jnp.where(incl, m1[CHUNK:, :PAIR], 0.0).astype(BF16)
            w["pw"] = a_ab.astype(BF16)
            w["inv"] = jnp.where(eye, 1.0, a_ab)
        for w in items:
            w["pw"] = _dot(w["pw"], bd(w["pw"])).astype(BF16)
        for f in range(1, int(math.log2(CHUNK)) - 1):
            for w in items:
                both = _dot(jnp.concatenate([w["pw"], w["inv"].astype(BF16)], axis=0), bd(w["pw"]))
                w["pw"] = both[:CHUNK].astype(BF16)
                w["inv"] = w["inv"] + both[CHUNK:]
        for w in items:
            w["inv"] = w["inv"] + _dot(w["inv"].astype(BF16), bd(w["pw"]))
        for w in items:
            w["av"] = _dot(w["a_kv"], bd(w["v"]))
        for w in items:
            rhs = jnp.concatenate([bd(w["at"]), bd(w["av"][:CHUNK].astype(BF16))], axis=1)
            w["wu"] = _dot(w["inv"].astype(BF16), rhs).astype(BF16)
        for w in items:
            rows, ln = w["rows"], w["ln"]
            wu = w["wu"]
            rw = _dot(w["a_rb"], jnp.concatenate([bd(wu[:, :PAIR]), bd(wu[:, PAIR:])], axis=1))
            rp_ref[rows, ln] = (w["rt"] + rw[:, :PAIR]).astype(rp_ref.dtype)
            y0_ref[rows, ln] = (rw[:, PAIR:] + w["av"][CHUNK:]).astype(y0_ref.dtype)
            gfull = _dot_tn(w["bk_out"][:CHUNK], wu[:, :PAIR])
            gdiag = jnp.where(eye, jnp.exp(w["tot"]), 0.0)
            g_ref[rows, ln] = (jnp.where(head0, gfull[:CHUNK], gfull[CHUNK:]) + gdiag).astype(g_ref.dtype)
            hfull = _dot_tn(w["bk_out"], jnp.concatenate([wu[:, PAIR:], w["v"]], axis=0))
            h_ref[rows, ln] = jnp.where(head0, hfull[:CHUNK], hfull[CHUNK:]).astype(h_ref.dtype)
        return carry

    lax.fori_loop(0, nchunk // unroll, chunk_body, 0)


def _wkv_stage1(r3, r_slab, logw, kd, v, k3, k_slab, a, kk_scale, z, reverse):
    n, d = v.shape
    npair = _pick(d // PAIR, (4, 2, 1))
    lanes = npair * PAIR
    tb = _pick(n, (512, 256))

    def slab(c):
        return pl.BlockSpec((None, tb, lanes), lambda i, j, c=c: (c, i, j))

    tok = pl.BlockSpec((tb, lanes), lambda i, j: (i, j))
    return pl.pallas_call(
        functools.partial(_wkv1_kernel, nchunk=tb // CHUNK, npair=npair, unroll=WKV_UNROLL, reverse=reverse),
        out_shape=(jax.ShapeDtypeStruct((n, d), BF16),
                   jax.ShapeDtypeStruct((n, d), WKV_MAP_DTYPE),
                   jax.ShapeDtypeStruct((n, d), BF16),
                   jax.ShapeDtypeStruct((n, d), WKV_MAP_DTYPE)),
        grid=(n // tb, d // lanes),
        in_specs=[slab(r_slab), slab(z), slab(z), tok, slab(k_slab), slab(z),
                  pl.BlockSpec((1, lanes), lambda i, j: (0, j))],
        out_specs=(tok, tok, tok, tok),
        compiler_params=_cparams(("parallel", "parallel")),
        name="wkv_stage1_rev" if reverse else "wkv_stage1_fwd",
    )(r3, logw, kd, v, k3, a, kk_scale)


def _wkv2_kernel(rpf_ref, y0f_ref, gf_ref, hf_ref, rpb_ref, y0b_ref, gb_ref, hb_ref,
                 yf_ref, yb_ref, stf_ref, stb_ref, *, npairs, nblocks, sub, n1, t1, t2):
    i = pl.program_id(0)

    @pl.when(i == 0)
    def _():
        stf_ref[...] = jnp.zeros_like(stf_ref)
        stb_ref[...] = jnp.zeros_like(stb_ref)

    lane = lax.broadcasted_iota(jnp.int32, (CHUNK, PAIR), 1)
    head0 = lane < RW_HEAD
    bd = functools.partial(_pair_rows, head0=head0)
    fwd_refs = (rpf_ref, y0f_ref, gf_ref, hf_ref, yf_ref, stf_ref)
    bwd_refs = (rpb_ref, y0b_ref, gb_ref, hb_ref, yb_ref, stb_ref)
    for s in range(sub):
        cf, cb = s, sub - 1 - s
        pos_f, _ = _seq_pos((i * sub + cf) * CHUNK, n1, t1, t2)
        pos_b, len_b = _seq_pos(((nblocks - 1 - i) * sub + cb) * CHUNK, n1, t1, t2)
        dirs = ((fwd_refs, pl.ds(cf * CHUNK, CHUNK), pos_f == 0),
                (bwd_refs, pl.ds(cb * CHUNK, CHUNK), pos_b + CHUNK == len_b))
        items = [(refs, rows, fresh, slice(p * PAIR, (p + 1) * PAIR))
                 for p in range(npairs) for refs, rows, fresh in dirs]
        results = []
        for (rp_ref, _, g_ref, _, _, st_ref), rows, fresh, ln in items:
            state = jnp.where(fresh, 0.0, st_ref[:, ln])
            hi, lo = _split(state)
            lhs = jnp.concatenate([rp_ref[rows, ln], g_ref[rows, ln]], axis=0)
            results.append(_dot(jnp.concatenate([lhs, lhs], axis=1), jnp.concatenate([bd(hi), bd(lo)], axis=0)))
        for ((_, y0_ref, _, h_ref, y_ref, st_ref), rows, _, ln), res in zip(items, results):
            y_ref[rows, ln] = (y0_ref[rows, ln].astype(F32) + res[:CHUNK]).astype(y_ref.dtype)
            st_ref[:, ln] = res[CHUNK:] + h_ref[rows, ln].astype(F32)


def _wkv_stage2(fwd, bwd, geom):
    n, d = fwd[0].shape
    n1, t1, t2 = geom
    sub = _pick(n // CHUNK, (WKV_SCAN_CHUNKS, 1))
    nblocks = n // (sub * CHUNK)
    tok_f = pl.BlockSpec((sub * CHUNK, d), lambda i: (i, 0))
    tok_b = pl.BlockSpec((sub * CHUNK, d), lambda i: (nblocks - 1 - i, 0))
    return pl.pallas_call(
        functools.partial(_wkv2_kernel, npairs=d // PAIR, nblocks=nblocks, sub=sub, n1=n1, t1=t1, t2=t2),
        out_shape=(jax.ShapeDtypeStruct((n, d), WKV_OUT_DTYPE), jax.ShapeDtypeStruct((n, d), WKV_OUT_DTYPE)),
        grid=(nblocks,),
        in_specs=[tok_f] * 4 + [tok_b] * 4,
        out_specs=(tok_f, tok_b),
        scratch_shapes=[pltpu.VMEM((CHUNK, d), F32), pltpu.VMEM((CHUNK, d), F32)],
        compiler_params=_cparams(("arbitrary",)),
        name="wkv_stage2",
    )(*fwd, *bwd)


def _post_kernel(yf_ref, yb_ref, r_ref, kd_ref, v_ref, g_ref, rk_ref, lw_ref, lb_ref, o_ref, *, ngroups):
    r2 = lax.broadcasted_iota(jnp.int32, (PAIR, PAIR), 0)
    q2 = lax.broadcasted_iota(jnp.int32, (PAIR, PAIR), 1)
    same = (r2 < RW_HEAD) == (q2 < RW_HEAD)
    ones_bd = same.astype(BF16)
    mean_bd = (same.astype(F32) * (1.0 / RW_HEAD)).astype(BF16)
    mean_bd2 = jnp.concatenate([mean_bd, mean_bd], axis=0)
    for p in range(ngroups):
        ln = slice(p * PAIR, (p + 1) * PAIR)
        yf, yb = yf_ref[:, ln], yb_ref[:, ln]
        y = yf.astype(F32) + yb.astype(F32)
        assert yf.dtype == BF16 and yb.dtype == BF16
        mean = _dot(jnp.concatenate([yf, yb], axis=1), mean_bd2)
        dev = y - mean
        var = _dot((dev * dev).astype(BF16), mean_bd)
        yn = dev * lax.rsqrt(var + LNX_EPS) * lw_ref[:, ln] + lb_ref[:, ln]
        kd = kd_ref[0, :, ln].astype(F32) + kd_ref[1, :, ln].astype(F32)
        rkd = (r_ref[:, ln].astype(F32) * kd * rk_ref[:, ln]).astype(BF16)
        bonus = _dot(rkd, ones_bd) * v_ref[:, ln].astype(F32)
        o_ref[:, ln] = ((yn + bonus) * g_ref[:, ln].astype(F32)).astype(o_ref.dtype)


def _rwkv_post(yf, yb, rkv, kd, v, g, r_k, lnx_w, lnx_b):
    n, d = yf.shape
    tm = _pick(n, (256, 128, 64, 32, 16, 8))
    tok = pl.BlockSpec((tm, d), lambda i: (i, 0))
    row = pl.BlockSpec((1, d), lambda i: (0, 0))
    return pl.pallas_call(
        functools.partial(_post_kernel, ngroups=d // PAIR),
        out_shape=jax.ShapeDtypeStruct((n, d), BF16),
        grid=(n // tm,),
        in_specs=[tok, tok, pl.BlockSpec((None, tm, d), lambda i: (0, i, 0)),
                  pl.BlockSpec((2, tm, d), lambda i: (0, i, 0)), tok, tok, row, row, row],
        out_specs=tok,
        compiler_params=_cparams(("parallel",)),
        name="rwkv_post",
    )(yf, yb, rkv, kd, v, g, r_k, lnx_w, lnx_b)


def _band_bias(tq, dil, nheads):
    nk = tq + 2 * BAND
    dist = np.abs(np.arange(nk)[None, :] - BAND - np.arange(tq)[:, None])
    slopes = np.exp2(-8.0 * np.arange(1, nheads + 1) / nheads)
    bias = -slopes[:, None, None] * (dil * dist)[None] * LOG2E
    return jnp.asarray(np.where((dist <= BAND)[None], bias, NEG_BIG), dtype=F32)


def _band_kernel(q_ref, kp_ref, kc_ref, kn_ref, vp_ref, vc_ref, vn_ref, bias_ref, o_ref, lse_ref,
                 *, tq, nsub, nheads, n1l, l1, l2):
    tb = tq * nsub
    pos0, seq_len = _seq_pos(pl.program_id(0) * tb, n1l, l1, l2)
    nk = tq + 2 * BAND
    col = lax.broadcasted_iota(jnp.int32, (1, nk), 1)
    head_lane = lax.broadcasted_iota(jnp.int32, (1, ATT_HEAD), 1)
    scale2 = ATT_HEAD ** -0.5 * LOG2E
    lanes = [slice(h * ATT_HEAD, (h + 1) * ATT_HEAD) for h in range(nheads)]

    def window(p_ref, c_ref, n_ref, sub, ln):
        lo, hi = sub * tq - BAND, sub * tq + tq + BAND
        parts = []
        if lo < 0:
            parts.append(p_ref[:, ln])
        parts.append(c_ref[max(lo, 0):min(hi, tb), ln])
        if hi > tb:
            parts.append(n_ref[:, ln])
        return jnp.concatenate(parts, axis=0) if len(parts) > 1 else parts[0]

    edges = []
    for sub in range(nsub):
        ok = jnp.full((1, nk), True)
        if sub == 0:
            ok = ok & ((col >= BAND) | (pos0 > 0))
        if sub == nsub - 1:
            ok = ok & ((col < tq + BAND) | (pos0 + tb < seq_len))
        edges.append(jnp.where(ok, 0.0, NEG_BIG))
    lse_all = [jnp.zeros((tq, ATT_HEAD), F32) for _ in range(nsub)]
    items = [(sub, h) for sub in range(nsub) for h in range(nheads)]
    for i0 in range(0, len(items), HEAD_BATCH):
        batch = items[i0:i0 + HEAD_BATCH]
        scores, probs, dens = {}, {}, {}
        for sub, h in batch:
            q = q_ref[sub * tq:(sub + 1) * tq, lanes[h]]
            keys = window(kp_ref, kc_ref, kn_ref, sub, lanes[h])
            scores[sub, h] = _dot_nt(q, keys) * scale2 + (bias_ref[h] + edges[sub])
        for sub, h in batch:
            m = jnp.max(scores[sub, h], axis=-1, keepdims=True)
            probs[sub, h] = jnp.exp2(scores[sub, h] - m)
            dens[sub, h] = jnp.sum(probs[sub, h], axis=-1, keepdims=True)
            lse_all[sub] = lse_all[sub] + jnp.where(head_lane == h, (m + jnp.log2(dens[sub, h])) * LN2, 0.0)
        for sub, h in batch:
            vals = window(vp_ref, vc_ref, vn_ref, sub, lanes[h])
            out = _dot(probs[sub, h].astype(BF16), vals) / dens[sub, h]
            o_ref[sub * tq:(sub + 1) * tq, lanes[h]] = out.astype(o_ref.dtype)
    for sub in range(nsub):
        lse_ref[sub * tq:(sub + 1) * tq, :] = lse_all[sub]


def _band_attention(qkv, group, geom, d):
    dil, nl, _ = qkv.shape
    n1, t1, t2 = geom
    n1l, l1, l2 = n1 // dil, t1 // dil, t2 // dil
    tq = _pick(math.gcd(l1, l2), (128, 64))
    nsub = _pick(math.gcd(l1, l2) // tq, (BAND_TILES, 1))
    tb = tq * nsub
    hb = tb // BAND
    nhalo = nl // BAND
    nheads = d // ATT_HEAD

    def cur(s):
        return pl.BlockSpec((None, tb, d), lambda i, c, s=s: (c, i, s))

    def before(s):
        return pl.BlockSpec((None, BAND, d), lambda i, c, s=s: (c, jnp.maximum(i * hb - 1, 0), s))

    def after(s):
        return pl.BlockSpec((None, BAND, d), lambda i, c, s=s: (c, jnp.minimum((i + 1) * hb, nhalo - 1), s))

    return pl.pallas_call(
        functools.partial(_band_kernel, tq=tq, nsub=nsub, nheads=nheads, n1l=n1l, l1=l1, l2=l2),
        out_shape=(jax.ShapeDtypeStruct((dil, nl, d), BF16),
                   jax.ShapeDtypeStruct((dil, nl, ATT_HEAD), F32)),
        grid=(nl // tb, dil),
        in_specs=[cur(0), before(1), cur(1), after(1), before(2), cur(2), after(2),
                  pl.BlockSpec((nheads, tq, tq + 2 * BAND), lambda i, c: (0, 0, 0))],
        out_specs=(pl.BlockSpec((None, tb, d), lambda i, c: (c, i, 0)),
                   pl.BlockSpec((None, tb, ATT_HEAD), lambda i, c: (c, i, 0))),
        compiler_params=_cparams(("parallel", "parallel")),
        name=f"band_attention_g{group}",
    )(qkv, qkv, qkv, qkv, qkv, qkv, qkv, _band_bias(tq, dil, nheads))


def _combine_kernel(o0_ref, o1_ref, o2_ref, l0_ref, l1_ref, l2_ref, e_ref, out_ref, lse_scr, o_scr, *, dils):
    tm = out_ref.shape[0]

    def token_order(ref, scr, dil):
        if dil == 1:
            return ref[0].astype(F32)
        rows = tm // dil
        ngrp = ref.shape[2] // LANES
        for k in range(ngrp):
            for c in range(dil):
                scr[k, pl.ds(c, rows, stride=dil), :] = ref[c, :, k * LANES:(k + 1) * LANES].astype(F32)
        return jnp.concatenate([scr[k] for k in range(ngrp)], axis=1)

    lses = [token_order(l_ref, lse_scr.at[pl.ds(g, 1)], dil)
            for g, (l_ref, dil) in enumerate(zip((l0_ref, l1_ref, l2_ref), dils))]
    m = jnp.maximum(jnp.maximum(lses[0], lses[1]), lses[2])
    ws = [jnp.exp(l - m) for l in lses]
    tot = ws[0] + ws[1] + ws[2]
    e = e_ref[...]
    acc = None
    for w, o_ref, dil in zip(ws, (o0_ref, o1_ref, o2_ref), dils):
        hi, lo = _split(w / tot)
        term = _dot(jnp.concatenate([hi, lo], axis=1), e) * token_order(o_ref, o_scr, dil)
        acc = term if acc is None else acc + term
    out_ref[...] = acc.astype(out_ref.dtype)


def _combine(os_, lses, d):
    dils = tuple(o.shape[0] for o in os_)
    n = os_[0].shape[0] * os_[0].shape[1]
    tm = _pick(n, (256,))
    head_of_lane = jnp.arange(d, dtype=jnp.int32) // ATT_HEAD
    expand = (jnp.arange(ATT_HEAD, dtype=jnp.int32)[:, None] == head_of_lane[None, :]).astype(BF16)
    expand = jnp.concatenate([expand, expand], axis=0)

    def classes(dil, width):
        return pl.BlockSpec((dil, tm // dil, width), lambda i: (0, i, 0))

    return pl.pallas_call(
        functools.partial(_combine_kernel, dils=dils),
        out_shape=jax.ShapeDtypeStruct((n, d), BF16),
        grid=(n // tm,),
        in_specs=[classes(dil, d) for dil in dils] + [classes(dil, ATT_HEAD) for dil in dils]
                 + [pl.BlockSpec((2 * ATT_HEAD, d), lambda i: (0, 0))],
        out_specs=pl.BlockSpec((tm, d), lambda i: (i, 0)),
        scratch_shapes=[pltpu.VMEM((len(dils), tm, LANES), F32), pltpu.VMEM((d // LANES, tm, LANES), F32)],
        compiler_params=_cparams(("parallel",)),
        name="attention_combine",
    )(*os_, *lses, expand)


def _rwkv_layer(x, gain, p, v_first, vres, geom):
    parts = None if not isinstance(x, list) else x
    if parts is None:
        xs = _token_shift(x, gain, p["mu"], geom)
    else:
        total = sum(xp.shape[0] for xp, _ in parts)
        xs, off = None, 0
        for xp, tp in parts:
            xs = _token_shift(xp, gain, p["mu"], (xp.shape[0], tp, tp), row_off=off, out_rows=total, prev=xs)
            off += xp.shape[0]
    rkv = _matmul_batched(xs, p["w_rkv"], 3, RKV_DTYPE)
    if v_first is None:
        v_first = rkv
    logw, kd, a, v, g = _rwkv_mid(xs, rkv, v_first, p, vres)
    maps = [_wkv_stage1(rkv, 0, logw, kd, v, rkv, 1, a, p["k_k"], z, reverse=(z == 1)) for z in range(2)]
    yf, yb = _wkv_stage2(maps[0], maps[1], geom)
    o = _rwkv_post(yf, yb, rkv, kd, v, g, p["r_k"], p["lnx_w"], p["lnx_b"])
    if parts is None:
        return _matmul_residual(o, p["w_o"], x), v_first
    out, off = None, 0
    for xp, _ in parts:
        out = _matmul_residual(o, p["w_o"], xp, row_off=off, out_rows=o.shape[0], prev=out)
        off += xp.shape[0]
    return out, v_first


def _attention_layer(x, gain, w_qkv, w_o, geom):
    d = x.shape[1]
    assert all((window // 2) // dil == BAND for window, dil in DIL_PATTERNS)
    qkvs = _matmul_norm_classes(x, gain, w_qkv, [dil for _, dil in DIL_PATTERNS], BF16)
    outs, lses = zip(*[_band_attention(qkv, gi, geom, d) for gi, qkv in enumerate(qkvs)])
    return _matmul_residual(_combine(outs, lses, d), w_o, x)


def kernel(x_prompt, x_sample, ln1, ln2, ln_f, rw_mu, rw_w_rkv, rw_w0, rw_w1, rw_w2, rw_a0, rw_a1, rw_a2, rw_v0, rw_v1, rw_v2, rw_g1, rw_g2, rw_k_k, rw_k_a, rw_r_k, rw_lnx_w, rw_lnx_b, rw_w_o, at_w_qkv, at_w_o, ffn_w_gate, ffn_w_up, ffn_w_down):
    b1, t1, d = x_prompt.shape
    b2, t2, _ = x_sample.shape
    n1, n2 = b1 * t1, b2 * t2
    geom = (n1, t1, t2)
    depth = ln1.shape[0]
    assert d % PAIR == 0 and d % ATT_HEAD == 0
    max_dil = max(dil for _, dil in DIL_PATTERNS)
    assert t1 % (max_dil * BAND) == 0 and t2 % (max_dil * BAND) == 0

    assert depth >= 1
    x = [(x_prompt.reshape(n1, d), t1), (x_sample.reshape(n2, d), t2)]
    bf = lambda w: w.astype(BF16)
    row = lambda w: w.reshape(1, d)

    v_first = None
    for i in range(depth):
        j = i // 2
        if i % 2 == 0:
            p = dict(mu=rw_mu[j], w_rkv=bf(rw_w_rkv[j]), w0=rw_w0[j], w1=bf(rw_w1[j]), w2=bf(rw_w2[j]),
                     a0=rw_a0[j], a1=bf(rw_a1[j]), a2=bf(rw_a2[j]), g1=bf(rw_g1[j]), g2=bf(rw_g2[j]),
                     k_k=row(rw_k_k[j]), k_a=row(rw_k_a[j]), r_k=rw_r_k[j].reshape(1, d),
                     lnx_w=row(rw_lnx_w[j]), lnx_b=row(rw_lnx_b[j]), w_o=bf(rw_w_o[j]))
            vres = None if j == 0 else (row(rw_v0[j - 1]), bf(rw_v1[j - 1]), bf(rw_v2[j - 1]))
            x, v_first = _rwkv_layer(x, row(ln1[i]), p, v_first, vres, geom)
        else:
            w_qkv = bf(at_w_qkv[j]).reshape(d, -1)
            x = _attention_layer(x, row(ln1[i]), w_qkv, bf(at_w_o[j]), geom)
        ffn_w = (row(ln2[i]), bf(ffn_w_gate[i]), bf(ffn_w_up[i]), bf(ffn_w_down[i]), row(ln_f))
        if i < depth - 1:
            x = _ffn(x, *ffn_w, final_norm=False)
    y_prompt = _ffn(x, *ffn_w, final_norm=True, rows=(0, n1))
    y_sample = _ffn(x, *ffn_w, final_norm=True, rows=(n1, n2))
    return y_prompt.reshape(b1, t1, d), y_sample.reshape(b2, t2, d)
```

```python
import functools
import math

import numpy as np

import jax
import jax.numpy as jnp
from jax import lax
from jax.experimental import pallas as pl
from jax.experimental.pallas import tpu as pltpu

F32 = jnp.float32
BF16 = jnp.bfloat16

RW_HEAD = 64
PAIR = 2 * RW_HEAD
ATT_HEAD = 128
LANES = 128
ROW_STRIDE = 4
DIL_PATTERNS = ((128, 1), (512, 4), (2048, 16))
BAND = 64
LNX_EPS = 64e-5
RMS_EPS = 1e-6
CHUNK = 64
NEG_BIG = -1e30
LOG2E = math.log2(math.e)
LN2 = math.log(2.0)
EXP_NEG_HALF = math.exp(-0.5)
RKV_DTYPE = BF16
WKV_MAP_DTYPE = BF16
WKV_OUT_DTYPE = BF16
WKV_SCAN_CHUNKS = 4
BAND_TILES = 4
HEAD_BATCH = 8
WKV_UNROLL = 8
V7X_VMEM_LIMIT_BYTES = 56 * 1024 * 1024


def _pick(n, candidates):
    for c in candidates:
        if n % c == 0:
            return c
    raise ValueError(f"no tile in {candidates} divides {n}")


def _cparams(sem):
    return pltpu.CompilerParams(dimension_semantics=sem, vmem_limit_bytes=V7X_VMEM_LIMIT_BYTES)


def _dot(a, b):
    return jnp.dot(a, b, preferred_element_type=F32)


def _dot_nt(a, b):
    return lax.dot_general(a, b, (((1,), (1,)), ((), ())), preferred_element_type=F32)


def _dot_tn(a, b):
    return lax.dot_general(a, b, (((0,), (0,)), ((), ())), preferred_element_type=F32)


def _split(x):
    hi = x.astype(BF16)
    lo = (x - hi.astype(F32)).astype(BF16)
    return hi, lo


def _sigmoid(x):
    return 1.0 / (1.0 + jnp.exp2(x * -LOG2E))


def _rms(x, g):
    return x * lax.rsqrt(jnp.mean(x * x, axis=-1, keepdims=True) + RMS_EPS) * g


def _seq_pos(idx, n1, t1, t2):
    first = idx < n1
    pos = jnp.where(first, lax.rem(idx, t1), lax.rem(idx - n1, t2))
    return pos, jnp.where(first, t1, t2)


def _mm_kernel(a_ref, w_ref, o_ref):
    o_ref[...] = _dot(a_ref[...], w_ref[...]).astype(o_ref.dtype)


def _matmul_batched(a, w, groups, out_dtype):
    _, m, k = a.shape
    n = w.shape[2]
    tm = _pick(m, (1024, 512, 256, 128))
    tn = _pick(n, (1024, 512, 256, 128))
    return pl.pallas_call(
        _mm_kernel,
        out_shape=jax.ShapeDtypeStruct((groups, m, n), out_dtype),
        grid=(groups, m // tm, n // tn),
        in_specs=[pl.BlockSpec((None, tm, k), lambda g, i, j: (g, i, 0)),
                  pl.BlockSpec((None, k, tn), lambda g, i, j: (g, 0, j))],
        out_specs=pl.BlockSpec((None, tm, tn), lambda g, i, j: (g, i, j)),
        compiler_params=_cparams(("parallel", "parallel", "arbitrary")),
        name="mm_batched",
    )(a, w)


def _class_segment(c, dil):
    return c if dil <= ROW_STRIDE else ROW_STRIDE * (c % ROW_STRIDE) + c // ROW_STRIDE


def _store_rows_by_class(hk, h_ref, cols, dils, a_ref, b_ref):
    tm = hk.shape[0]
    quarter = tm // ROW_STRIDE
    if any(dil > 1 for dil in dils):
        a_ref[...] = hk
        for c0 in range(ROW_STRIDE):
            b_ref[c0 * quarter:(c0 + 1) * quarter, :] = a_ref[pl.ds(c0, quarter, stride=ROW_STRIDE), :]
    for gi, dil in enumerate(dils):
        if dil == 1:
            h_ref[gi, :, cols] = hk.astype(BF16)
        elif dil == ROW_STRIDE:
            h_ref[gi, :, cols] = b_ref[...].astype(BF16)
        else:
            rows = tm // dil
            for c0 in range(ROW_STRIDE):
                for b in range(ROW_STRIDE):
                    seg = ROW_STRIDE * c0 + b
                    h_ref[gi, seg * rows:(seg + 1) * rows, cols] = b_ref[
                        pl.ds(c0 * quarter + b, rows, stride=ROW_STRIDE), :].astype(BF16)


def _mm_qkv_kernel(x_ref, g_ref, w_ref, *refs, dils):
    ng = len(dils)
    o_refs, h_ref, a_ref, b_ref = refs[:ng], refs[ng], refs[ng + 1], refs[ng + 2]
    tm, k = x_ref.shape
    group = pl.program_id(1)

    @pl.when((group == 0) & (pl.program_id(2) == 0))
    def _():
        x = x_ref[...]
        inv = lax.rsqrt(jnp.mean(x * x, axis=-1, keepdims=True) + RMS_EPS)
        for kk in range(k // LANES):
            cols = slice(kk * LANES, (kk + 1) * LANES)
            hk = x_ref[:, cols] * inv * g_ref[:, cols]
            _store_rows_by_class(hk, h_ref, cols, dils, a_ref, b_ref)

    for gi, (o_ref, dil) in enumerate(zip(o_refs, dils)):
        @pl.when(group == gi)
        def _(gi=gi, o_ref=o_ref, dil=dil):
            rows = tm // dil
            res = _dot(h_ref[gi], w_ref[...])
            for c in range(dil):
                seg = _class_segment(c, dil)
                o_ref[c] = res[seg * rows:(seg + 1) * rows].astype(o_ref.dtype)


def _matmul_norm_classes(x, gain, w, dils, out_dtype):
    m, k = x.shape
    ngroups = len(dils)
    ng = w.shape[1] // ngroups
    tm = _pick(m, (1024, 512, 256))
    tn = _pick(ng, (1024, 512, 256, 128))
    nj = ng // tn
    assert all(dil in (1, ROW_STRIDE, ROW_STRIDE ** 2) for dil in dils)

    def out_spec(gi, dil):
        def index(i, g, j):
            return (0, i, jnp.where(g < gi, 0, jnp.where(g == gi, j, nj - 1)))
        return pl.BlockSpec((dil, tm // dil, tn), index)

    return pl.pallas_call(
        functools.partial(_mm_qkv_kernel, dils=tuple(dils)),
        out_shape=tuple(jax.ShapeDtypeStruct((dil, m // dil, ng), out_dtype) for dil in dils),
        grid=(m // tm, ngroups, nj),
        in_specs=[pl.BlockSpec((tm, k), lambda i, g, j: (i, 0)),
                  pl.BlockSpec((1, k), lambda i, g, j: (0, 0)),
                  pl.BlockSpec((k, tn), lambda i, g, j: (0, g * nj + j))],
        out_specs=tuple(out_spec(gi, dil) for gi, dil in enumerate(dils)),
        scratch_shapes=[pltpu.VMEM((ngroups, tm, k), BF16),
                        pltpu.VMEM((tm, LANES), F32), pltpu.VMEM((tm, LANES), F32)],
        compiler_params=_cparams(("parallel", "arbitrary", "arbitrary")),
        name="mm_norm_qkv",
    )(x, gain, w)


def _mm_res_kernel(a_ref, w_ref, r_ref, *refs):
    o_ref = refs[-1]
    o_ref[...] = r_ref[...] + _dot(a_ref[...], w_ref[...])


def _matmul_residual(a, w, res, row_off=0, out_rows=None, prev=None):
    m, n = res.shape
    k = a.shape[1]
    tm = _pick(math.gcd(m, row_off) if row_off else m, (512, 256, 128))
    tn = _pick(n, (2048, 1024, 512, 256, 128))
    ib = row_off // tm
    in_specs = [pl.BlockSpec((tm, k), lambda i, j: (i + ib, 0)),
                pl.BlockSpec((k, tn), lambda i, j: (0, j)),
                pl.BlockSpec((tm, tn), lambda i, j: (i, j))]
    args = [a, w, res]
    if out_rows is None:
        out_rows, aliases = m, {2: 0}
    elif prev is None:
        aliases = {}
    else:
        in_specs.append(pl.BlockSpec(memory_space=pl.ANY))
        args.append(prev)
        aliases = {3: 0}
    return pl.pallas_call(
        _mm_res_kernel,
        out_shape=jax.ShapeDtypeStruct((out_rows, n), F32),
        grid=(m // tm, n // tn),
        in_specs=in_specs,
        out_specs=pl.BlockSpec((tm, tn), lambda i, j: (i + ib, j)),
        input_output_aliases=aliases,
        compiler_params=_cparams(("parallel", "arbitrary")),
        name="mm_residual",
    )(*args)


def _ffn_kernel(x_ref, g_ref, wg_ref, wu_ref, wd_ref, gf_ref, o_ref, h_ref, *, final_norm):
    j = pl.program_id(1)

    @pl.when(j == 0)
    def _():
        x = x_ref[...]
        h_ref[...] = _rms(x, g_ref[...]).astype(BF16)
        o_ref[...] = x

    h = h_ref[...]
    gate = _dot(h, wg_ref[...])
    up = _dot(h, wu_ref[...])
    act = (gate * _sigmoid(gate) * up).astype(BF16)
    o_ref[...] += _dot(act, wd_ref[...])

    if final_norm:
        @pl.when(j == pl.num_programs(1) - 1)
        def _():
            o_ref[...] = _rms(o_ref[...], gf_ref[...])


def _ffn(x, gain, wg, wu, wd, final_gain, final_norm, rows=None):
    d = x.shape[1]
    start, m = rows if rows is not None else (0, x.shape[0])
    ff = wg.shape[1]
    tm = _pick(math.gcd(m, start) if start else m, (512, 256, 128))
    tf = _pick(ff, (512, 256, 128))
    ib = start // tm
    return pl.pallas_call(
        functools.partial(_ffn_kernel, final_norm=final_norm),
        out_shape=jax.ShapeDtypeStruct((m, d), F32),
        grid=(m // tm, ff // tf),
        in_specs=[pl.BlockSpec((tm, d), lambda i, j: (i + ib, 0)),
                  pl.BlockSpec((1, d), lambda i, j: (0, 0)),
                  pl.BlockSpec((d, tf), lambda i, j: (0, j)),
                  pl.BlockSpec((d, tf), lambda i, j: (0, j)),
                  pl.BlockSpec((tf, d), lambda i, j: (j, 0)),
                  pl.BlockSpec((1, d), lambda i, j: (0, 0))],
        out_specs=pl.BlockSpec((tm, d), lambda i, j: (i, 0)),
        scratch_shapes=[pltpu.VMEM((tm, d), BF16)],
        input_output_aliases={} if rows is not None else {0: 0},
        compiler_params=_cparams(("parallel", "arbitrary")),
        name="ffn",
    )(x, gain, wg, wu, wd, final_gain)


def _shift_kernel(x_ref, xp_ref, xn_ref, g_ref, mu_ref, *refs, tt, n1, t1, t2):
    o_ref = refs[-1]
    pos0, seq_len = _seq_pos(pl.program_id(0) * tt, n1, t1, t2)

    def inv_rms(x):
        return lax.rsqrt(jnp.mean(x * x, axis=-1, keepdims=True) + RMS_EPS)

    inv = inv_rms(x_ref[...])
    inv_before = jnp.where(pos0 == 0, 0.0, inv_rms(xp_ref[7:8, :]))
    inv_after = jnp.where(pos0 + tt == seq_len, 0.0, inv_rms(xn_ref[0:1, :]))
    row = lax.broadcasted_iota(jnp.int32, (tt, 1), 0)
    for k in range(x_ref.shape[1] // LANES):
        cols = slice(k * LANES, (k + 1) * LANES)
        g = g_ref[:, cols]
        h = x_ref[:, cols] * inv * g
        h_prev = jnp.where(row == 0, xp_ref[7:8, cols] * inv_before * g, pltpu.roll(h, 1, 0))
        h_next = jnp.where(row == tt - 1, xn_ref[0:1, cols] * inv_after * g, pltpu.roll(h, tt - 1, 0))
        xx = 0.5 * (h_prev + h_next) - h
        for c in range(6):
            o_ref[c, :, cols] = (h + xx * mu_ref[c:c + 1, cols]).astype(BF16)


def _token_shift(x, gain, mu, geom, row_off=0, out_rows=None, prev=None):
    n, d = x.shape
    n1, t1, t2 = geom
    tt = _pick(math.gcd(math.gcd(t1, t2), row_off) if row_off else math.gcd(t1, t2), (256, 128, 64, 32, 16, 8))
    nb8 = n // 8
    ib = row_off // tt
    in_specs = [pl.BlockSpec((tt, d), lambda i: (i, 0)),
                pl.BlockSpec((8, d), lambda i: (jnp.maximum(i * (tt // 8) - 1, 0), 0)),
                pl.BlockSpec((8, d), lambda i: (jnp.minimum((i + 1) * (tt // 8), nb8 - 1), 0)),
                pl.BlockSpec((1, d), lambda i: (0, 0)),
                pl.BlockSpec((6, d), lambda i: (0, 0))]
    args = [x, x, x, gain, mu]
    aliases = {}
    if prev is not None:
        in_specs.append(pl.BlockSpec(memory_space=pl.ANY))
        args.append(prev)
        aliases = {5: 0}
    return pl.pallas_call(
        functools.partial(_shift_kernel, tt=tt, n1=n1, t1=t1, t2=t2),
        out_shape=jax.ShapeDtypeStruct((6, out_rows or n, d), BF16),
        grid=(n // tt,),
        in_specs=in_specs,
        out_specs=pl.BlockSpec((6, tt, d), lambda i: (0, i + ib, 0)),
        input_output_aliases=aliases,
        compiler_params=_cparams(("parallel",)),
        name="token_shift",
    )(*args)


def _mid_kernel(*refs, has_vres):
    if has_vres:
        (xv_ref, xw_ref, xa_ref, xg_ref, k_ref, v_ref, vf_ref,
         w0_ref, w1_ref, w2_ref, a0_ref, a1_ref, a2_ref, g1_ref, g2_ref, ka_ref,
         v0_ref, v1_ref, v2_ref,
         logw_ref, kd_ref, a_ref, vo_ref, g_ref) = refs
    else:
        (xv_ref, xw_ref, xa_ref, xg_ref, k_ref, v_ref,
         w0_ref, w1_ref, w2_ref, a0_ref, a1_ref, a2_ref, g1_ref, g2_ref, ka_ref,
         logw_ref, kd_ref, a_ref, vo_ref, g_ref) = refs
    xw = xw_ref[...]
    xa = xa_ref[...]
    k = k_ref[...].astype(F32)
    kka = k * ka_ref[...]
    k_rest = k - kka
    for z in range(2):
        lora = _dot(jnp.tanh(_dot(xw, w1_ref[z])).astype(BF16), w2_ref[z])
        logw_ref[z] = -EXP_NEG_HALF * _sigmoid(w0_ref[z:z + 1, :] + lora)
        aa = _sigmoid(a0_ref[z:z + 1, :] + _dot(_dot(xa, a1_ref[z]).astype(BF16), a2_ref[z]))
        a_ref[z] = aa.astype(a_ref.dtype)
        kd_ref[z] = (k_rest + kka * aa).astype(kd_ref.dtype)
    v = v_ref[...].astype(F32)
    if has_vres:
        gate = _sigmoid(v0_ref[...] + _dot(_dot(xv_ref[...], v1_ref[...]).astype(BF16), v2_ref[...]))
        v = v + (vf_ref[...].astype(F32) - v) * gate
    vo_ref[...] = v.astype(vo_ref.dtype)
    g_ref[...] = _dot(_sigmoid(_dot(xg_ref[...], g1_ref[...])).astype(BF16), g2_ref[...]).astype(g_ref.dtype)


def _rwkv_mid(xs, rkv, v_first, p, vres):
    _, n, d = xs.shape
    tm = _pick(n, (256, 128, 64, 32, 16, 8))
    has_vres = vres is not None

    def slab(c):
        return pl.BlockSpec((None, tm, d), lambda i, c=c: (c, i, 0))

    def whole(a):
        nd = a.ndim
        return pl.BlockSpec(a.shape, lambda i, nd=nd: (0,) * nd)

    tok = pl.BlockSpec((tm, d), lambda i: (i, 0))
    tok2 = pl.BlockSpec((2, tm, d), lambda i: (0, i, 0))
    weights = [p["w0"], p["w1"], p["w2"], p["a0"], p["a1"], p["a2"], p["g1"], p["g2"], p["k_a"]]
    args = [xs, xs, xs, xs, rkv, rkv]
    specs = [slab(2), slab(3), slab(4), slab(5), slab(1), slab(2)]
    if has_vres:
        args.append(v_first)
        specs.append(slab(2))
        weights += list(vres)
    args += weights
    specs += [whole(a) for a in weights]
    return pl.pallas_call(
        functools.partial(_mid_kernel, has_vres=has_vres),
        out_shape=(jax.ShapeDtypeStruct((2, n, d), F32),
                   jax.ShapeDtypeStruct((2, n, d), BF16),
                   jax.ShapeDtypeStruct((2, n, d), BF16),
                   jax.ShapeDtypeStruct((n, d), BF16),
                   jax.ShapeDtypeStruct((n, d), BF16)),
        grid=(n // tm,),
        in_specs=specs,
        out_specs=(tok2, tok2, tok2, tok, tok),
        compiler_params=_cparams(("parallel",)),
        name="rwkv_mid",
    )(*args)


def _pair_rows(x, head0):
    zero = jnp.zeros_like(x)
    return jnp.concatenate([jnp.where(head0, x, zero), jnp.where(head0, zero, x)], axis=0)


def _wkv1_kernel(r_ref, lw_ref, kd_ref, v_ref, k_ref, a_ref, kk_ref,
                 rp_ref, y0_ref, g_ref, h_ref, *, nchunk, npair, unroll, reverse):
    lane = lax.broadcasted_iota(jnp.int32, (CHUNK, PAIR), 1)
    head0 = lane < RW_HEAD
    col = jnp.where(head0, lane, lane - RW_HEAD)
    row = lax.broadcasted_iota(jnp.int32, (CHUNK, PAIR), 0)
    strict = (row < col) if reverse else (row > col)
    incl = (row <= col) if reverse else (row >= col)
    eye = row == col
    r2 = lax.broadcasted_iota(jnp.int32, (PAIR, PAIR), 0)
    q2 = lax.broadcasted_iota(jnp.int32, (PAIR, PAIR), 1)
    ones_bd = ((r2 < RW_HEAD) == (q2 < RW_HEAD)).astype(BF16)
    t1 = lax.broadcasted_iota(jnp.int32, (CHUNK, CHUNK), 0)
    s1 = lax.broadcasted_iota(jnp.int32, (CHUNK, CHUNK), 1)
    tri = ((t1 <= s1) if reverse else (t1 >= s1)).astype(BF16)
    tri2 = jnp.concatenate([tri, tri], axis=1)
    last = 0 if reverse else CHUNK - 1
    bd = functools.partial(_pair_rows, head0=head0)

    def prepare(rows, ln, lw, cum):
        kk = k_ref[rows, ln].astype(F32) * kk_ref[:, ln]
        ss = _dot((kk * kk).astype(BF16), ones_bd)
        kk = kk / jnp.maximum(jnp.sqrt(ss), 1e-12)
        b = kk * a_ref[rows, ln].astype(F32)
        kd = kd_ref[rows, ln].astype(F32)
        tot = cum[last:last + 1, :]
        e_neg = jnp.exp(-cum)
        e_tail = jnp.exp(tot - cum)
        w = dict(tot=tot)
        w["at"] = (-kk * jnp.exp(cum - lw)).astype(BF16)
        w["rt"] = r_ref[rows, ln].astype(F32) * jnp.exp(cum)
        w["bk_in"] = jnp.concatenate([bd((b * e_neg).astype(BF16)), bd((kd * e_neg).astype(BF16))], axis=0)
        w["bk_out"] = jnp.concatenate([(b * e_tail).astype(BF16), (kd * e_tail).astype(BF16)], axis=0)
        w["v"] = v_ref[rows, ln]
        return w

    def chunk_body(ci, carry):
        items = []
        for u in range(unroll):
            rows = pl.ds(pl.multiple_of((ci * unroll + u) * CHUNK, CHUNK), CHUNK)
            lw_all = lw_ref[rows, :]
            hi, lo = _split(lw_all)
            cum_all = _dot(tri2, jnp.concatenate([hi, lo], axis=0))
            for p in range(npair):
                ln = slice(p * PAIR, (p + 1) * PAIR)
                w = prepare(rows, ln, lw_all[:, ln], cum_all[:, ln])
                w["rows"], w["ln"] = rows, ln
                items.append(w)
        for w in items:
            m1 = _dot_nt(jnp.concatenate([w["at"], w["rt"].astype(BF16)], axis=0), w["bk_in"])
            a_ab = jnp.where(strict, m1[:CHUNK, :PAIR], 0.0)
            w["a_kv"] = jnp.concatenate([jnp.where(strict, m1[:CHUNK, PAIR:], 0.0),
                                         jnp.where(incl, m1[CHUNK:, PAIR:], 0.0)], axis=0).astype(BF16)
            w["a_rb"] = jnp.where(incl, m1[CHUNK:, :PAIR], 0.0).astype(BF16)
            w["pw"] = a_ab.astype(BF16)
            w["inv"] = jnp.where(eye, 1.0, a_ab)
        for w in items:
            w["pw"] = _dot(w["pw"], bd(w["pw"])).astype(BF16)
        for f in range(1, int(math.log2(CHUNK)) - 1):
            for w in items:
                both = _dot(jnp.concatenate([w["pw"], w["inv"].astype(BF16)], axis=0), bd(w["pw"]))
                w["pw"] = both[:CHUNK].astype(BF16)
                w["inv"] = w["inv"] + both[CHUNK:]
        for w in items:
            w["inv"] = w["inv"] + _dot(w["inv"].astype(BF16), bd(w["pw"]))
        for w in items:
            w["av"] = _dot(w["a_kv"], bd(w["v"]))
        for w in items:
            rhs = jnp.concatenate([bd(w["at"]), bd(w["av"][:CHUNK].astype(BF16))], axis=1)
            w["wu"] = _dot(w["inv"].astype(BF16), rhs).astype(BF16)
        for w in items:
            rows, ln = w["rows"], w["ln"]
            wu = w["wu"]
            rw = _dot(w["a_rb"], jnp.concatenate([bd(wu[:, :PAIR]), bd(wu[:, PAIR:])], axis=1))
            rp_ref[rows, ln] = (w["rt"] + rw[:, :PAIR]).astype(rp_ref.dtype)
            y0_ref[rows, ln] = (rw[:, PAIR:] + w["av"][CHUNK:]).astype(y0_ref.dtype)
            gfull = _dot_tn(w["bk_out"][:CHUNK], wu[:, :PAIR])
            gdiag = jnp.where(eye, jnp.exp(w["tot"]), 0.0)
            g_ref[rows, ln] = (jnp.where(head0, gfull[:CHUNK], gfull[CHUNK:]) + gdiag).astype(g_ref.dtype)
            hfull = _dot_tn(w["bk_out"], jnp.concatenate([wu[:, PAIR:], w["v"]], axis=0))
            h_ref[rows, ln] = jnp.where(head0, hfull[:CHUNK], hfull[CHUNK:]).astype(h_ref.dtype)
        return carry

    lax.fori_loop(0, nchunk // unroll, chunk_body, 0)


def _wkv_stage1(r3, r_slab, logw, kd, v, k3, k_slab, a, kk_scale, z, reverse):
    n, d = v.shape
    npair = _pick(d // PAIR, (4, 2, 1))
    lanes = npair * PAIR
    tb = _pick(n, (512, 256))

    def slab(c):
        return pl.BlockSpec((None, tb, lanes), lambda i, j, c=c: (c, i, j))

    tok = pl.BlockSpec((tb, lanes), lambda i, j: (i, j))
    return pl.pallas_call(
        functools.partial(_wkv1_kernel, nchunk=tb // CHUNK, npair=npair, unroll=WKV_UNROLL, reverse=reverse),
        out_shape=(jax.ShapeDtypeStruct((n, d), BF16),
                   jax.ShapeDtypeStruct((n, d), WKV_MAP_DTYPE),
                   jax.ShapeDtypeStruct((n, d), BF16),
                   jax.ShapeDtypeStruct((n, d), WKV_MAP_DTYPE)),
        grid=(n // tb, d // lanes),
        in_specs=[slab(r_slab), slab(z), slab(z), tok, slab(k_slab), slab(z),
                  pl.BlockSpec((1, lanes), lambda i, j: (0, j))],
        out_specs=(tok, tok, tok, tok),
        compiler_params=_cparams(("parallel", "parallel")),
        name="wkv_stage1_rev" if reverse else "wkv_stage1_fwd",
    )(r3, logw, kd, v, k3, a, kk_scale)


def _wkv2_kernel(rpf_ref, y0f_ref, gf_ref, hf_ref, rpb_ref, y0b_ref, gb_ref, hb_ref,
                 yf_ref, yb_ref, stf_ref, stb_ref, *, npairs, nblocks, sub, n1, t1, t2):
    i = pl.program_id(0)

    @pl.when(i == 0)
    def _():
        stf_ref[...] = jnp.zeros_like(stf_ref)
        stb_ref[...] = jnp.zeros_like(stb_ref)

    lane = lax.broadcasted_iota(jnp.int32, (CHUNK, PAIR), 1)
    head0 = lane < RW_HEAD
    bd = functools.partial(_pair_rows, head0=head0)
    fwd_refs = (rpf_ref, y0f_ref, gf_ref, hf_ref, yf_ref, stf_ref)
    bwd_refs = (rpb_ref, y0b_ref, gb_ref, hb_ref, yb_ref, stb_ref)
    for s in range(sub):
        cf, cb = s, sub - 1 - s
        pos_f, _ = _seq_pos((i * sub + cf) * CHUNK, n1, t1, t2)
        pos_b, len_b = _seq_pos(((nblocks - 1 - i) * sub + cb) * CHUNK, n1, t1, t2)
        dirs = ((fwd_refs, pl.ds(cf * CHUNK, CHUNK), pos_f == 0),
                (bwd_refs, pl.ds(cb * CHUNK, CHUNK), pos_b + CHUNK == len_b))
        items = [(refs, rows, fresh, slice(p * PAIR, (p + 1) * PAIR))
                 for p in range(npairs) for refs, rows, fresh in dirs]
        results = []
        for (rp_ref, _, g_ref, _, _, st_ref), rows, fresh, ln in items:
            state = jnp.where(fresh, 0.0, st_ref[:, ln])
            hi, lo = _split(state)
            lhs = jnp.concatenate([rp_ref[rows, ln], g_ref[rows, ln]], axis=0)
            results.append(_dot(jnp.concatenate([lhs, lhs], axis=1), jnp.concatenate([bd(hi), bd(lo)], axis=0)))
        for ((_, y0_ref, _, h_ref, y_ref, st_ref), rows, _, ln), res in zip(items, results):
            y_ref[rows, ln] = (y0_ref[rows, ln].astype(F32) + res[:CHUNK]).astype(y_ref.dtype)
            st_ref[:, ln] = res[CHUNK:] + h_ref[rows, ln].astype(F32)


def _wkv_stage2(fwd, bwd, geom):
    n, d = fwd[0].shape
    n1, t1, t2 = geom
    sub = _pick(n // CHUNK, (WKV_SCAN_CHUNKS, 2, 1))
    nblocks = n // (sub * CHUNK)
    tok_f = pl.BlockSpec((sub * CHUNK, d), lambda i: (i, 0))
    tok_b = pl.BlockSpec((sub * CHUNK, d), lambda i: (nblocks - 1 - i, 0))
    return pl.pallas_call(
        functools.partial(_wkv2_kernel, npairs=d // PAIR, nblocks=nblocks, sub=sub, n1=n1, t1=t1, t2=t2),
        out_shape=(jax.ShapeDtypeStruct((n, d), WKV_OUT_DTYPE), jax.ShapeDtypeStruct((n, d), WKV_OUT_DTYPE)),
        grid=(nblocks,),
        in_specs=[tok_f] * 4 + [tok_b] * 4,
        out_specs=(tok_f, tok_b),
        scratch_shapes=[pltpu.VMEM((CHUNK, d), F32), pltpu.VMEM((CHUNK, d), F32)],
        compiler_params=_cparams(("arbitrary",)),
        name="wkv_stage2",
    )(*fwd, *bwd)


def _post_kernel(yf_ref, yb_ref, r_ref, kd_ref, v_ref, g_ref, rk_ref, lw_ref, lb_ref, o_ref, *, ngroups):
    r2 = lax.broadcasted_iota(jnp.int32, (PAIR, PAIR), 0)
    q2 = lax.broadcasted_iota(jnp.int32, (PAIR, PAIR), 1)
    same = (r2 < RW_HEAD) == (q2 < RW_HEAD)
    ones_bd = same.astype(BF16)
    mean_bd = (same.astype(F32) * (1.0 / RW_HEAD)).astype(BF16)
    mean_bd2 = jnp.concatenate([mean_bd, mean_bd], axis=0)
    for p in range(ngroups):
        ln = slice(p * PAIR, (p + 1) * PAIR)
        yf, yb = yf_ref[:, ln], yb_ref[:, ln]
        y = yf.astype(F32) + yb.astype(F32)
        assert yf.dtype == BF16 and yb.dtype == BF16
        mean = _dot(jnp.concatenate([yf, yb], axis=1), mean_bd2)
        dev = y - mean
        var = _dot((dev * dev).astype(BF16), mean_bd)
        yn = dev * lax.rsqrt(var + LNX_EPS) * lw_ref[:, ln] + lb_ref[:, ln]
        kd = kd_ref[0, :, ln].astype(F32) + kd_ref[1, :, ln].astype(F32)
        rkd = (r_ref[:, ln].astype(F32) * kd * rk_ref[:, ln]).astype(BF16)
        bonus = _dot(rkd, ones_bd) * v_ref[:, ln].astype(F32)
        o_ref[:, ln] = ((yn + bonus) * g_ref[:, ln].astype(F32)).astype(o_ref.dtype)


def _rwkv_post(yf, yb, rkv, kd, v, g, r_k, lnx_w, lnx_b):
    n, d = yf.shape
    tm = _pick(n, (256, 128, 64, 32, 16, 8))
    tok = pl.BlockSpec((tm, d), lambda i: (i, 0))
    row = pl.BlockSpec((1, d), lambda i: (0, 0))
    return pl.pallas_call(
        functools.partial(_post_kernel, ngroups=d // PAIR),
        out_shape=jax.ShapeDtypeStruct((n, d), BF16),
        grid=(n // tm,),
        in_specs=[tok, tok, pl.BlockSpec((None, tm, d), lambda i: (0, i, 0)),
                  pl.BlockSpec((2, tm, d), lambda i: (0, i, 0)), tok, tok, row, row, row],
        out_specs=tok,
        compiler_params=_cparams(("parallel",)),
        name="rwkv_post",
    )(yf, yb, rkv, kd, v, g, r_k, lnx_w, lnx_b)


def _band_bias(tq, dil, nheads):
    nk = tq + 2 * BAND
    dist = np.abs(np.arange(nk)[None, :] - BAND - np.arange(tq)[:, None])
    slopes = np.exp2(-8.0 * np.arange(1, nheads + 1) / nheads)
    bias = -slopes[:, None, None] * (dil * dist)[None] * LOG2E
    return jnp.asarray(np.where((dist <= BAND)[None], bias, NEG_BIG), dtype=F32)


def _band_kernel(q_ref, kp_ref, kc_ref, kn_ref, vp_ref, vc_ref, vn_ref, bias_ref, o_ref, lse_ref,
                 *, tq, nsub, nheads, n1l, l1, l2):
    tb = tq * nsub
    pos0, seq_len = _seq_pos(pl.program_id(0) * tb, n1l, l1, l2)
    nk = tq + 2 * BAND
    col = lax.broadcasted_iota(jnp.int32, (1, nk), 1)
    head_lane = lax.broadcasted_iota(jnp.int32, (1, ATT_HEAD), 1)
    scale2 = ATT_HEAD ** -0.5 * LOG2E
    lanes = [slice(h * ATT_HEAD, (h + 1) * ATT_HEAD) for h in range(nheads)]

    def window(p_ref, c_ref, n_ref, sub, ln):
        lo, hi = sub * tq - BAND, sub * tq + tq + BAND
        parts = []
        if lo < 0:
            parts.append(p_ref[:, ln])
        parts.append(c_ref[max(lo, 0):min(hi, tb), ln])
        if hi > tb:
            parts.append(n_ref[:, ln])
        return jnp.concatenate(parts, axis=0) if len(parts) > 1 else parts[0]

    edges = []
    for sub in range(nsub):
        ok = jnp.full((1, nk), True)
        if sub == 0:
            ok = ok & ((col >= BAND) | (pos0 > 0))
        if sub == nsub - 1:
            ok = ok & ((col < tq + BAND) | (pos0 + tb < seq_len))
        edges.append(jnp.where(ok, 0.0, NEG_BIG))
    lse_all = [jnp.zeros((tq, ATT_HEAD), F32) for _ in range(nsub)]
    items = [(sub, h) for sub in range(nsub) for h in range(nheads)]
    for i0 in range(0, len(items), HEAD_BATCH):
        batch = items[i0:i0 + HEAD_BATCH]
        scores, probs, dens = {}, {}, {}
        for sub, h in batch:
            q = q_ref[sub * tq:(sub + 1) * tq, lanes[h]]
            keys = window(kp_ref, kc_ref, kn_ref, sub, lanes[h])
            scores[sub, h] = _dot_nt(q, keys) * scale2 + (bias_ref[h] + edges[sub])
        for sub, h in batch:
            m = jnp.max(scores[sub, h], axis=-1, keepdims=True)
            probs[sub, h] = jnp.exp2(scores[sub, h] - m)
            dens[sub, h] = jnp.sum(probs[sub, h], axis=-1, keepdims=True)
            lse_all[sub] = lse_all[sub] + jnp.where(head_lane == h, (m + jnp.log2(dens[sub, h])) * LN2, 0.0)
        for sub, h in batch:
            vals = window(vp_ref, vc_ref, vn_ref, sub, lanes[h])
            out = _dot(probs[sub, h].astype(BF16), vals) / dens[sub, h]
            o_ref[sub * tq:(sub + 1) * tq, lanes[h]] = out.astype(o_ref.dtype)
    for sub in range(nsub):
        lse_ref[sub * tq:(sub + 1) * tq, :] = lse_all[sub]


def _band_attention(qkv, group, geom, d):
    dil, nl, _ = qkv.shape
    n1, t1, t2 = geom
    n1l, l1, l2 = n1 // dil, t1 // dil, t2 // dil
    tq = _pick(math.gcd(l1, l2), (128, 64))
    nsub = _pick(math.gcd(l1, l2) // tq, (BAND_TILES, 2, 1))
    tb = tq * nsub
    hb = tb // BAND
    nhalo = nl // BAND
    nheads = d // ATT_HEAD

    def cur(s):
        return pl.BlockSpec((None, tb, d), lambda i, c, s=s: (c, i, s))

    def before(s):
        return pl.BlockSpec((None, BAND, d), lambda i, c, s=s: (c, jnp.maximum(i * hb - 1, 0), s))

    def after(s):
        return pl.BlockSpec((None, BAND, d), lambda i, c, s=s: (c, jnp.minimum((i + 1) * hb, nhalo - 1), s))

    return pl.pallas_call(
        functools.partial(_band_kernel, tq=tq, nsub=nsub, nheads=nheads, n1l=n1l, l1=l1, l2=l2),
        out_shape=(jax.ShapeDtypeStruct((dil, nl, d), BF16),
                   jax.ShapeDtypeStruct((dil, nl, ATT_HEAD), F32)),
        grid=(nl // tb, dil),
        in_specs=[cur(0), before(1), cur(1), after(1), before(2), cur(2), after(2),
                  pl.BlockSpec((nheads, tq, tq + 2 * BAND), lambda i, c: (0, 0, 0))],
        out_specs=(pl.BlockSpec((None, tb, d), lambda i, c: (c, i, 0)),
                   pl.BlockSpec((None, tb, ATT_HEAD), lambda i, c: (c, i, 0))),
        compiler_params=_cparams(("parallel", "parallel")),
        name=f"band_attention_g{group}",
    )(qkv, qkv, qkv, qkv, qkv, qkv, qkv, _band_bias(tq, dil, nheads))


def _combine_kernel(o0_ref, o1_ref, o2_ref, l0_ref, l1_ref, l2_ref, e_ref, out_ref, lse_scr, o_scr, *, dils):
    tm = out_ref.shape[0]

    def token_order(ref, scr, dil):
        if dil == 1:
            return ref[0].astype(F32)
        rows = tm // dil
        ngrp = ref.shape[2] // LANES
        for k in range(ngrp):
            for c in range(dil):
                scr[k, pl.ds(c, rows, stride=dil), :] = ref[c, :, k * LANES:(k + 1) * LANES].astype(F32)
        return jnp.concatenate([scr[k] for k in range(ngrp)], axis=1)

    lses = [token_order(l_ref, lse_scr.at[pl.ds(g, 1)], dil)
            for g, (l_ref, dil) in enumerate(zip((l0_ref, l1_ref, l2_ref), dils))]
    m = jnp.maximum(jnp.maximum(lses[0], lses[1]), lses[2])
    ws = [jnp.exp(l - m) for l in lses]
    tot = ws[0] + ws[1] + ws[2]
    e = e_ref[...]
    acc = None
    for w, o_ref, dil in zip(ws, (o0_ref, o1_ref, o2_ref), dils):
        hi, lo = _split(w / tot)
        term = _dot(jnp.concatenate([hi, lo], axis=1), e) * token_order(o_ref, o_scr, dil)
        acc = term if acc is None else acc + term
    out_ref[...] = acc.astype(out_ref.dtype)


def _combine(os_, lses, d):
    dils = tuple(o.shape[0] for o in os_)
    n = os_[0].shape[0] * os_[0].shape[1]
    tm = _pick(n, (256,))
    head_of_lane = jnp.arange(d, dtype=jnp.int32) // ATT_HEAD
    expand = (jnp.arange(ATT_HEAD, dtype=jnp.int32)[:, None] == head_of_lane[None, :]).astype(BF16)
    expand = jnp.concatenate([expand, expand], axis=0)

    def classes(dil, width):
        return pl.BlockSpec((dil, tm // dil, width), lambda i: (0, i, 0))

    return pl.pallas_call(
        functools.partial(_combine_kernel, dils=dils),
        out_shape=jax.ShapeDtypeStruct((n, d), BF16),
        grid=(n // tm,),
        in_specs=[classes(dil, d) for dil in dils] + [classes(dil, ATT_HEAD) for dil in dils]
                 + [pl.BlockSpec((2 * ATT_HEAD, d), lambda i: (0, 0))],
        out_specs=pl.BlockSpec((tm, d), lambda i: (i, 0)),
        scratch_shapes=[pltpu.VMEM((len(dils), tm, LANES), F32), pltpu.VMEM((d // LANES, tm, LANES), F32)],
        compiler_params=_cparams(("parallel",)),
        name="attention_combine",
    )(*os_, *lses, expand)


def _rwkv_layer(x, gain, p, v_first, vres, geom):
    parts = None if not isinstance(x, list) else x
    if parts is None:
        xs = _token_shift(x, gain, p["mu"], geom)
    else:
        total = sum(xp.shape[0] for xp, _ in parts)
        xs, off = None, 0
        for xp, tp in parts:
            xs = _token_shift(xp, gain, p["mu"], (xp.shape[0], tp, tp), row_off=off, out_rows=total, prev=xs)
            off += xp.shape[0]
    rkv = _matmul_batched(xs, p["w_rkv"], 3, RKV_DTYPE)
    if v_first is None:
        v_first = rkv
    logw, kd, a, v, g = _rwkv_mid(xs, rkv, v_first, p, vres)
    maps = [_wkv_stage1(rkv, 0, logw, kd, v, rkv, 1, a, p["k_k"], z, reverse=(z == 1)) for z in range(2)]
    yf, yb = _wkv_stage2(maps[0], maps[1], geom)
    o = _rwkv_post(yf, yb, rkv, kd, v, g, p["r_k"], p["lnx_w"], p["lnx_b"])
    if parts is None:
        return _matmul_residual(o, p["w_o"], x), v_first
    out, off = None, 0
    for xp, _ in parts:
        out = _matmul_residual(o, p["w_o"], xp, row_off=off, out_rows=o.shape[0], prev=out)
        off += xp.shape[0]
    return out, v_first


def _attention_layer(x, gain, w_qkv, w_o, geom):
    d = x.shape[1]
    assert all((window // 2) // dil == BAND for window, dil in DIL_PATTERNS)
    qkvs = _matmul_norm_classes(x, gain, w_qkv, [dil for _, dil in DIL_PATTERNS], BF16)
    outs, lses = zip(*[_band_attention(qkv, gi, geom, d) for gi, qkv in enumerate(qkvs)])
    return _matmul_residual(_combine(outs, lses, d), w_o, x)


def kernel(x_prompt, x_sample, ln1, ln2, ln_f, rw_mu, rw_w_rkv, rw_w0, rw_w1, rw_w2, rw_a0, rw_a1, rw_a2, rw_v0, rw_v1, rw_v2, rw_g1, rw_g2, rw_k_k, rw_k_a, rw_r_k, rw_lnx_w, rw_lnx_b, rw_w_o, at_w_qkv, at_w_o, ffn_w_gate, ffn_w_up, ffn_w_down):
    b1, t1, d = x_prompt.shape
    b2, t2, _ = x_sample.shape
    n1, n2 = b1 * t1, b2 * t2
    geom = (n1, t1, t2)
    depth = ln1.shape[0]
    assert d % PAIR == 0 and d % ATT_HEAD == 0
    max_dil = max(dil for _, dil in DIL_PATTERNS)
    assert t1 % (max_dil * BAND) == 0 and t2 % (max_dil * BAND) == 0

    assert depth >= 1
    x = [(x_prompt.reshape(n1, d), t1), (x_sample.reshape(n2, d), t2)]
    bf = lambda w: w.astype(BF16)
    row = lambda w: w.reshape(1, d)

    v_first = None
    for i in range(depth):
        j = i // 2
        if i % 2 == 0:
            p = dict(mu=rw_mu[j], w_rkv=bf(rw_w_rkv[j]), w0=rw_w0[j], w1=bf(rw_w1[j]), w2=bf(rw_w2[j]),
                     a0=rw_a0[j], a1=bf(rw_a1[j]), a2=bf(rw_a2[j]), g1=bf(rw_g1[j]), g2=bf(rw_g2[j]),
                     k_k=row(rw_k_k[j]), k_a=row(rw_k_a[j]), r_k=rw_r_k[j].reshape(1, d),
                     lnx_w=row(rw_lnx_w[j]), lnx_b=row(rw_lnx_b[j]), w_o=bf(rw_w_o[j]))
            vres = None if j == 0 else (row(rw_v0[j - 1]), bf(rw_v1[j - 1]), bf(rw_v2[j - 1]))
            x, v_first = _rwkv_layer(x, row(ln1[i]), p, v_first, vres, geom)
        else:
            w_qkv = bf(at_w_qkv[j]).reshape(d, -1)
            x = _attention_layer(x, row(ln1[i]), w_qkv, bf(at_w_o[j]), geom)
        ffn_w = (row(ln2[i]), bf(ffn_w_gate[i]), bf(ffn_w_up[i]), bf(ffn_w_down[i]), row(ln_f))
        if i < depth - 1:
            x = _ffn(x, *ffn_w, final_norm=False)
    y_prompt = _ffn(x, *ffn_w, final_norm=True, rows=(0, n1))
    y_sample = _ffn(x, *ffn_w, final_norm=True, rows=(n1, n2))
    return y_prompt.reshape(b1, t1, d), y_sample.reshape(b2, t2, d)
```

```python
import functools
import math

import numpy as np

import jax
import jax.numpy as jnp
from jax import lax
from jax.experimental import pallas as pl
from jax.experimental.pallas import tpu as pltpu

F32 = jnp.float32
BF16 = jnp.bfloat16

RW_HEAD = 64
PAIR = 2 * RW_HEAD
ATT_HEAD = 128
LANES = 128
ROW_STRIDE = 4
DIL_PATTERNS = ((128, 1), (512, 4), (2048, 16))
BAND = 64
LNX_EPS = 64e-5
RMS_EPS = 1e-6
CHUNK = 64
NEG_BIG = -1e30
LOG2E = math.log2(math.e)
LN2 = math.log(2.0)
EXP_NEG_HALF = math.exp(-0.5)
RKV_DTYPE = BF16
WKV_MAP_DTYPE = BF16
WKV_OUT_DTYPE = BF16
WKV_SCAN_CHUNKS = 4
BAND_TILES = 4
HEAD_BATCH = 8
WKV_UNROLL = 4
V7X_VMEM_LIMIT_BYTES = 56 * 1024 * 1024


def _pick(n, candidates):
    for c in candidates:
        if n % c == 0:
            return c
    raise ValueError(f"no tile in {candidates} divides {n}")


def _cparams(sem):
    return pltpu.CompilerParams(dimension_semantics=sem, vmem_limit_bytes=V7X_VMEM_LIMIT_BYTES)


def _dot(a, b):
    return jnp.dot(a, b, preferred_element_type=F32)


def _dot_nt(a, b):
    return lax.dot_general(a, b, (((1,), (1,)), ((), ())), preferred_element_type=F32)


def _dot_tn(a, b):
    return lax.dot_general(a, b, (((0,), (0,)), ((), ())), preferred_element_type=F32)


def _split(x):
    hi = x.astype(BF16)
    lo = (x - hi.astype(F32)).astype(BF16)
    return hi, lo


def _sigmoid(x):
    return 1.0 / (1.0 + jnp.exp2(x * -LOG2E))


def _rms(x, g):
    return x * lax.rsqrt(jnp.mean(x * x, axis=-1, keepdims=True) + RMS_EPS) * g


def _seq_pos(idx, n1, t1, t2):
    first = idx < n1
    pos = jnp.where(first, lax.rem(idx, t1), lax.rem(idx - n1, t2))
    return pos, jnp.where(first, t1, t2)


def _mm_kernel(a_ref, w_ref, o_ref):
    o_ref[...] = _dot(a_ref[...], w_ref[...]).astype(o_ref.dtype)


def _matmul_batched(a, w, groups, out_dtype):
    _, m, k = a.shape
    n = w.shape[2]
    tm = _pick(m, (1024, 512, 256, 128))
    tn = _pick(n, (1024, 512, 256, 128))
    return pl.pallas_call(
        _mm_kernel,
        out_shape=jax.ShapeDtypeStruct((groups, m, n), out_dtype),
        grid=(groups, m // tm, n // tn),
        in_specs=[pl.BlockSpec((None, tm, k), lambda g, i, j: (g, i, 0)),
                  pl.BlockSpec((None, k, tn), lambda g, i, j: (g, 0, j))],
        out_specs=pl.BlockSpec((None, tm, tn), lambda g, i, j: (g, i, j)),
        compiler_params=_cparams(("parallel", "parallel", "arbitrary")),
        name="mm_batched",
    )(a, w)


def _class_segment(c, dil):
    return c if dil <= ROW_STRIDE else ROW_STRIDE * (c % ROW_STRIDE) + c // ROW_STRIDE


def _store_rows_by_class(hk, h_ref, cols, dils, a_ref, b_ref):
    tm = hk.shape[0]
    quarter = tm // ROW_STRIDE
    if any(dil > 1 for dil in dils):
        a_ref[...] = hk
        for c0 in range(ROW_STRIDE):
            b_ref[c0 * quarter:(c0 + 1) * quarter, :] = a_ref[pl.ds(c0, quarter, stride=ROW_STRIDE), :]
    for gi, dil in enumerate(dils):
        if dil == 1:
            h_ref[gi, :, cols] = hk.astype(BF16)
        elif dil == ROW_STRIDE:
            h_ref[gi, :, cols] = b_ref[...].astype(BF16)
        else:
            rows = tm // dil
            for c0 in range(ROW_STRIDE):
                for b in range(ROW_STRIDE):
                    seg = ROW_STRIDE * c0 + b
                    h_ref[gi, seg * rows:(seg + 1) * rows, cols] = b_ref[
                        pl.ds(c0 * quarter + b, rows, stride=ROW_STRIDE), :].astype(BF16)


def _mm_qkv_kernel(x_ref, g_ref, w_ref, *refs, dils):
    ng = len(dils)
    o_refs, h_ref, a_ref, b_ref = refs[:ng], refs[ng], refs[ng + 1], refs[ng + 2]
    tm, k = x_ref.shape
    group = pl.program_id(1)

    @pl.when((group == 0) & (pl.program_id(2) == 0))
    def _():
        x = x_ref[...]
        inv = lax.rsqrt(jnp.mean(x * x, axis=-1, keepdims=True) + RMS_EPS)
        for kk in range(k // LANES):
            cols = slice(kk * LANES, (kk + 1) * LANES)
            hk = x_ref[:, cols] * inv * g_ref[:, cols]
            _store_rows_by_class(hk, h_ref, cols, dils, a_ref, b_ref)

    for gi, (o_ref, dil) in enumerate(zip(o_refs, dils)):
        @pl.when(group == gi)
        def _(gi=gi, o_ref=o_ref, dil=dil):
            rows = tm // dil
            res = _dot(h_ref[gi], w_ref[...])
            for c in range(dil):
                seg = _class_segment(c, dil)
                o_ref[c] = res[seg * rows:(seg + 1) * rows].astype(o_ref.dtype)


def _matmul_norm_classes(x, gain, w, dils, out_dtype):
    m, k = x.shape
    ngroups = len(dils)
    ng = w.shape[1] // ngroups
    tm = _pick(m, (1024, 512, 256))
    tn = _pick(ng, (1024, 512, 256, 128))
    nj = ng // tn
    assert all(dil in (1, ROW_STRIDE, ROW_STRIDE ** 2) for dil in dils)

    def out_spec(gi, dil):
        def index(i, g, j):
            return (0, i, jnp.where(g < gi, 0, jnp.where(g == gi, j, nj - 1)))
        return pl.BlockSpec((dil, tm // dil, tn), index)

    return pl.pallas_call(
        functools.partial(_mm_qkv_kernel, dils=tuple(dils)),
        out_shape=tuple(jax.ShapeDtypeStruct((dil, m // dil, ng), out_dtype) for dil in dils),
        grid=(m // tm, ngroups, nj),
        in_specs=[pl.BlockSpec((tm, k), lambda i, g, j: (i, 0)),
                  pl.BlockSpec((1, k), lambda i, g, j: (0, 0)),
                  pl.BlockSpec((k, tn), lambda i, g, j: (0, g * nj + j))],
        out_specs=tuple(out_spec(gi, dil) for gi, dil in enumerate(dils)),
        scratch_shapes=[pltpu.VMEM((ngroups, tm, k), BF16),
                        pltpu.VMEM((tm, LANES), F32), pltpu.VMEM((tm, LANES), F32)],
        compiler_params=_cparams(("parallel", "arbitrary", "arbitrary")),
        name="mm_norm_qkv",
    )(x, gain, w)


def _mm_res_kernel(a_ref, w_ref, r_ref, *refs):
    o_ref = refs[-1]
    o_ref[...] = r_ref[...] + _dot(a_ref[...], w_ref[...])


def _matmul_residual(a, w, res, row_off=0, out_rows=None, prev=None):
    m, n = res.shape
    k = a.shape[1]
    tm = _pick(math.gcd(m, row_off) if row_off else m, (512, 256, 128))
    tn = _pick(n, (2048, 1024, 512, 256, 128))
    ib = row_off // tm
    in_specs = [pl.BlockSpec((tm, k), lambda i, j: (i + ib, 0)),
                pl.BlockSpec((k, tn), lambda i, j: (0, j)),
                pl.BlockSpec((tm, tn), lambda i, j: (i, j))]
    args = [a, w, res]
    if out_rows is None:
        out_rows, aliases = m, {2: 0}
    elif prev is None:
        aliases = {}
    else:
        in_specs.append(pl.BlockSpec(memory_space=pl.ANY))
        args.append(prev)
        aliases = {3: 0}
    return pl.pallas_call(
        _mm_res_kernel,
        out_shape=jax.ShapeDtypeStruct((out_rows, n), F32),
        grid=(m // tm, n // tn),
        in_specs=in_specs,
        out_specs=pl.BlockSpec((tm, tn), lambda i, j: (i + ib, j)),
        input_output_aliases=aliases,
        compiler_params=_cparams(("parallel", "arbitrary")),
        name="mm_residual",
    )(*args)


def _ffn_kernel(x_ref, g_ref, wg_ref, wu_ref, wd_ref, gf_ref, o_ref, h_ref, *, final_norm):
    j = pl.program_id(1)

    @pl.when(j == 0)
    def _():
        x = x_ref[...]
        h_ref[...] = _rms(x, g_ref[...]).astype(BF16)
        o_ref[...] = x

    h = h_ref[...]
    gate = _dot(h, wg_ref[...])
    up = _dot(h, wu_ref[...])
    act = (gate * _sigmoid(gate) * up).astype(BF16)
    o_ref[...] += _dot(act, wd_ref[...])

    if final_norm:
        @pl.when(j == pl.num_programs(1) - 1)
        def _():
            o_ref[...] = _rms(o_ref[...], gf_ref[...])


def _ffn(x, gain, wg, wu, wd, final_gain, final_norm, rows=None):
    d = x.shape[1]
    start, m = rows if rows is not None else (0, x.shape[0])
    ff = wg.shape[1]
    tm = _pick(math.gcd(m, start) if start else m, (512, 256, 128))
    tf = _pick(ff, (512, 256, 128))
    ib = start // tm
    return pl.pallas_call(
        functools.partial(_ffn_kernel, final_norm=final_norm),
        out_shape=jax.ShapeDtypeStruct((m, d), F32),
        grid=(m // tm, ff // tf),
        in_specs=[pl.BlockSpec((tm, d), lambda i, j: (i + ib, 0)),
                  pl.BlockSpec((1, d), lambda i, j: (0, 0)),
                  pl.BlockSpec((d, tf), lambda i, j: (0, j)),
                  pl.BlockSpec((d, tf), lambda i, j: (0, j)),
                  pl.BlockSpec((tf, d), lambda i, j: (j, 0)),
                  pl.BlockSpec((1, d), lambda i, j: (0, 0))],
        out_specs=pl.BlockSpec((tm, d), lambda i, j: (i, 0)),
        scratch_shapes=[pltpu.VMEM((tm, d), BF16)],
        input_output_aliases={} if rows is not None else {0: 0},
        compiler_params=_cparams(("parallel", "arbitrary")),
        name="ffn",
    )(x, gain, wg, wu, wd, final_gain)


def _shift_kernel(x_ref, xp_ref, xn_ref, g_ref, mu_ref, *refs, tt, n1, t1, t2):
    o_ref = refs[-1]
    pos0, seq_len = _seq_pos(pl.program_id(0) * tt, n1, t1, t2)

    def inv_rms(x):
        return lax.rsqrt(jnp.mean(x * x, axis=-1, keepdims=True) + RMS_EPS)

    inv = inv_rms(x_ref[...])
    inv_before = jnp.where(pos0 == 0, 0.0, inv_rms(xp_ref[7:8, :]))
    inv_after = jnp.where(pos0 + tt == seq_len, 0.0, inv_rms(xn_ref[0:1, :]))
    row = lax.broadcasted_iota(jnp.int32, (tt, 1), 0)
    for k in range(x_ref.shape[1] // LANES):
        cols = slice(k * LANES, (k + 1) * LANES)
        g = g_ref[:, cols]
        h = x_ref[:, cols] * inv * g
        h_prev = jnp.where(row == 0, xp_ref[7:8, cols] * inv_before * g, pltpu.roll(h, 1, 0))
        h_next = jnp.where(row == tt - 1, xn_ref[0:1, cols] * inv_after * g, pltpu.roll(h, tt - 1, 0))
        xx = 0.5 * (h_prev + h_next) - h
        for c in range(6):
            o_ref[c, :, cols] = (h + xx * mu_ref[c:c + 1, cols]).astype(BF16)


def _token_shift(x, gain, mu, geom, row_off=0, out_rows=None, prev=None):
    n, d = x.shape
    n1, t1, t2 = geom
    tt = _pick(math.gcd(math.gcd(t1, t2), row_off) if row_off else math.gcd(t1, t2), (256, 128, 64, 32, 16, 8))
    nb8 = n // 8
    ib = row_off // tt
    in_specs = [pl.BlockSpec((tt, d), lambda i: (i, 0)),
                pl.BlockSpec((8, d), lambda i: (jnp.maximum(i * (tt // 8) - 1, 0), 0)),
                pl.BlockSpec((8, d), lambda i: (jnp.minimum((i + 1) * (tt // 8), nb8 - 1), 0)),
                pl.BlockSpec((1, d), lambda i: (0, 0)),
                pl.BlockSpec((6, d), lambda i: (0, 0))]
    args = [x, x, x, gain, mu]
    aliases = {}
    if prev is not None:
        in_specs.append(pl.BlockSpec(memory_space=pl.ANY))
        args.append(prev)
        aliases = {5: 0}
    return pl.pallas_call(
        functools.partial(_shift_kernel, tt=tt, n1=n1, t1=t1, t2=t2),
        out_shape=jax.ShapeDtypeStruct((6, out_rows or n, d), BF16),
        grid=(n // tt,),
        in_specs=in_specs,
        out_specs=pl.BlockSpec((6, tt, d), lambda i: (0, i + ib, 0)),
        input_output_aliases=aliases,
        compiler_params=_cparams(("parallel",)),
        name="token_shift",
    )(*args)


def _mid_kernel(*refs, has_vres):
    if has_vres:
        (xv_ref, xw_ref, xa_ref, xg_ref, k_ref, v_ref, vf_ref,
         w0_ref, w1_ref, w2_ref, a0_ref, a1_ref, a2_ref, g1_ref, g2_ref, ka_ref, kk_ref,
         v0_ref, v1_ref, v2_ref,
         cum_ref, kd_ref, b_ref, kkn_ref, vo_ref, g_ref) = refs
    else:
        (xv_ref, xw_ref, xa_ref, xg_ref, k_ref, v_ref,
         w0_ref, w1_ref, w2_ref, a0_ref, a1_ref, a2_ref, g1_ref, g2_ref, ka_ref, kk_ref,
         cum_ref, kd_ref, b_ref, kkn_ref, vo_ref, g_ref) = refs
    tm, d = k_ref.shape
    xw = xw_ref[...]
    xa = xa_ref[...]
    k = k_ref[...].astype(F32)
    kka = k * ka_ref[...]
    k_rest = k - kka

    r2 = lax.broadcasted_iota(jnp.int32, (PAIR, PAIR), 0)
    q2 = lax.broadcasted_iota(jnp.int32, (PAIR, PAIR), 1)
    ones_bd = ((r2 < RW_HEAD) == (q2 < RW_HEAD)).astype(BF16)
    for p in range(d // PAIR):
        ln = slice(p * PAIR, (p + 1) * PAIR)
        kk = k[:, ln] * kk_ref[:, ln]
        ss = _dot((kk * kk).astype(BF16), ones_bd)
        kkn_ref[:, ln] = (kk * lax.rsqrt(jnp.maximum(ss, 1e-24))).astype(kkn_ref.dtype)
    kkn = kkn_ref[...].astype(F32)

    w_mid = [jnp.tanh(_dot(xw, w1_ref[z])).astype(BF16) for z in range(2)]
    a_mid = [_dot(xa, a1_ref[z]).astype(BF16) for z in range(2)]
    g_mid = _sigmoid(_dot(xg_ref[...], g1_ref[...])).astype(BF16)
    if has_vres:
        v_mid = _dot(xv_ref[...], v1_ref[...]).astype(BF16)
    w_lora = [_dot(w_mid[z], w2_ref[z]) for z in range(2)]
    a_lora = [_dot(a_mid[z], a2_ref[z]) for z in range(2)]
    g_ref[...] = _dot(g_mid, g2_ref[...]).astype(g_ref.dtype)
    v = v_ref[...].astype(F32)
    if has_vres:
        gate = _sigmoid(v0_ref[...] + _dot(v_mid, v2_ref[...]))
        v = v + (vf_ref[...].astype(F32) - v) * gate
    vo_ref[...] = v.astype(vo_ref.dtype)

    t1 = lax.broadcasted_iota(jnp.int32, (CHUNK, CHUNK), 0)
    s1 = lax.broadcasted_iota(jnp.int32, (CHUNK, CHUNK), 1)
    for z in range(2):
        logw = -EXP_NEG_HALF * _sigmoid(w0_ref[z:z + 1, :] + w_lora[z])
        tri = ((t1 <= s1) if z == 1 else (t1 >= s1)).astype(BF16)
        tri2 = jnp.concatenate([tri, tri], axis=1)
        hi, lo = _split(logw)
        for c in range(tm // CHUNK):
            rows = slice(c * CHUNK, (c + 1) * CHUNK)
            cum_ref[z, rows, :] = _dot(tri2, jnp.concatenate([hi[rows], lo[rows]], axis=0))
        aa = _sigmoid(a0_ref[z:z + 1, :] + a_lora[z])
        b_ref[z] = (kkn * aa).astype(b_ref.dtype)
        kd_ref[z] = (k_rest + kka * aa).astype(kd_ref.dtype)


def _rwkv_mid(xs, rkv, v_first, p, vres):
    _, n, d = xs.shape
    tm = _pick(n, (256, 128, 64))
    has_vres = vres is not None

    def slab(c):
        return pl.BlockSpec((None, tm, d), lambda i, c=c: (c, i, 0))

    def whole(a):
        nd = a.ndim
        return pl.BlockSpec(a.shape, lambda i, nd=nd: (0,) * nd)

    tok = pl.BlockSpec((tm, d), lambda i: (i, 0))
    tok2 = pl.BlockSpec((2, tm, d), lambda i: (0, i, 0))
    weights = [p["w0"], p["w1"], p["w2"], p["a0"], p["a1"], p["a2"], p["g1"], p["g2"], p["k_a"], p["k_k"]]
    args = [xs, xs, xs, xs, rkv, rkv]
    specs = [slab(2), slab(3), slab(4), slab(5), slab(1), slab(2)]
    if has_vres:
        args.append(v_first)
        specs.append(slab(2))
        weights += list(vres)
    args += weights
    specs += [whole(a) for a in weights]
    return pl.pallas_call(
        functools.partial(_mid_kernel, has_vres=has_vres),
        out_shape=(jax.ShapeDtypeStruct((2, n, d), F32),
                   jax.ShapeDtypeStruct((2, n, d), BF16),
                   jax.ShapeDtypeStruct((2, n, d), BF16),
                   jax.ShapeDtypeStruct((n, d), BF16),
                   jax.ShapeDtypeStruct((n, d), BF16),
                   jax.ShapeDtypeStruct((n, d), BF16)),
        grid=(n // tm,),
        in_specs=specs,
        out_specs=(tok2, tok2, tok2, tok, tok, tok),
        compiler_params=_cparams(("parallel",)),
        name="rwkv_mid",
    )(*args)


def _pair_rows(x, head0):
    zero = jnp.zeros_like(x)
    return jnp.concatenate([jnp.where(head0, x, zero), jnp.where(head0, zero, x)], axis=0)


def _wkv1_kernel(r_ref, cum_ref, kd_ref, v_ref, kkn_ref, b_ref,
                 rp_ref, y0_ref, g_ref, h_ref, *, nchunk, npair, unroll, reverse):
    lane = lax.broadcasted_iota(jnp.int32, (CHUNK, PAIR), 1)
    head0 = lane < RW_HEAD
    col = jnp.where(head0, lane, lane - RW_HEAD)
    row = lax.broadcasted_iota(jnp.int32, (CHUNK, PAIR), 0)
    strict = (row < col) if reverse else (row > col)
    incl = (row <= col) if reverse else (row >= col)
    eye = row == col
    first, last = (CHUNK - 1, 0) if reverse else (0, CHUNK - 1)
    bd = functools.partial(_pair_rows, head0=head0)

    def prepare(rows, ln):
        cum = cum_ref[rows, ln]
        before = pltpu.roll(cum, CHUNK - 1 if reverse else 1, 0)
        cum_ex = jnp.where(row == first, 0.0, before)
        kk = kkn_ref[rows, ln].astype(F32)
        b = b_ref[rows, ln].astype(F32)
        kd = kd_ref[rows, ln].astype(F32)
        tot = cum[last:last + 1, :]
        e_neg = jnp.exp(-cum)
        e_tail = jnp.exp(tot - cum)
        w = dict(tot=tot)
        w["at"] = (-kk * jnp.exp(cum_ex)).astype(BF16)
        w["rt"] = r_ref[rows, ln].astype(F32) * jnp.exp(cum)
        w["bk_in"] = jnp.concatenate([bd((b * e_neg).astype(BF16)), bd((kd * e_neg).astype(BF16))], axis=0)
        w["bk_out"] = jnp.concatenate([(b * e_tail).astype(BF16), (kd * e_tail).astype(BF16)], axis=0)
        w["v"] = v_ref[rows, ln]
        return w

    def start(w):
        m1 = _dot_nt(jnp.concatenate([w["at"], w["rt"].astype(BF16)], axis=0), w["bk_in"])
        a_ab = jnp.where(strict, m1[:CHUNK, :PAIR], 0.0)
        w["a_kv"] = jnp.concatenate([jnp.where(strict, m1[:CHUNK, PAIR:], 0.0),
                                     jnp.where(incl, m1[CHUNK:, PAIR:], 0.0)], axis=0).astype(BF16)
        w["a_rb"] = jnp.where(incl, m1[CHUNK:, :PAIR], 0.0).astype(BF16)
        w["pw"] = a_ab.astype(BF16)
        w["inv"] = jnp.where(eye, 1.0, a_ab)

    def square(w):
        w["pw"] = _dot(w["pw"], bd(w["pw"])).astype(BF16)

    def square_and_extend(w):
        both = _dot(jnp.concatenate([w["pw"], w["inv"].astype(BF16)], axis=0), bd(w["pw"]))
        w["pw"] = both[:CHUNK].astype(BF16)
        w["inv"] = w["inv"] + both[CHUNK:]

    def extend(w):
        w["inv"] = w["inv"] + _dot(w["inv"].astype(BF16), bd(w["pw"]))

    def values(w):
        w["av"] = _dot(w["a_kv"], bd(w["v"]))

    def solve(w):
        rhs = jnp.concatenate([bd(w["at"]), bd(w["av"][:CHUNK].astype(BF16))], axis=1)
        w["wu"] = _dot(w["inv"].astype(BF16), rhs).astype(BF16)

    def finish(w):
        rows, ln = w["rows"], w["ln"]
        wu = w["wu"]
        rw = _dot(w["a_rb"], jnp.concatenate([bd(wu[:, :PAIR]), bd(wu[:, PAIR:])], axis=1))
        rp_ref[rows, ln] = (w["rt"] + rw[:, :PAIR]).astype(rp_ref.dtype)
        y0_ref[rows, ln] = (rw[:, PAIR:] + w["av"][CHUNK:]).astype(y0_ref.dtype)
        gfull = _dot_tn(w["bk_out"][:CHUNK], wu[:, :PAIR])
        gdiag = jnp.where(eye, jnp.exp(w["tot"]), 0.0)
        g_ref[rows, ln] = (jnp.where(head0, gfull[:CHUNK], gfull[CHUNK:]) + gdiag).astype(g_ref.dtype)
        hfull = _dot_tn(w["bk_out"], jnp.concatenate([wu[:, PAIR:], w["v"]], axis=0))
        h_ref[rows, ln] = jnp.where(head0, hfull[:CHUNK], hfull[CHUNK:]).astype(h_ref.dtype)

    n_extend = int(math.log2(CHUNK)) - 2
    phases = [start, square] + [square_and_extend] * n_extend + [extend, values, solve, finish]

    def chunk_body(ci, carry):
        items = []
        for u in range(unroll):
            rows = pl.ds(pl.multiple_of((ci * unroll + u) * CHUNK, CHUNK), CHUNK)
            for p in range(npair):
                ln = slice(p * PAIR, (p + 1) * PAIR)
                w = prepare(rows, ln)
                w["rows"], w["ln"] = rows, ln
                items.append(w)
        for phase in phases:
            for w in items:
                phase(w)
        return carry

    lax.fori_loop(0, nchunk // unroll, chunk_body, 0)


def _wkv_stage1(r3, r_slab, cum, kd, v, kkn, b, z, reverse):
    n, d = v.shape
    npair = _pick(d // PAIR, (8, 4, 2, 1))
    lanes = npair * PAIR
    tb = _pick(n, (1024, 512, 256))

    def slab(c):
        return pl.BlockSpec((None, tb, lanes), lambda i, j, c=c: (c, i, j))

    tok = pl.BlockSpec((tb, lanes), lambda i, j: (i, j))
    return pl.pallas_call(
        functools.partial(_wkv1_kernel, nchunk=tb // CHUNK, npair=npair, unroll=WKV_UNROLL, reverse=reverse),
        out_shape=(jax.ShapeDtypeStruct((n, d), BF16),
                   jax.ShapeDtypeStruct((n, d), WKV_MAP_DTYPE),
                   jax.ShapeDtypeStruct((n, d), BF16),
                   jax.ShapeDtypeStruct((n, d), WKV_MAP_DTYPE)),
        grid=(n // tb, d // lanes),
        in_specs=[slab(r_slab), slab(z), slab(z), tok, tok, slab(z)],
        out_specs=(tok, tok, tok, tok),
        compiler_params=_cparams(("parallel", "parallel")),
        name="wkv_stage1_rev" if reverse else "wkv_stage1_fwd",
    )(r3, cum, kd, v, kkn, b)


def _wkv2_kernel(rpf_ref, y0f_ref, gf_ref, hf_ref, rpb_ref, y0b_ref, gb_ref, hb_ref,
                 yf_ref, yb_ref, stf_ref, stb_ref, *, npairs, nblocks, sub, n1, t1, t2):
    i = pl.program_id(0)

    @pl.when(i == 0)
    def _():
        stf_ref[...] = jnp.zeros_like(stf_ref)
        stb_ref[...] = jnp.zeros_like(stb_ref)

    lane = lax.broadcasted_iota(jnp.int32, (CHUNK, PAIR), 1)
    head0 = lane < RW_HEAD
    bd = functools.partial(_pair_rows, head0=head0)
    fwd_refs = (rpf_ref, y0f_ref, gf_ref, hf_ref, yf_ref, stf_ref)
    bwd_refs = (rpb_ref, y0b_ref, gb_ref, hb_ref, yb_ref, stb_ref)
    for s in range(sub):
        cf, cb = s, sub - 1 - s
        pos_f, _ = _seq_pos((i * sub + cf) * CHUNK, n1, t1, t2)
        pos_b, len_b = _seq_pos(((nblocks - 1 - i) * sub + cb) * CHUNK, n1, t1, t2)
        dirs = ((fwd_refs, pl.ds(cf * CHUNK, CHUNK), pos_f == 0),
                (bwd_refs, pl.ds(cb * CHUNK, CHUNK), pos_b + CHUNK == len_b))
        items = [(refs, rows, fresh, slice(p * PAIR, (p + 1) * PAIR))
                 for p in range(npairs) for refs, rows, fresh in dirs]
        results = []
        for (rp_ref, _, g_ref, _, _, st_ref), rows, fresh, ln in items:
            state = jnp.where(fresh, 0.0, st_ref[:, ln])
            hi, lo = _split(state)
            lhs = jnp.concatenate([rp_ref[rows, ln], g_ref[rows, ln]], axis=0)
            results.append(_dot(jnp.concatenate([lhs, lhs], axis=1), jnp.concatenate([bd(hi), bd(lo)], axis=0)))
        for ((_, y0_ref, _, h_ref, y_ref, st_ref), rows, _, ln), res in zip(items, results):
            y_ref[rows, ln] = (y0_ref[rows, ln].astype(F32) + res[:CHUNK]).astype(y_ref.dtype)
            st_ref[:, ln] = res[CHUNK:] + h_ref[rows, ln].astype(F32)


def _wkv_stage2(fwd, bwd, geom):
    n, d = fwd[0].shape
    n1, t1, t2 = geom
    sub = _pick(n // CHUNK, (WKV_SCAN_CHUNKS, 2, 1))
    nblocks = n // (sub * CHUNK)
    tok_f = pl.BlockSpec((sub * CHUNK, d), lambda i: (i, 0))
    tok_b = pl.BlockSpec((sub * CHUNK, d), lambda i: (nblocks - 1 - i, 0))
    return pl.pallas_call(
        functools.partial(_wkv2_kernel, npairs=d // PAIR, nblocks=nblocks, sub=sub, n1=n1, t1=t1, t2=t2),
        out_shape=(jax.ShapeDtypeStruct((n, d), WKV_OUT_DTYPE), jax.ShapeDtypeStruct((n, d), WKV_OUT_DTYPE)),
        grid=(nblocks,),
        in_specs=[tok_f] * 4 + [tok_b] * 4,
        out_specs=(tok_f, tok_b),
        scratch_shapes=[pltpu.VMEM((CHUNK, d), F32), pltpu.VMEM((CHUNK, d), F32)],
        compiler_params=_cparams(("arbitrary",)),
        name="wkv_stage2",
    )(*fwd, *bwd)


def _post_kernel(yf_ref, yb_ref, r_ref, kd_ref, v_ref, g_ref, rk_ref, lw_ref, lb_ref, o_ref, *, ngroups):
    r2 = lax.broadcasted_iota(jnp.int32, (PAIR, PAIR), 0)
    q2 = lax.broadcasted_iota(jnp.int32, (PAIR, PAIR), 1)
    same = (r2 < RW_HEAD) == (q2 < RW_HEAD)
    ones_bd = same.astype(BF16)
    mean_bd = (same.astype(F32) * (1.0 / RW_HEAD)).astype(BF16)
    mean_bd2 = jnp.concatenate([mean_bd, mean_bd], axis=0)
    for p in range(ngroups):
        ln = slice(p * PAIR, (p + 1) * PAIR)
        yf, yb = yf_ref[:, ln], yb_ref[:, ln]
        y = yf.astype(F32) + yb.astype(F32)
        assert yf.dtype == BF16 and yb.dtype == BF16
        mean = _dot(jnp.concatenate([yf, yb], axis=1), mean_bd2)
        dev = y - mean
        var = _dot((dev * dev).astype(BF16), mean_bd)
        yn = dev * lax.rsqrt(var + LNX_EPS) * lw_ref[:, ln] + lb_ref[:, ln]
        kd = kd_ref[0, :, ln].astype(F32) + kd_ref[1, :, ln].astype(F32)
        rkd = (r_ref[:, ln].astype(F32) * kd * rk_ref[:, ln]).astype(BF16)
        bonus = _dot(rkd, ones_bd) * v_ref[:, ln].astype(F32)
        o_ref[:, ln] = ((yn + bonus) * g_ref[:, ln].astype(F32)).astype(o_ref.dtype)


def _rwkv_post(yf, yb, rkv, kd, v, g, r_k, lnx_w, lnx_b):
    n, d = yf.shape
    tm = _pick(n, (512, 256, 128, 64, 32, 16, 8))
    tok = pl.BlockSpec((tm, d), lambda i: (i, 0))
    row = pl.BlockSpec((1, d), lambda i: (0, 0))
    return pl.pallas_call(
        functools.partial(_post_kernel, ngroups=d // PAIR),
        out_shape=jax.ShapeDtypeStruct((n, d), BF16),
        grid=(n // tm,),
        in_specs=[tok, tok, pl.BlockSpec((None, tm, d), lambda i: (0, i, 0)),
                  pl.BlockSpec((2, tm, d), lambda i: (0, i, 0)), tok, tok, row, row, row],
        out_specs=tok,
        compiler_params=_cparams(("parallel",)),
        name="rwkv_post",
    )(yf, yb, rkv, kd, v, g, r_k, lnx_w, lnx_b)


def _band_bias(tq, dil, nheads):
    nk = tq + 2 * BAND
    dist = np.abs(np.arange(nk)[None, :] - BAND - np.arange(tq)[:, None])
    slopes = np.exp2(-8.0 * np.arange(1, nheads + 1) / nheads)
    bias = -slopes[:, None, None] * (dil * dist)[None] * LOG2E
    return jnp.asarray(np.where((dist <= BAND)[None], bias, NEG_BIG), dtype=F32)


def _band_kernel(q_ref, kp_ref, kc_ref, kn_ref, vp_ref, vc_ref, vn_ref, bias_ref, o_ref, lse_ref,
                 *, tq, nsub, nheads, n1l, l1, l2):
    tb = tq * nsub
    pos0, seq_len = _seq_pos(pl.program_id(0) * tb, n1l, l1, l2)
    nk = tq + 2 * BAND
    col = lax.broadcasted_iota(jnp.int32, (1, nk), 1)
    head_lane = lax.broadcasted_iota(jnp.int32, (1, ATT_HEAD), 1)
    scale2 = ATT_HEAD ** -0.5 * LOG2E
    lanes = [slice(h * ATT_HEAD, (h + 1) * ATT_HEAD) for h in range(nheads)]

    def window(p_ref, c_ref, n_ref, sub, ln):
        lo, hi = sub * tq - BAND, sub * tq + tq + BAND
        parts = []
        if lo < 0:
            parts.append(p_ref[:, ln])
        parts.append(c_ref[max(lo, 0):min(hi, tb), ln])
        if hi > tb:
            parts.append(n_ref[:, ln])
        return jnp.concatenate(parts, axis=0) if len(parts) > 1 else parts[0]

    no_before = jnp.where((col < BAND) & (pos0 == 0), NEG_BIG, 0.0)
    no_after = jnp.where((col >= tq + BAND) & (pos0 + tb == seq_len), NEG_BIG, 0.0)
    blocks = [slice(c, min(c + LANES, nk)) for c in range(0, nk, LANES)]

    def mask_ends(s, sub):
        parts = [s[:, blk] for blk in blocks]
        if sub == 0:
            parts[0] = parts[0] + no_before[:, blocks[0]]
        if sub == nsub - 1:
            parts[-1] = parts[-1] + no_after[:, blocks[-1]]
        return jnp.concatenate(parts, axis=1)

    lse_all = [jnp.zeros((tq, ATT_HEAD), F32) for _ in range(nsub)]
    items = [(sub, h) for sub in range(nsub) for h in range(nheads)]
    for i0 in range(0, len(items), HEAD_BATCH):
        batch = items[i0:i0 + HEAD_BATCH]
        scores, probs, dens = {}, {}, {}
        for sub, h in batch:
            q = q_ref[sub * tq:(sub + 1) * tq, lanes[h]]
            keys = window(kp_ref, kc_ref, kn_ref, sub, lanes[h])
            scores[sub, h] = mask_ends(_dot_nt(q, keys) * scale2 + bias_ref[h], sub)
        for sub, h in batch:
            m = jnp.max(scores[sub, h], axis=-1, keepdims=True)
            probs[sub, h] = jnp.exp2(scores[sub, h] - m)
            dens[sub, h] = jnp.sum(probs[sub, h], axis=-1, keepdims=True)
            lse_all[sub] = lse_all[sub] + jnp.where(head_lane == h, (m + jnp.log2(dens[sub, h])) * LN2, 0.0)
        for sub, h in batch:
            vals = window(vp_ref, vc_ref, vn_ref, sub, lanes[h])
            out = _dot(probs[sub, h].astype(BF16), vals) / dens[sub, h]
            o_ref[sub * tq:(sub + 1) * tq, lanes[h]] = out.astype(o_ref.dtype)
    for sub in range(nsub):
        lse_ref[sub * tq:(sub + 1) * tq, :] = lse_all[sub]


def _band_attention(qkv, group, geom, d):
    dil, nl, _ = qkv.shape
    n1, t1, t2 = geom
    n1l, l1, l2 = n1 // dil, t1 // dil, t2 // dil
    tq = _pick(math.gcd(l1, l2), (128, 64))
    nsub = _pick(math.gcd(l1, l2) // tq, (BAND_TILES, 2, 1))
    tb = tq * nsub
    hb = tb // BAND
    nhalo = nl // BAND
    nheads = d // ATT_HEAD

    def cur(s):
        return pl.BlockSpec((None, tb, d), lambda i, c, s=s: (c, i, s))

    def before(s):
        return pl.BlockSpec((None, BAND, d), lambda i, c, s=s: (c, jnp.maximum(i * hb - 1, 0), s))

    def after(s):
        return pl.BlockSpec((None, BAND, d), lambda i, c, s=s: (c, jnp.minimum((i + 1) * hb, nhalo - 1), s))

    return pl.pallas_call(
        functools.partial(_band_kernel, tq=tq, nsub=nsub, nheads=nheads, n1l=n1l, l1=l1, l2=l2),
        out_shape=(jax.ShapeDtypeStruct((dil, nl, d), BF16),
                   jax.ShapeDtypeStruct((dil, nl, ATT_HEAD), F32)),
        grid=(nl // tb, dil),
        in_specs=[cur(0), before(1), cur(1), after(1), before(2), cur(2), after(2),
                  pl.BlockSpec((nheads, tq, tq + 2 * BAND), lambda i, c: (0, 0, 0))],
        out_specs=(pl.BlockSpec((None, tb, d), lambda i, c: (c, i, 0)),
                   pl.BlockSpec((None, tb, ATT_HEAD), lambda i, c: (c, i, 0))),
        compiler_params=_cparams(("parallel", "parallel")),
        name=f"band_attention_g{group}",
    )(qkv, qkv, qkv, qkv, qkv, qkv, qkv, _band_bias(tq, dil, nheads))


def _combine_kernel(o0_ref, o1_ref, o2_ref, l0_ref, l1_ref, l2_ref, e_ref, out_ref, lse_scr, o_scr, *, dils):
    tm = out_ref.shape[0]

    def token_order(ref, scr, dil):
        if dil == 1:
            return ref[0].astype(F32)
        rows = tm // dil
        ngrp = ref.shape[2] // LANES
        for k in range(ngrp):
            for c in range(dil):
                scr[k, pl.ds(c, rows, stride=dil), :] = ref[c, :, k * LANES:(k + 1) * LANES].astype(F32)
        return jnp.concatenate([scr[k] for k in range(ngrp)], axis=1)

    lses = [token_order(l_ref, lse_scr.at[pl.ds(g, 1)], dil)
            for g, (l_ref, dil) in enumerate(zip((l0_ref, l1_ref, l2_ref), dils))]
    m = jnp.maximum(jnp.maximum(lses[0], lses[1]), lses[2])
    ws = [jnp.exp(l - m) for l in lses]
    tot = ws[0] + ws[1] + ws[2]
    e = e_ref[...]
    acc = None
    for w, o_ref, dil in zip(ws, (o0_ref, o1_ref, o2_ref), dils):
        hi, lo = _split(w / tot)
        term = _dot(jnp.concatenate([hi, lo], axis=1), e) * token_order(o_ref, o_scr, dil)
        acc = term if acc is None else acc + term
    out_ref[...] = acc.astype(out_ref.dtype)


def _combine(os_, lses, d):
    dils = tuple(o.shape[0] for o in os_)
    n = os_[0].shape[0] * os_[0].shape[1]
    tm = _pick(n, (256,))
    head_of_lane = jnp.arange(d, dtype=jnp.int32) // ATT_HEAD
    expand = (jnp.arange(ATT_HEAD, dtype=jnp.int32)[:, None] == head_of_lane[None, :]).astype(BF16)
    expand = jnp.concatenate([expand, expand], axis=0)

    def classes(dil, width):
        return pl.BlockSpec((dil, tm // dil, width), lambda i: (0, i, 0))

    return pl.pallas_call(
        functools.partial(_combine_kernel, dils=dils),
        out_shape=jax.ShapeDtypeStruct((n, d), BF16),
        grid=(n // tm,),
        in_specs=[classes(dil, d) for dil in dils] + [classes(dil, ATT_HEAD) for dil in dils]
                 + [pl.BlockSpec((2 * ATT_HEAD, d), lambda i: (0, 0))],
        out_specs=pl.BlockSpec((tm, d), lambda i: (i, 0)),
        scratch_shapes=[pltpu.VMEM((len(dils), tm, LANES), F32), pltpu.VMEM((d // LANES, tm, LANES), F32)],
        compiler_params=_cparams(("parallel",)),
        name="attention_combine",
    )(*os_, *lses, expand)


def _rwkv_layer(x, gain, p, v_first, vres, geom):
    parts = None if not isinstance(x, list) else x
    if parts is None:
        xs = _token_shift(x, gain, p["mu"], geom)
    else:
        total = sum(xp.shape[0] for xp, _ in parts)
        xs, off = None, 0
        for xp, tp in parts:
            xs = _token_shift(xp, gain, p["mu"], (xp.shape[0], tp, tp), row_off=off, out_rows=total, prev=xs)
            off += xp.shape[0]
    rkv = _matmul_batched(xs, p["w_rkv"], 3, RKV_DTYPE)
    if v_first is None:
        v_first = rkv
    cum, kd, b, kkn, v, g = _rwkv_mid(xs, rkv, v_first, p, vres)
    maps = [_wkv_stage1(rkv, 0, cum, kd, v, kkn, b, z, reverse=(z == 1)) for z in range(2)]
    yf, yb = _wkv_stage2(maps[0], maps[1], geom)
    o = _rwkv_post(yf, yb, rkv, kd, v, g, p["r_k"], p["lnx_w"], p["lnx_b"])
    if parts is None:
        return _matmul_residual(o, p["w_o"], x), v_first
    out, off = None, 0
    for xp, _ in parts:
        out = _matmul_residual(o, p["w_o"], xp, row_off=off, out_rows=o.shape[0], prev=out)
        off += xp.shape[0]
    return out, v_first


def _attention_layer(x, gain, w_qkv, w_o, geom):
    d = x.shape[1]
    assert all((window // 2) // dil == BAND for window, dil in DIL_PATTERNS)
    qkvs = _matmul_norm_classes(x, gain, w_qkv, [dil for _, dil in DIL_PATTERNS], BF16)
    outs, lses = zip(*[_band_attention(qkv, gi, geom, d) for gi, qkv in enumerate(qkvs)])
    return _matmul_residual(_combine(outs, lses, d), w_o, x)


def kernel(x_prompt, x_sample, ln1, ln2, ln_f, rw_mu, rw_w_rkv, rw_w0, rw_w1, rw_w2, rw_a0, rw_a1, rw_a2, rw_v0, rw_v1, rw_v2, rw_g1, rw_g2, rw_k_k, rw_k_a, rw_r_k, rw_lnx_w, rw_lnx_b, rw_w_o, at_w_qkv, at_w_o, ffn_w_gate, ffn_w_up, ffn_w_down):
    b1, t1, d = x_prompt.shape
    b2, t2, _ = x_sample.shape
    n1, n2 = b1 * t1, b2 * t2
    geom = (n1, t1, t2)
    depth = ln1.shape[0]
    assert d % PAIR == 0 and d % ATT_HEAD == 0
    max_dil = max(dil for _, dil in DIL_PATTERNS)
    assert t1 % (max_dil * BAND) == 0 and t2 % (max_dil * BAND) == 0

    assert depth >= 1
    x = [(x_prompt.reshape(n1, d), t1), (x_sample.reshape(n2, d), t2)]
    bf = lambda w: w.astype(BF16)
    row = lambda w: w.reshape(1, d)

    v_first = None
    for i in range(depth):
        j = i // 2
        if i % 2 == 0:
            p = dict(mu=rw_mu[j], w_rkv=bf(rw_w_rkv[j]), w0=rw_w0[j], w1=bf(rw_w1[j]), w2=bf(rw_w2[j]),
                     a0=rw_a0[j], a1=bf(rw_a1[j]), a2=bf(rw_a2[j]), g1=bf(rw_g1[j]), g2=bf(rw_g2[j]),
                     k_k=row(rw_k_k[j]), k_a=row(rw_k_a[j]), r_k=rw_r_k[j].reshape(1, d),
                     lnx_w=row(rw_lnx_w[j]), lnx_b=row(rw_lnx_b[j]), w_o=bf(rw_w_o[j]))
            vres = None if j == 0 else (row(rw_v0[j - 1]), bf(rw_v1[j - 1]), bf(rw_v2[j - 1]))
            x, v_first = _rwkv_layer(x, row(ln1[i]), p, v_first, vres, geom)
        else:
            w_qkv = bf(at_w_qkv[j]).reshape(d, -1)
            x = _attention_layer(x, row(ln1[i]), w_qkv, bf(at_w_o[j]), geom)
        ffn_w = (row(ln2[i]), bf(ffn_w_gate[i]), bf(ffn_w_up[i]), bf(ffn_w_down[i]), row(ln_f))
        if i < depth - 1:
            x = _ffn(x, *ffn_w, final_norm=False)
    y_prompt = _ffn(x, *ffn_w, final_norm=True, rows=(0, n1))
    y_sample = _ffn(x, *ffn_w, final_norm=True, rows=(n1, n2))
    return y_prompt.reshape(b1, t1, d), y_sample.reshape(b2, t2, d)
```

```python
import functools
import math

import numpy as np

import jax
import jax.numpy as jnp
from jax import lax
from jax.experimental import pallas as pl
from jax.experimental.pallas import tpu as pltpu

F32 = jnp.float32
BF16 = jnp.bfloat16

RW_HEAD = 64
PAIR = 2 * RW_HEAD
ATT_HEAD = 128
LANES = 128
ROW_STRIDE = 4
DIL_PATTERNS = ((128, 1), (512, 4), (2048, 16))
BAND = 64
LNX_EPS = 64e-5
RMS_EPS = 1e-6
CHUNK = 64
NEG_BIG = -1e30
LOG2E = math.log2(math.e)
LN2 = math.log(2.0)
EXP_NEG_HALF = math.exp(-0.5)
RKV_DTYPE = BF16
WKV_MAP_DTYPE = BF16
WKV_OUT_DTYPE = BF16
WKV_SCAN_CHUNKS = 8
BAND_TILES = 4
HEAD_BATCH = 8
WKV_UNROLL = 4
V7X_VMEM_LIMIT_BYTES = 56 * 1024 * 1024


def _pick(n, candidates):
    for c in candidates:
        if n % c == 0:
            return c
    raise ValueError(f"no tile in {candidates} divides {n}")


def _cparams(sem):
    return pltpu.CompilerParams(dimension_semantics=sem, vmem_limit_bytes=V7X_VMEM_LIMIT_BYTES)


def _dot(a, b):
    return jnp.dot(a, b, preferred_element_type=F32)


def _dot_nt(a, b):
    return lax.dot_general(a, b, (((1,), (1,)), ((), ())), preferred_element_type=F32)


def _dot_tn(a, b):
    return lax.dot_general(a, b, (((0,), (0,)), ((), ())), preferred_element_type=F32)


def _split(x):
    hi = x.astype(BF16)
    lo = (x - hi.astype(F32)).astype(BF16)
    return hi, lo


def _sigmoid(x):
    return 1.0 / (1.0 + jnp.exp2(x * -LOG2E))


def _rms(x, g):
    return x * lax.rsqrt(jnp.mean(x * x, axis=-1, keepdims=True) + RMS_EPS) * g


def _seq_pos(idx, n1, t1, t2):
    first = idx < n1
    pos = jnp.where(first, lax.rem(idx, t1), lax.rem(idx - n1, t2))
    return pos, jnp.where(first, t1, t2)


def _mm_kernel(a_ref, w_ref, o_ref):
    o_ref[...] = _dot(a_ref[...], w_ref[...]).astype(o_ref.dtype)


def _matmul_batched(a, w, groups, out_dtype):
    _, m, k = a.shape
    n = w.shape[2]
    tm = _pick(m, (1024, 512, 256, 128))
    tn = _pick(n, (1024, 512, 256, 128))
    return pl.pallas_call(
        _mm_kernel,
        out_shape=jax.ShapeDtypeStruct((groups, m, n), out_dtype),
        grid=(groups, m // tm, n // tn),
        in_specs=[pl.BlockSpec((None, tm, k), lambda g, i, j: (g, i, 0)),
                  pl.BlockSpec((None, k, tn), lambda g, i, j: (g, 0, j))],
        out_specs=pl.BlockSpec((None, tm, tn), lambda g, i, j: (g, i, j)),
        compiler_params=_cparams(("parallel", "parallel", "arbitrary")),
        name="mm_batched",
    )(a, w)


def _class_segment(c, dil):
    return c if dil <= ROW_STRIDE else ROW_STRIDE * (c % ROW_STRIDE) + c // ROW_STRIDE


def _store_rows_by_class(hk, h_ref, cols, dils, a_ref, b_ref):
    tm = hk.shape[0]
    quarter = tm // ROW_STRIDE
    if any(dil > 1 for dil in dils):
        a_ref[...] = hk
        for c0 in range(ROW_STRIDE):
            b_ref[c0 * quarter:(c0 + 1) * quarter, :] = a_ref[pl.ds(c0, quarter, stride=ROW_STRIDE), :]
    for gi, dil in enumerate(dils):
        if dil == 1:
            h_ref[gi, :, cols] = hk.astype(BF16)
        elif dil == ROW_STRIDE:
            h_ref[gi, :, cols] = b_ref[...].astype(BF16)
        else:
            rows = tm // dil
            for c0 in range(ROW_STRIDE):
                for b in range(ROW_STRIDE):
                    seg = ROW_STRIDE * c0 + b
                    h_ref[gi, seg * rows:(seg + 1) * rows, cols] = b_ref[
                        pl.ds(c0 * quarter + b, rows, stride=ROW_STRIDE), :].astype(BF16)


def _mm_qkv_kernel(x_ref, g_ref, w_ref, *refs, dils):
    ng = len(dils)
    o_refs, h_ref, a_ref, b_ref = refs[:ng], refs[ng], refs[ng + 1], refs[ng + 2]
    tm, k = x_ref.shape
    group = pl.program_id(1)

    @pl.when((group == 0) & (pl.program_id(2) == 0))
    def _():
        x = x_ref[...]
        inv = lax.rsqrt(jnp.mean(x * x, axis=-1, keepdims=True) + RMS_EPS)
        for kk in range(k // LANES):
            cols = slice(kk * LANES, (kk + 1) * LANES)
            hk = x_ref[:, cols] * inv * g_ref[:, cols]
            _store_rows_by_class(hk, h_ref, cols, dils, a_ref, b_ref)

    for gi, (o_ref, dil) in enumerate(zip(o_refs, dils)):
        @pl.when(group == gi)
        def _(gi=gi, o_ref=o_ref, dil=dil):
            rows = tm // dil
            res = _dot(h_ref[gi], w_ref[...])
            for c in range(dil):
                seg = _class_segment(c, dil)
                o_ref[c] = res[seg * rows:(seg + 1) * rows].astype(o_ref.dtype)


def _matmul_norm_classes(x, gain, w, dils, out_dtype):
    m, k = x.shape
    ngroups = len(dils)
    ng = w.shape[1] // ngroups
    tm = _pick(m, (1024, 512, 256))
    tn = _pick(ng, (1024, 512, 256, 128))
    nj = ng // tn
    assert all(dil in (1, ROW_STRIDE, ROW_STRIDE ** 2) for dil in dils)

    def out_spec(gi, dil):
        def index(i, g, j):
            return (0, i, jnp.where(g < gi, 0, jnp.where(g == gi, j, nj - 1)))
        return pl.BlockSpec((dil, tm // dil, tn), index)

    return pl.pallas_call(
        functools.partial(_mm_qkv_kernel, dils=tuple(dils)),
        out_shape=tuple(jax.ShapeDtypeStruct((dil, m // dil, ng), out_dtype) for dil in dils),
        grid=(m // tm, ngroups, nj),
        in_specs=[pl.BlockSpec((tm, k), lambda i, g, j: (i, 0)),
                  pl.BlockSpec((1, k), lambda i, g, j: (0, 0)),
                  pl.BlockSpec((k, tn), lambda i, g, j: (0, g * nj + j))],
        out_specs=tuple(out_spec(gi, dil) for gi, dil in enumerate(dils)),
        scratch_shapes=[pltpu.VMEM((ngroups, tm, k), BF16),
                        pltpu.VMEM((tm, LANES), F32), pltpu.VMEM((tm, LANES), F32)],
        compiler_params=_cparams(("parallel", "arbitrary", "arbitrary")),
        name="mm_norm_qkv",
    )(x, gain, w)


def _mm_res_kernel(a_ref, w_ref, r_ref, *refs):
    o_ref = refs[-1]
    o_ref[...] = r_ref[...] + _dot(a_ref[...], w_ref[...])


def _matmul_residual(a, w, res, row_off=0, out_rows=None, prev=None):
    m, n = res.shape
    k = a.shape[1]
    tm = _pick(math.gcd(m, row_off) if row_off else m, (512, 256, 128))
    tn = _pick(n, (2048, 1024, 512, 256, 128))
    ib = row_off // tm
    in_specs = [pl.BlockSpec((tm, k), lambda i, j: (i + ib, 0)),
                pl.BlockSpec((k, tn), lambda i, j: (0, j)),
                pl.BlockSpec((tm, tn), lambda i, j: (i, j))]
    args = [a, w, res]
    if out_rows is None:
        out_rows, aliases = m, {2: 0}
    elif prev is None:
        aliases = {}
    else:
        in_specs.append(pl.BlockSpec(memory_space=pl.ANY))
        args.append(prev)
        aliases = {3: 0}
    return pl.pallas_call(
        _mm_res_kernel,
        out_shape=jax.ShapeDtypeStruct((out_rows, n), F32),
        grid=(m // tm, n // tn),
        in_specs=in_specs,
        out_specs=pl.BlockSpec((tm, tn), lambda i, j: (i + ib, j)),
        input_output_aliases=aliases,
        compiler_params=_cparams(("parallel", "arbitrary")),
        name="mm_residual",
    )(*args)


def _ffn_kernel(x_ref, g_ref, wg_ref, wu_ref, wd_ref, gf_ref, o_ref, h_ref, *, final_norm):
    j = pl.program_id(1)

    @pl.when(j == 0)
    def _():
        x = x_ref[...]
        h_ref[...] = _rms(x, g_ref[...]).astype(BF16)
        o_ref[...] = x

    h = h_ref[...]
    gate = _dot(h, wg_ref[...])
    up = _dot(h, wu_ref[...])
    act = (gate * _sigmoid(gate) * up).astype(BF16)
    o_ref[...] += _dot(act, wd_ref[...])

    if final_norm:
        @pl.when(j == pl.num_programs(1) - 1)
        def _():
            o_ref[...] = _rms(o_ref[...], gf_ref[...])


def _ffn(x, gain, wg, wu, wd, final_gain, final_norm, rows=None):
    d = x.shape[1]
    start, m = rows if rows is not None else (0, x.shape[0])
    ff = wg.shape[1]
    tm = _pick(math.gcd(m, start) if start else m, (512, 256, 128))
    tf = _pick(ff, (512, 256, 128))
    ib = start // tm
    return pl.pallas_call(
        functools.partial(_ffn_kernel, final_norm=final_norm),
        out_shape=jax.ShapeDtypeStruct((m, d), F32),
        grid=(m // tm, ff // tf),
        in_specs=[pl.BlockSpec((tm, d), lambda i, j: (i + ib, 0)),
                  pl.BlockSpec((1, d), lambda i, j: (0, 0)),
                  pl.BlockSpec((d, tf), lambda i, j: (0, j)),
                  pl.BlockSpec((d, tf), lambda i, j: (0, j)),
                  pl.BlockSpec((tf, d), lambda i, j: (j, 0)),
                  pl.BlockSpec((1, d), lambda i, j: (0, 0))],
        out_specs=pl.BlockSpec((tm, d), lambda i, j: (i, 0)),
        scratch_shapes=[pltpu.VMEM((tm, d), BF16)],
        input_output_aliases={} if rows is not None else {0: 0},
        compiler_params=_cparams(("parallel", "arbitrary")),
        name="ffn",
    )(x, gain, wg, wu, wd, final_gain)


def _shift_kernel(x_ref, xp_ref, xn_ref, g_ref, mu_ref, *refs, tt, n1, t1, t2):
    o_ref = refs[-1]
    pos0, seq_len = _seq_pos(pl.program_id(0) * tt, n1, t1, t2)

    def inv_rms(x):
        return lax.rsqrt(jnp.mean(x * x, axis=-1, keepdims=True) + RMS_EPS)

    inv = inv_rms(x_ref[...])
    inv_before = jnp.where(pos0 == 0, 0.0, inv_rms(xp_ref[7:8, :]))
    inv_after = jnp.where(pos0 + tt == seq_len, 0.0, inv_rms(xn_ref[0:1, :]))
    row = lax.broadcasted_iota(jnp.int32, (tt, 1), 0)
    for k in range(x_ref.shape[1] // LANES):
        cols = slice(k * LANES, (k + 1) * LANES)
        g = g_ref[:, cols]
        h = x_ref[:, cols] * inv * g
        h_prev = jnp.where(row == 0, xp_ref[7:8, cols] * inv_before * g, pltpu.roll(h, 1, 0))
        h_next = jnp.where(row == tt - 1, xn_ref[0:1, cols] * inv_after * g, pltpu.roll(h, tt - 1, 0))
        xx = 0.5 * (h_prev + h_next) - h
        for c in range(6):
            o_ref[c, :, cols] = (h + xx * mu_ref[c:c + 1, cols]).astype(BF16)


def _token_shift(x, gain, mu, geom, row_off=0, out_rows=None, prev=None):
    n, d = x.shape
    n1, t1, t2 = geom
    tt = _pick(math.gcd(math.gcd(t1, t2), row_off) if row_off else math.gcd(t1, t2), (256, 128, 64, 32, 16, 8))
    nb8 = n // 8
    ib = row_off // tt
    in_specs = [pl.BlockSpec((tt, d), lambda i: (i, 0)),
                pl.BlockSpec((8, d), lambda i: (jnp.maximum(i * (tt // 8) - 1, 0), 0)),
                pl.BlockSpec((8, d), lambda i: (jnp.minimum((i + 1) * (tt // 8), nb8 - 1), 0)),
                pl.BlockSpec((1, d), lambda i: (0, 0)),
                pl.BlockSpec((6, d), lambda i: (0, 0))]
    args = [x, x, x, gain, mu]
    aliases = {}
    if prev is not None:
        in_specs.append(pl.BlockSpec(memory_space=pl.ANY))
        args.append(prev)
        aliases = {5: 0}
    return pl.pallas_call(
        functools.partial(_shift_kernel, tt=tt, n1=n1, t1=t1, t2=t2),
        out_shape=jax.ShapeDtypeStruct((6, out_rows or n, d), BF16),
        grid=(n // tt,),
        in_specs=in_specs,
        out_specs=pl.BlockSpec((6, tt, d), lambda i: (0, i + ib, 0)),
        input_output_aliases=aliases,
        compiler_params=_cparams(("parallel",)),
        name="token_shift",
    )(*args)


def _mid_kernel(*refs, has_vres):
    if has_vres:
        (xv_ref, xw_ref, xa_ref, xg_ref, k_ref, v_ref, vf_ref,
         w0_ref, w1_ref, w2_ref, a0_ref, a1_ref, a2_ref, g1_ref, g2_ref, ka_ref, kk_ref,
         v0_ref, v1_ref, v2_ref,
         cum_ref, kd_ref, b_ref, kkn_ref, vo_ref, g_ref) = refs
    else:
        (xv_ref, xw_ref, xa_ref, xg_ref, k_ref, v_ref,
         w0_ref, w1_ref, w2_ref, a0_ref, a1_ref, a2_ref, g1_ref, g2_ref, ka_ref, kk_ref,
         cum_ref, kd_ref, b_ref, kkn_ref, vo_ref, g_ref) = refs
    tm, d = k_ref.shape
    xw = xw_ref[...]
    xa = xa_ref[...]
    k = k_ref[...].astype(F32)
    kka = k * ka_ref[...]
    k_rest = k - kka

    r2 = lax.broadcasted_iota(jnp.int32, (PAIR, PAIR), 0)
    q2 = lax.broadcasted_iota(jnp.int32, (PAIR, PAIR), 1)
    ones_bd = ((r2 < RW_HEAD) == (q2 < RW_HEAD)).astype(BF16)
    for p in range(d // PAIR):
        ln = slice(p * PAIR, (p + 1) * PAIR)
        kk = k[:, ln] * kk_ref[:, ln]
        ss = _dot((kk * kk).astype(BF16), ones_bd)
        kkn_ref[:, ln] = (kk * lax.rsqrt(jnp.maximum(ss, 1e-24))).astype(kkn_ref.dtype)
    kkn = kkn_ref[...].astype(F32)

    w_mid = [jnp.tanh(_dot(xw, w1_ref[z])).astype(BF16) for z in range(2)]
    a_mid = [_dot(xa, a1_ref[z]).astype(BF16) for z in range(2)]
    g_mid = _sigmoid(_dot(xg_ref[...], g1_ref[...])).astype(BF16)
    if has_vres:
        v_mid = _dot(xv_ref[...], v1_ref[...]).astype(BF16)
    w_lora = [_dot(w_mid[z], w2_ref[z]) for z in range(2)]
    a_lora = [_dot(a_mid[z], a2_ref[z]) for z in range(2)]
    g_ref[...] = _dot(g_mid, g2_ref[...]).astype(g_ref.dtype)
    v = v_ref[...].astype(F32)
    if has_vres:
        gate = _sigmoid(v0_ref[...] + _dot(v_mid, v2_ref[...]))
        v = v + (vf_ref[...].astype(F32) - v) * gate
    vo_ref[...] = v.astype(vo_ref.dtype)

    t1 = lax.broadcasted_iota(jnp.int32, (CHUNK, CHUNK), 0)
    s1 = lax.broadcasted_iota(jnp.int32, (CHUNK, CHUNK), 1)
    for z in range(2):
        logw = -EXP_NEG_HALF * _sigmoid(w0_ref[z:z + 1, :] + w_lora[z])
        tri = ((t1 <= s1) if z == 1 else (t1 >= s1)).astype(BF16)
        tri2 = jnp.concatenate([tri, tri], axis=1)
        hi, lo = _split(logw)
        for c in range(tm // CHUNK):
            rows = slice(c * CHUNK, (c + 1) * CHUNK)
            cum_ref[z, rows, :] = _dot(tri2, jnp.concatenate([hi[rows], lo[rows]], axis=0))
        aa = _sigmoid(a0_ref[z:z + 1, :] + a_lora[z])
        b_ref[z] = (kkn * aa).astype(b_ref.dtype)
        kd_ref[z] = (k_rest + kka * aa).astype(kd_ref.dtype)


def _rwkv_mid(xs, rkv, v_first, p, vres):
    _, n, d = xs.shape
    tm = _pick(n, (256, 128, 64))
    has_vres = vres is not None

    def slab(c):
        return pl.BlockSpec((None, tm, d), lambda i, c=c: (c, i, 0))

    def whole(a):
        nd = a.ndim
        return pl.BlockSpec(a.shape, lambda i, nd=nd: (0,) * nd)

    tok = pl.BlockSpec((tm, d), lambda i: (i, 0))
    tok2 = pl.BlockSpec((2, tm, d), lambda i: (0, i, 0))
    weights = [p["w0"], p["w1"], p["w2"], p["a0"], p["a1"], p["a2"], p["g1"], p["g2"], p["k_a"], p["k_k"]]
    args = [xs, xs, xs, xs, rkv, rkv]
    specs = [slab(2), slab(3), slab(4), slab(5), slab(1), slab(2)]
    if has_vres:
        args.append(v_first)
        specs.append(slab(2))
        weights += list(vres)
    args += weights
    specs += [whole(a) for a in weights]
    return pl.pallas_call(
        functools.partial(_mid_kernel, has_vres=has_vres),
        out_shape=(jax.ShapeDtypeStruct((2, n, d), F32),
                   jax.ShapeDtypeStruct((2, n, d), BF16),
                   jax.ShapeDtypeStruct((2, n, d), BF16),
                   jax.ShapeDtypeStruct((n, d), BF16),
                   jax.ShapeDtypeStruct((n, d), BF16),
                   jax.ShapeDtypeStruct((n, d), BF16)),
        grid=(n // tm,),
        in_specs=specs,
        out_specs=(tok2, tok2, tok2, tok, tok, tok),
        compiler_params=_cparams(("parallel",)),
        name="rwkv_mid",
    )(*args)


def _pair_rows(x, head0):
    zero = jnp.zeros_like(x)
    return jnp.concatenate([jnp.where(head0, x, zero), jnp.where(head0, zero, x)], axis=0)


def _wkv1_kernel(r_ref, cum_ref, kd_ref, v_ref, kkn_ref, b_ref,
                 rp_ref, y0_ref, g_ref, h_ref, *, nchunk, npair, unroll, reverse):
    lane = lax.broadcasted_iota(jnp.int32, (CHUNK, PAIR), 1)
    head0 = lane < RW_HEAD
    col = jnp.where(head0, lane, lane - RW_HEAD)
    row = lax.broadcasted_iota(jnp.int32, (CHUNK, PAIR), 0)
    strict = (row < col) if reverse else (row > col)
    incl = (row <= col) if reverse else (row >= col)
    eye = row == col
    first, last = (CHUNK - 1, 0) if reverse else (0, CHUNK - 1)
    bd = functools.partial(_pair_rows, head0=head0)

    def prepare(rows, ln):
        cum = cum_ref[rows, ln]
        before = pltpu.roll(cum, CHUNK - 1 if reverse else 1, 0)
        cum_ex = jnp.where(row == first, 0.0, before)
        kk = kkn_ref[rows, ln].astype(F32)
        b = b_ref[rows, ln].astype(F32)
        kd = kd_ref[rows, ln].astype(F32)
        tot = cum[last:last + 1, :]
        e_neg = jnp.exp(-cum)
        e_tail = jnp.exp(tot - cum)
        w = dict(tot=tot)
        w["at"] = (-kk * jnp.exp(cum_ex)).astype(BF16)
        w["rt"] = r_ref[rows, ln].astype(F32) * jnp.exp(cum)
        w["bk_in"] = jnp.concatenate([bd((b * e_neg).astype(BF16)), bd((kd * e_neg).astype(BF16))], axis=0)
        w["bk_out"] = jnp.concatenate([(b * e_tail).astype(BF16), (kd * e_tail).astype(BF16)], axis=0)
        w["v"] = v_ref[rows, ln]
        return w

    def start(w):
        m1 = _dot_nt(jnp.concatenate([w["at"], w["rt"].astype(BF16)], axis=0), w["bk_in"])
        a_ab = jnp.where(strict, m1[:CHUNK, :PAIR], 0.0)
        w["a_kv"] = jnp.concatenate([jnp.where(strict, m1[:CHUNK, PAIR:], 0.0),
                                     jnp.where(incl, m1[CHUNK:, PAIR:], 0.0)], axis=0).astype(BF16)
        w["a_rb"] = jnp.where(incl, m1[CHUNK:, :PAIR], 0.0).astype(BF16)
        w["pw"] = a_ab.astype(BF16)
        w["inv"] = jnp.where(eye, 1.0, a_ab)

    def square(w):
        w["pw"] = _dot(w["pw"], bd(w["pw"])).astype(BF16)

    def square_and_extend(w):
        both = _dot(jnp.concatenate([w["pw"], w["inv"].astype(BF16)], axis=0), bd(w["pw"]))
        w["pw"] = both[:CHUNK].astype(BF16)
        w["inv"] = w["inv"] + both[CHUNK:]

    def extend(w):
        w["inv"] = w["inv"] + _dot(w["inv"].astype(BF16), bd(w["pw"]))

    def values(w):
        w["av"] = _dot(w["a_kv"], bd(w["v"]))

    def solve(w):
        rhs = jnp.concatenate([bd(w["at"]), bd(w["av"][:CHUNK].astype(BF16))], axis=1)
        w["wu"] = _dot(w["inv"].astype(BF16), rhs).astype(BF16)

    def finish(w):
        rows, ln = w["rows"], w["ln"]
        wu = w["wu"]
        rw = _dot(w["a_rb"], jnp.concatenate([bd(wu[:, :PAIR]), bd(wu[:, PAIR:])], axis=1))
        rp_ref[rows, ln] = (w["rt"] + rw[:, :PAIR]).astype(rp_ref.dtype)
        y0_ref[rows, ln] = (rw[:, PAIR:] + w["av"][CHUNK:]).astype(y0_ref.dtype)
        gfull = _dot_tn(w["bk_out"][:CHUNK], wu[:, :PAIR])
        gdiag = jnp.where(eye, jnp.exp(w["tot"]), 0.0)
        g_ref[rows, ln] = (jnp.where(head0, gfull[:CHUNK], gfull[CHUNK:]) + gdiag).astype(g_ref.dtype)
        hfull = _dot_tn(w["bk_out"], jnp.concatenate([wu[:, PAIR:], w["v"]], axis=0))
        h_ref[rows, ln] = jnp.where(head0, hfull[:CHUNK], hfull[CHUNK:]).astype(h_ref.dtype)

    n_extend = int(math.log2(CHUNK)) - 2
    phases = [start, square] + [square_and_extend] * n_extend + [extend, values, solve, finish]

    def chunk_body(ci, carry):
        items = []
        for u in range(unroll):
            rows = pl.ds(pl.multiple_of((ci * unroll + u) * CHUNK, CHUNK), CHUNK)
            for p in range(npair):
                ln = slice(p * PAIR, (p + 1) * PAIR)
                w = prepare(rows, ln)
                w["rows"], w["ln"] = rows, ln
                items.append(w)
        for phase in phases:
            for w in items:
                phase(w)
        return carry

    lax.fori_loop(0, nchunk // unroll, chunk_body, 0)


def _wkv_stage1(r3, r_slab, cum, kd, v, kkn, b, z, reverse):
    n, d = v.shape
    npair = _pick(d // PAIR, (8, 4, 2, 1))
    lanes = npair * PAIR
    tb = _pick(n, (1024, 512, 256))

    def slab(c):
        return pl.BlockSpec((None, tb, lanes), lambda i, j, c=c: (c, i, j))

    tok = pl.BlockSpec((tb, lanes), lambda i, j: (i, j))
    return pl.pallas_call(
        functools.partial(_wkv1_kernel, nchunk=tb // CHUNK, npair=npair, unroll=WKV_UNROLL, reverse=reverse),
        out_shape=(jax.ShapeDtypeStruct((n, d), BF16),
                   jax.ShapeDtypeStruct((n, d), WKV_MAP_DTYPE),
                   jax.ShapeDtypeStruct((n, d), BF16),
                   jax.ShapeDtypeStruct((n, d), WKV_MAP_DTYPE)),
        grid=(n // tb, d // lanes),
        in_specs=[slab(r_slab), slab(z), slab(z), tok, tok, slab(z)],
        out_specs=(tok, tok, tok, tok),
        compiler_params=_cparams(("parallel", "parallel")),
        name="wkv_stage1_rev" if reverse else "wkv_stage1_fwd",
    )(r3, cum, kd, v, kkn, b)


def _wkv2_kernel(rpf_ref, y0f_ref, gf_ref, hf_ref, rpb_ref, y0b_ref, gb_ref, hb_ref,
                 yf_ref, yb_ref, stf_ref, stb_ref, *, npairs, nblocks, sub, n1, t1, t2):
    i = pl.program_id(0)

    @pl.when(i == 0)
    def _():
        stf_ref[...] = jnp.zeros_like(stf_ref)
        stb_ref[...] = jnp.zeros_like(stb_ref)

    lane = lax.broadcasted_iota(jnp.int32, (CHUNK, PAIR), 1)
    head0 = lane < RW_HEAD
    bd = functools.partial(_pair_rows, head0=head0)
    fwd_refs = (rpf_ref, y0f_ref, gf_ref, hf_ref, yf_ref, stf_ref)
    bwd_refs = (rpb_ref, y0b_ref, gb_ref, hb_ref, yb_ref, stb_ref)
    for s in range(sub):
        cf, cb = s, sub - 1 - s
        pos_f, _ = _seq_pos((i * sub + cf) * CHUNK, n1, t1, t2)
        pos_b, len_b = _seq_pos(((nblocks - 1 - i) * sub + cb) * CHUNK, n1, t1, t2)
        dirs = ((fwd_refs, pl.ds(cf * CHUNK, CHUNK), pos_f == 0),
                (bwd_refs, pl.ds(cb * CHUNK, CHUNK), pos_b + CHUNK == len_b))
        items = [(refs, rows, fresh, slice(p * PAIR, (p + 1) * PAIR))
                 for p in range(npairs) for refs, rows, fresh in dirs]
        results = []
        for (rp_ref, _, g_ref, _, _, st_ref), rows, fresh, ln in items:
            state = jnp.where(fresh, 0.0, st_ref[:, ln])
            hi, lo = _split(state)
            lhs = jnp.concatenate([rp_ref[rows, ln], g_ref[rows, ln]], axis=0)
            results.append(_dot(jnp.concatenate([lhs, lhs], axis=1), jnp.concatenate([bd(hi), bd(lo)], axis=0)))
        for ((_, y0_ref, _, h_ref, y_ref, st_ref), rows, _, ln), res in zip(items, results):
            y_ref[rows, ln] = (y0_ref[rows, ln].astype(F32) + res[:CHUNK]).astype(y_ref.dtype)
            st_ref[:, ln] = res[CHUNK:] + h_ref[rows, ln].astype(F32)


def _wkv_stage2(fwd, bwd, geom):
    n, d = fwd[0].shape
    n1, t1, t2 = geom
    sub = _pick(n // CHUNK, (WKV_SCAN_CHUNKS, 2, 1))
    nblocks = n // (sub * CHUNK)
    tok_f = pl.BlockSpec((sub * CHUNK, d), lambda i: (i, 0))
    tok_b = pl.BlockSpec((sub * CHUNK, d), lambda i: (nblocks - 1 - i, 0))
    return pl.pallas_call(
        functools.partial(_wkv2_kernel, npairs=d // PAIR, nblocks=nblocks, sub=sub, n1=n1, t1=t1, t2=t2),
        out_shape=(jax.ShapeDtypeStruct((n, d), WKV_OUT_DTYPE), jax.ShapeDtypeStruct((n, d), WKV_OUT_DTYPE)),
        grid=(nblocks,),
        in_specs=[tok_f] * 4 + [tok_b] * 4,
        out_specs=(tok_f, tok_b),
        scratch_shapes=[pltpu.VMEM((CHUNK, d), F32), pltpu.VMEM((CHUNK, d), F32)],
        compiler_params=_cparams(("arbitrary",)),
        name="wkv_stage2",
    )(*fwd, *bwd)


def _post_kernel(yf_ref, yb_ref, r_ref, kd_ref, v_ref, g_ref, rk_ref, lw_ref, lb_ref, o_ref, *, ngroups):
    r2 = lax.broadcasted_iota(jnp.int32, (PAIR, PAIR), 0)
    q2 = lax.broadcasted_iota(jnp.int32, (PAIR, PAIR), 1)
    same = (r2 < RW_HEAD) == (q2 < RW_HEAD)
    ones_bd = same.astype(BF16)
    mean_bd = (same.astype(F32) * (1.0 / RW_HEAD)).astype(BF16)
    mean_bd2 = jnp.concatenate([mean_bd, mean_bd], axis=0)
    for p in range(ngroups):
        ln = slice(p * PAIR, (p + 1) * PAIR)
        yf, yb = yf_ref[:, ln], yb_ref[:, ln]
        y = yf.astype(F32) + yb.astype(F32)
        assert yf.dtype == BF16 and yb.dtype == BF16
        mean = _dot(jnp.concatenate([yf, yb], axis=1), mean_bd2)
        dev = y - mean
        var = _dot((dev * dev).astype(BF16), mean_bd)
        yn = dev * lax.rsqrt(var + LNX_EPS) * lw_ref[:, ln] + lb_ref[:, ln]
        kd = kd_ref[0, :, ln].astype(F32) + kd_ref[1, :, ln].astype(F32)
        rkd = (r_ref[:, ln].astype(F32) * kd * rk_ref[:, ln]).astype(BF16)
        bonus = _dot(rkd, ones_bd) * v_ref[:, ln].astype(F32)
        o_ref[:, ln] = ((yn + bonus) * g_ref[:, ln].astype(F32)).astype(o_ref.dtype)


def _rwkv_post(yf, yb, rkv, kd, v, g, r_k, lnx_w, lnx_b):
    n, d = yf.shape
    tm = _pick(n, (512, 256, 128, 64, 32, 16, 8))
    tok = pl.BlockSpec((tm, d), lambda i: (i, 0))
    row = pl.BlockSpec((1, d), lambda i: (0, 0))
    return pl.pallas_call(
        functools.partial(_post_kernel, ngroups=d // PAIR),
        out_shape=jax.ShapeDtypeStruct((n, d), BF16),
        grid=(n // tm,),
        in_specs=[tok, tok, pl.BlockSpec((None, tm, d), lambda i: (0, i, 0)),
                  pl.BlockSpec((2, tm, d), lambda i: (0, i, 0)), tok, tok, row, row, row],
        out_specs=tok,
        compiler_params=_cparams(("parallel",)),
        name="rwkv_post",
    )(yf, yb, rkv, kd, v, g, r_k, lnx_w, lnx_b)


def _band_bias(tq, dil, nheads):
    nk = tq + 2 * BAND
    dist = np.abs(np.arange(nk)[None, :] - BAND - np.arange(tq)[:, None])
    slopes = np.exp2(-8.0 * np.arange(1, nheads + 1) / nheads)
    bias = -slopes[:, None, None] * (dil * dist)[None] * LOG2E
    return jnp.asarray(np.where((dist <= BAND)[None], bias, NEG_BIG), dtype=F32)


def _band_kernel(q_ref, kp_ref, kc_ref, kn_ref, vp_ref, vc_ref, vn_ref, bias_ref, o_ref, lse_ref,
                 *, tq, nsub, nheads, n1l, l1, l2):
    tb = tq * nsub
    pos0, seq_len = _seq_pos(pl.program_id(0) * tb, n1l, l1, l2)
    nk = tq + 2 * BAND
    col = lax.broadcasted_iota(jnp.int32, (1, nk), 1)
    head_lane = lax.broadcasted_iota(jnp.int32, (1, ATT_HEAD), 1)
    scale2 = ATT_HEAD ** -0.5 * LOG2E
    lanes = [slice(h * ATT_HEAD, (h + 1) * ATT_HEAD) for h in range(nheads)]

    def window(p_ref, c_ref, n_ref, sub, ln):
        lo, hi = sub * tq - BAND, sub * tq + tq + BAND
        parts = []
        if lo < 0:
            parts.append(p_ref[:, ln])
        parts.append(c_ref[max(lo, 0):min(hi, tb), ln])
        if hi > tb:
            parts.append(n_ref[:, ln])
        return jnp.concatenate(parts, axis=0) if len(parts) > 1 else parts[0]

    no_before = jnp.where((col < BAND) & (pos0 == 0), NEG_BIG, 0.0)
    no_after = jnp.where((col >= tq + BAND) & (pos0 + tb == seq_len), NEG_BIG, 0.0)
    blocks = [slice(c, min(c + LANES, nk)) for c in range(0, nk, LANES)]

    def mask_ends(s, sub):
        parts = [s[:, blk] for blk in blocks]
        if sub == 0:
            parts[0] = parts[0] + no_before[:, blocks[0]]
        if sub == nsub - 1:
            parts[-1] = parts[-1] + no_after[:, blocks[-1]]
        return jnp.concatenate(parts, axis=1)

    lse_all = [jnp.zeros((tq, ATT_HEAD), F32) for _ in range(nsub)]
    items = [(sub, h) for sub in range(nsub) for h in range(nheads)]
    for i0 in range(0, len(items), HEAD_BATCH):
        batch = items[i0:i0 + HEAD_BATCH]
        scores, probs, dens = {}, {}, {}
        for sub, h in batch:
            q = q_ref[sub * tq:(sub + 1) * tq, lanes[h]]
            keys = window(kp_ref, kc_ref, kn_ref, sub, lanes[h])
            scores[sub, h] = mask_ends(_dot_nt(q, keys) * scale2 + bias_ref[h], sub)
        for sub, h in batch:
            m = jnp.max(scores[sub, h], axis=-1, keepdims=True)
            probs[sub, h] = jnp.exp2(scores[sub, h] - m)
            dens[sub, h] = jnp.sum(probs[sub, h], axis=-1, keepdims=True)
            lse_all[sub] = lse_all[sub] + jnp.where(head_lane == h, (m + jnp.log2(dens[sub, h])) * LN2, 0.0)
        for sub, h in batch:
            vals = window(vp_ref, vc_ref, vn_ref, sub, lanes[h])
            out = _dot(probs[sub, h].astype(BF16), vals) / dens[sub, h]
            o_ref[sub * tq:(sub + 1) * tq, lanes[h]] = out.astype(o_ref.dtype)
    for sub in range(nsub):
        lse_ref[sub * tq:(sub + 1) * tq, :] = lse_all[sub]


def _band_attention(qkv, group, geom, d):
    dil, nl, _ = qkv.shape
    n1, t1, t2 = geom
    n1l, l1, l2 = n1 // dil, t1 // dil, t2 // dil
    tq = _pick(math.gcd(l1, l2), (128, 64))
    nsub = _pick(math.gcd(l1, l2) // tq, (BAND_TILES, 2, 1))
    tb = tq * nsub
    hb = tb // BAND
    nhalo = nl // BAND
    nheads = d // ATT_HEAD

    def cur(s):
        return pl.BlockSpec((None, tb, d), lambda i, c, s=s: (c, i, s))

    def before(s):
        return pl.BlockSpec((None, BAND, d), lambda i, c, s=s: (c, jnp.maximum(i * hb - 1, 0), s))

    def after(s):
        return pl.BlockSpec((None, BAND, d), lambda i, c, s=s: (c, jnp.minimum((i + 1) * hb, nhalo - 1), s))

    return pl.pallas_call(
        functools.partial(_band_kernel, tq=tq, nsub=nsub, nheads=nheads, n1l=n1l, l1=l1, l2=l2),
        out_shape=(jax.ShapeDtypeStruct((dil, nl, d), BF16),
                   jax.ShapeDtypeStruct((dil, nl, ATT_HEAD), F32)),
        grid=(nl // tb, dil),
        in_specs=[cur(0), before(1), cur(1), after(1), before(2), cur(2), after(2),
                  pl.BlockSpec((nheads, tq, tq + 2 * BAND), lambda i, c: (0, 0, 0))],
        out_specs=(pl.BlockSpec((None, tb, d), lambda i, c: (c, i, 0)),
                   pl.BlockSpec((None, tb, ATT_HEAD), lambda i, c: (c, i, 0))),
        compiler_params=_cparams(("parallel", "parallel")),
        name=f"band_attention_g{group}",
    )(qkv, qkv, qkv, qkv, qkv, qkv, qkv, _band_bias(tq, dil, nheads))


def _combine_kernel(o0_ref, o1_ref, o2_ref, l0_ref, l1_ref, l2_ref, e_ref, out_ref, lse_scr, o_scr, *, dils):
    tm = out_ref.shape[0]

    def token_order(ref, scr, dil):
        if dil == 1:
            return ref[0].astype(F32)
        rows = tm // dil
        ngrp = ref.shape[2] // LANES
        for k in range(ngrp):
            for c in range(dil):
                scr[k, pl.ds(c, rows, stride=dil), :] = ref[c, :, k * LANES:(k + 1) * LANES].astype(F32)
        return jnp.concatenate([scr[k] for k in range(ngrp)], axis=1)

    lses = [token_order(l_ref, lse_scr.at[pl.ds(g, 1)], dil)
            for g, (l_ref, dil) in enumerate(zip((l0_ref, l1_ref, l2_ref), dils))]
    m = jnp.maximum(jnp.maximum(lses[0], lses[1]), lses[2])
    ws = [jnp.exp(l - m) for l in lses]
    tot = ws[0] + ws[1] + ws[2]
    e = e_ref[...]
    acc = None
    for w, o_ref, dil in zip(ws, (o0_ref, o1_ref, o2_ref), dils):
        hi, lo = _split(w / tot)
        term = _dot(jnp.concatenate([hi, lo], axis=1), e) * token_order(o_ref, o_scr, dil)
        acc = term if acc is None else acc + term
    out_ref[...] = acc.astype(out_ref.dtype)


def _combine(os_, lses, d):
    dils = tuple(o.shape[0] for o in os_)
    n = os_[0].shape[0] * os_[0].shape[1]
    tm = _pick(n, (512, 256))
    head_of_lane = jnp.arange(d, dtype=jnp.int32) // ATT_HEAD
    expand = (jnp.arange(ATT_HEAD, dtype=jnp.int32)[:, None] == head_of_lane[None, :]).astype(BF16)
    expand = jnp.concatenate([expand, expand], axis=0)

    def classes(dil, width):
        return pl.BlockSpec((dil, tm // dil, width), lambda i: (0, i, 0))

    return pl.pallas_call(
        functools.partial(_combine_kernel, dils=dils),
        out_shape=jax.ShapeDtypeStruct((n, d), BF16),
        grid=(n // tm,),
        in_specs=[classes(dil, d) for dil in dils] + [classes(dil, ATT_HEAD) for dil in dils]
                 + [pl.BlockSpec((2 * ATT_HEAD, d), lambda i: (0, 0))],
        out_specs=pl.BlockSpec((tm, d), lambda i: (i, 0)),
        scratch_shapes=[pltpu.VMEM((len(dils), tm, LANES), F32), pltpu.VMEM((d // LANES, tm, LANES), F32)],
        compiler_params=_cparams(("parallel",)),
        name="attention_combine",
    )(*os_, *lses, expand)


def _rwkv_layer(x, gain, p, v_first, vres, geom):
    parts = None if not isinstance(x, list) else x
    if parts is None:
        xs = _token_shift(x, gain, p["mu"], geom)
    else:
        total = sum(xp.shape[0] for xp, _ in parts)
        xs, off = None, 0
        for xp, tp in parts:
            xs = _token_shift(xp, gain, p["mu"], (xp.shape[0], tp, tp), row_off=off, out_rows=total, prev=xs)
            off += xp.shape[0]
    rkv = _matmul_batched(xs, p["w_rkv"], 3, RKV_DTYPE)
    if v_first is None:
        v_first = rkv
    cum, kd, b, kkn, v, g = _rwkv_mid(xs, rkv, v_first, p, vres)
    maps = [_wkv_stage1(rkv, 0, cum, kd, v, kkn, b, z, reverse=(z == 1)) for z in range(2)]
    yf, yb = _wkv_stage2(maps[0], maps[1], geom)
    o = _rwkv_post(yf, yb, rkv, kd, v, g, p["r_k"], p["lnx_w"], p["lnx_b"])
    if parts is None:
        return _matmul_residual(o, p["w_o"], x), v_first
    out, off = None, 0
    for xp, _ in parts:
        out = _matmul_residual(o, p["w_o"], xp, row_off=off, out_rows=o.shape[0], prev=out)
        off += xp.shape[0]
    return out, v_first


def _attention_layer(x, gain, w_qkv, w_o, geom):
    d = x.shape[1]
    assert all((window // 2) // dil == BAND for window, dil in DIL_PATTERNS)
    qkvs = _matmul_norm_classes(x, gain, w_qkv, [dil for _, dil in DIL_PATTERNS], BF16)
    outs, lses = zip(*[_band_attention(qkv, gi, geom, d) for gi, qkv in enumerate(qkvs)])
    return _matmul_residual(_combine(outs, lses, d), w_o, x)


def kernel(x_prompt, x_sample, ln1, ln2, ln_f, rw_mu, rw_w_rkv, rw_w0, rw_w1, rw_w2, rw_a0, rw_a1, rw_a2, rw_v0, rw_v1, rw_v2, rw_g1, rw_g2, rw_k_k, rw_k_a, rw_r_k, rw_lnx_w, rw_lnx_b, rw_w_o, at_w_qkv, at_w_o, ffn_w_gate, ffn_w_up, ffn_w_down):
    b1, t1, d = x_prompt.shape
    b2, t2, _ = x_sample.shape
    n1, n2 = b1 * t1, b2 * t2
    geom = (n1, t1, t2)
    depth = ln1.shape[0]
    assert d % PAIR == 0 and d % ATT_HEAD == 0
    max_dil = max(dil for _, dil in DIL_PATTERNS)
    assert t1 % (max_dil * BAND) == 0 and t2 % (max_dil * BAND) == 0

    assert depth >= 1
    x = [(x_prompt.reshape(n1, d), t1), (x_sample.reshape(n2, d), t2)]
    bf = lambda w: w.astype(BF16)
    row = lambda w: w.reshape(1, d)

    v_first = None
    for i in range(depth):
        j = i // 2
        if i % 2 == 0:
            p = dict(mu=rw_mu[j], w_rkv=bf(rw_w_rkv[j]), w0=rw_w0[j], w1=bf(rw_w1[j]), w2=bf(rw_w2[j]),
                     a0=rw_a0[j], a1=bf(rw_a1[j]), a2=bf(rw_a2[j]), g1=bf(rw_g1[j]), g2=bf(rw_g2[j]),
                     k_k=row(rw_k_k[j]), k_a=row(rw_k_a[j]), r_k=rw_r_k[j].reshape(1, d),
                     lnx_w=row(rw_lnx_w[j]), lnx_b=row(rw_lnx_b[j]), w_o=bf(rw_w_o[j]))
            vres = None if j == 0 else (row(rw_v0[j - 1]), bf(rw_v1[j - 1]), bf(rw_v2[j - 1]))
            x, v_first = _rwkv_layer(x, row(ln1[i]), p, v_first, vres, geom)
        else:
            w_qkv = bf(at_w_qkv[j]).reshape(d, -1)
            x = _attention_layer(x, row(ln1[i]), w_qkv, bf(at_w_o[j]), geom)
        ffn_w = (row(ln2[i]), bf(ffn_w_gate[i]), bf(ffn_w_up[i]), bf(ffn_w_down[i]), row(ln_f))
        if i < depth - 1:
            x = _ffn(x, *ffn_w, final_norm=False)
    y_prompt = _ffn(x, *ffn_w, final_norm=True, rows=(0, n1))
    y_sample = _ffn(x, *ffn_w, final_norm=True, rows=(n1, n2))
    return y_prompt.reshape(b1, t1, d), y_sample.reshape(b2, t2, d)
```
